```python
import jax
import jax.numpy as jnp
from jax import lax
import numpy as np

D_MODEL = 1024
BATCH = 8
SEQ = 2048
DEPTH = 1

GRID_W = 64
CTX_LEN = 256
EPS = 1e-6

S5_WIDTH = D_MODEL // 2
S5_GROUP = 16
S5_GROUPS = S5_WIDTH // S5_GROUP
S5_STATE = 64

GDN_HEADS = 4
GDN_DK = 128
GDN_DV = 128
GDN_QK = GDN_HEADS * GDN_DK
GDN_V = GDN_HEADS * GDN_DV
GDN_CONV_CH = 2 * GDN_QK + GDN_V
CONV_K = 5
CHUNK = 64

N_EXPERTS = 32
TOP_K = 4
D_EXPERT = D_MODEL
SWIGLU_LIMIT = 7.0
SWIGLU_ALPHA = 1.702

N_BRANCH = 2
IN_SPLITS = (S5_WIDTH, GDN_CONV_CH, GDN_V, 2 * GDN_HEADS, 2 * GDN_HEADS, N_BRANCH * D_MODEL)
IN_COLS = sum(IN_SPLITS)

kernel_name = "hybrid_s5_gdn_moe_prefix_dit_block"


def _rmsnorm(x, gain):
    xf = x.astype(jnp.float32)
    y = xf * lax.rsqrt(jnp.mean(xf * xf, axis=-1, keepdims=True) + EPS)
    return (y * gain.astype(jnp.float32)).astype(x.dtype)


def _split_in(z):
    offs = np.cumsum(IN_SPLITS)[:-1].tolist()
    u, qkv, gate, b_raw, a_raw, br = jnp.split(z, offs, axis=-1)
    lead = z.shape[:-1]
    return (u, qkv, gate, b_raw.reshape(lead + (2, GDN_HEADS)),
            a_raw.reshape(lead + (2, GDN_HEADS)), br)


def _short_conv(u, w):
    y = lax.conv_general_dilated(u, w[:, None, :].astype(u.dtype), window_strides=(1,), padding='SAME',
                                 dimension_numbers=('NWC', 'WIO', 'NWC'),
                                 feature_group_count=u.shape[-1])
    return jax.nn.silu(y)


def _s5_discretise(lam_re, lam_im, log_step, b_re, b_im):
    f32 = jnp.float32
    lr = jnp.minimum(lam_re.astype(f32), -1e-4)
    li = lam_im.astype(f32)
    dt = jnp.exp(log_step.astype(f32))[:, None]
    mag = jnp.exp(lr * dt)
    ar, ai = mag * jnp.cos(li * dt), mag * jnp.sin(li * dt)
    den = lr * lr + li * li
    fr = ((ar - 1.0) * lr + ai * li) / den
    fi = (ai * lr - (ar - 1.0) * li) / den
    br, bi = b_re.astype(f32), b_im.astype(f32)
    bbr = fr[..., None] * br - fi[..., None] * bi
    bbi = fr[..., None] * bi + fi[..., None] * br
    return ar, ai, bbr, bbi


def _cmul_combine(e1, e2):
    a1r, a1i, b1r, b1i = e1
    a2r, a2i, b2r, b2i = e2
    return (a2r * a1r - a2i * a1i, a2r * a1i + a2i * a1r,
            a2r * b1r - a2i * b1i + b2r, a2r * b1i + a2i * b1r + b2i)


def _complex_scan(ar, ai, bur, bui, reverse):
    ar = jnp.broadcast_to(ar, bur.shape)
    ai = jnp.broadcast_to(ai, bur.shape)
    return lax.associative_scan(_cmul_combine, (ar, ai, bur, bui), axis=1, reverse=reverse)


def _s5_drive(ug, bbr, bbi):
    return (jnp.einsum('blgc,gnc->blgn', ug, bbr), jnp.einsum('blgc,gnc->blgn', ug, bbi))


def _s5_readout(hr, hi, cr, ci):
    return jnp.einsum('blgn,gcn->blgc', hr, cr) - jnp.einsum('blgn,gcn->blgc', hi, ci)


def _s5_glu(y, w_glu, b_glu, dtype):
    b, l = y.shape[:2]
    z = jax.nn.gelu(y.reshape(b, l, S5_WIDTH)).astype(dtype)
    return z * jax.nn.sigmoid(z @ w_glu + b_glu)


def _s5_mixer(u_ctx, u_lat, lam_re, lam_im, log_step, b_re, b_im, c_re, c_im, d_skip, w_glu, b_glu,
              with_ctx_out):
    dtype = u_lat.dtype
    bsz, lat_len, _ = u_lat.shape
    ctx_len = u_ctx.shape[1]
    ug_c = u_ctx.astype(jnp.float32).reshape(bsz, ctx_len, S5_GROUPS, S5_GROUP)
    ug_l = u_lat.astype(jnp.float32).reshape(bsz, lat_len, S5_GROUPS, S5_GROUP)
    d = d_skip.astype(jnp.float32).reshape(S5_GROUPS, S5_GROUP)
    y_l = d * ug_l
    y_c = d * ug_c
    for direction in range(2):
        rev = direction == 1
        ar, ai, bbr, bbi = _s5_discretise(lam_re[direction], lam_im[direction], log_step[direction],
                                          b_re[direction], b_im[direction])
        cr = c_re[direction].astype(jnp.float32)
        ci = c_im[direction].astype(jnp.float32)
        _, _, hcr, hci = _complex_scan(ar, ai, *_s5_drive(ug_c, bbr, bbi), rev)
        end = 0 if rev else -1
        h0r, h0i = hcr[:, end][:, None], hci[:, end][:, None]
        pr, pi, hr, hi = _complex_scan(ar, ai, *_s5_drive(ug_l, bbr, bbi), rev)
        hr, hi = hr + pr * h0r - pi * h0i, hi + pr * h0i + pi * h0r
        y_l = y_l + _s5_readout(hr, hi, cr, ci)
        if with_ctx_out:
            y_c = y_c + _s5_readout(hcr, hci, cr, ci)
    out_l = _s5_glu(y_l, w_glu, b_glu, dtype)
    out_c = _s5_glu(y_c, w_glu, b_glu, dtype) if with_ctx_out else None
    return out_c, out_l


def _l2norm(t):
    return t * lax.rsqrt(jnp.sum(t * t, axis=-1, keepdims=True) + EPS)


def _gdn_inputs(qkv, b_raw, a_raw, a_log, dt_bias):
    bsz, length, _ = qkv.shape
    f32 = jnp.float32
    q, k, v = jnp.split(qkv.astype(f32), [GDN_QK, 2 * GDN_QK], axis=-1)
    q = _l2norm(q.reshape(bsz, length, GDN_HEADS, GDN_DK)) * (GDN_DK ** -0.5)
    k = _l2norm(k.reshape(bsz, length, GDN_HEADS, GDN_DK))
    v = v.reshape(bsz, length, GDN_HEADS, GDN_DV)
    beta = jax.nn.sigmoid(b_raw.astype(f32))
    g = -jnp.exp(a_log.astype(f32)) * jax.nn.softplus(a_raw.astype(f32) + dt_bias.astype(f32))
    return q, k, v, beta, g


def _gdn_chunked(q, k, v, g, beta, s0):
    bsz, length, heads, _ = q.shape
    dv = v.shape[-1]
    n = length // CHUNK

    def blocks(t):
        t = t.reshape((bsz, n, CHUNK) + t.shape[2:])
        return jnp.moveaxis(t, 3, 1)

    qb, kb, vb, gb, bb = (blocks(t) for t in (q, k, v, g, beta))
    gcum = jnp.cumsum(gb, axis=-1)
    idx = jnp.arange(CHUNK)
    incl = idx[:, None] >= idx[None, :]
    strict = idx[:, None] > idx[None, :]
    decay = jnp.exp(jnp.where(incl, gcum[..., :, None] - gcum[..., None, :], -jnp.inf))
    kbeta = kb * bb[..., None]
    a_mat = jnp.where(strict, jnp.einsum('bhncd,bhnsd->bhncs', kbeta, kb) * decay, 0.0)
    rhs = jnp.concatenate([vb * bb[..., None], kbeta * jnp.exp(gcum)[..., None]], axis=-1)
    sol = lax.linalg.triangular_solve(a_mat, rhs, left_side=True, lower=True, unit_diagonal=True)
    u_blk, w_blk = sol[..., :dv], sol[..., dv:]
    qk = jnp.einsum('bhncd,bhnsd->bhncs', qb, kb) * decay
    glast = gcum[..., -1]
    q_dec = qb * jnp.exp(gcum)[..., None]
    k_dec = kb * jnp.exp(glast[..., None] - gcum)[..., None]
    xs = tuple(jnp.moveaxis(t, 2, 0) for t in (q_dec, k_dec, u_blk, w_blk, qk, jnp.exp(glast)))

    def step(s, inp):
        qg, kd, u_i, w_i, qk_i, dl = inp
        v_new = u_i - jnp.einsum('bhck,bhkv->bhcv', w_i, s)
        o = jnp.einsum('bhck,bhkv->bhcv', qg, s) + jnp.einsum('bhcs,bhsv->bhcv', qk_i, v_new)
        s = s * dl[..., None, None] + jnp.einsum('bhck,bhcv->bhkv', kd, v_new)
        return s, o

    s_final, o = lax.scan(step, s0, xs)
    o = jnp.moveaxis(jnp.moveaxis(o, 0, 2), 1, 3).reshape(bsz, length, heads, dv)
    return o, s_final


def _gdn_direction(inp, direction, s_init):
    q, k, v, beta, g = inp
    flip = (lambda t: jnp.flip(t, axis=1)) if direction == 1 else (lambda t: t)
    o, s = _gdn_chunked(flip(q), flip(k), flip(v), flip(g[:, :, direction]), flip(beta[:, :, direction]), s_init)
    return flip(o), s


def _gdn_out(o, gate, norm_w, dtype):
    bsz, length = o.shape[:2]
    on = o * lax.rsqrt(jnp.mean(o * o, axis=-1, keepdims=True) + EPS) * norm_w.astype(jnp.float32)
    return (on.reshape(bsz, length, GDN_V) * jax.nn.silu(gate.astype(jnp.float32))).astype(dtype)


def _gdn_mixer(qkv_ctx, b_ctx, a_ctx, gate_ctx, qkv_lat, b_lat, a_lat, gate_lat, a_log, dt_bias, norm_w,
               with_ctx_out):
    dtype = qkv_lat.dtype
    ctx_in = _gdn_inputs(qkv_ctx, b_ctx, a_ctx, a_log, dt_bias)
    lat_in = _gdn_inputs(qkv_lat, b_lat, a_lat, a_log, dt_bias)
    s0 = jnp.zeros((qkv_lat.shape[0], GDN_HEADS, GDN_DK, GDN_DV), jnp.float32)
    o_c, o_l = 0.0, 0.0
    for direction in range(2):
        oc, sc = _gdn_direction(ctx_in, direction, s0)
        ol, _ = _gdn_direction(lat_in, direction, sc)
        o_c, o_l = o_c + oc, o_l + ol
    y_l = _gdn_out(o_l, gate_lat, norm_w, dtype)
    y_c = _gdn_out(o_c, gate_ctx, norm_w, dtype) if with_ctx_out else None
    return y_c, y_l


def _token_mixer(h, hc, rows, w_in, s5_lam_re, s5_lam_im, s5_log_step, s5_b_re, s5_b_im, s5_c_re, s5_c_im,
                 s5_d, s5_w_glu, s5_b_glu, gdn_conv, gdn_a_log, gdn_dt_bias, gdn_norm,
                 w_branch_a, w_branch_b, w_out, with_ctx_out):
    bsz, length, _ = h.shape
    u, qkv, gate, b_raw, a_raw, br = _split_in(h @ w_in)
    uc, qkvc, gatec, b_rawc, a_rawc, brc = _split_in(hc @ w_in)
    qkv = _short_conv(qkv.reshape(bsz * rows, GRID_W, GDN_CONV_CH), gdn_conv).reshape(bsz, length, GDN_CONV_CH)
    qkvc = _short_conv(qkvc, gdn_conv)
    ya_c, ya = _s5_mixer(uc, u, s5_lam_re, s5_lam_im, s5_log_step, s5_b_re, s5_b_im, s5_c_re, s5_c_im,
                         s5_d, s5_w_glu, s5_b_glu, with_ctx_out)
    yb_c, yb = _gdn_mixer(qkvc, b_rawc, a_rawc, gatec, qkv, b_raw, a_raw, gate, gdn_a_log, gdn_dt_bias,
                          gdn_norm, with_ctx_out)

    def merge(ya_, yb_, br_):
        ga, gb = jnp.split(jax.nn.sigmoid(br_), N_BRANCH, axis=-1)
        return (ga * (ya_ @ w_branch_a) + gb * (yb_ @ w_branch_b)) @ w_out

    out = merge(ya, yb, br)
    out_c = merge(ya_c, yb_c, brc) if with_ctx_out else None
    return out, out_c


def _moe(h, w_router, b_router, w_gate_up, b_gate_up, w_down, b_down):
    shp = h.shape
    t = h.reshape(-1, shp[-1])
    logits = (t @ w_router + b_router).astype(jnp.float32)
    top_val, top_idx = lax.top_k(logits, TOP_K)
    weights = jax.nn.softmax(top_val, axis=-1)
    combine = jnp.einsum('tk,tke->te', weights,
                         jax.nn.one_hot(top_idx, N_EXPERTS, dtype=jnp.float32)).astype(h.dtype)
    out = jnp.zeros_like(t)
    for e in range(N_EXPERTS):
        gu = t @ w_gate_up[e] + b_gate_up[e]
        gate, up = gu[:, :D_EXPERT], gu[:, D_EXPERT:]
        gate = jnp.minimum(gate, SWIGLU_LIMIT)
        up = jnp.clip(up, -SWIGLU_LIMIT, SWIGLU_LIMIT)
        act = (up + 1.0) * gate * jax.nn.sigmoid(gate * SWIGLU_ALPHA)
        out = out + combine[:, e:e + 1] * (act @ w_down[e] + b_down[e])
    return out.reshape(shp)


def setup_inputs(seed: int = 0) -> dict:
    key = jax.random.key(seed)
    ks = iter(jax.random.split(key, 48))
    f32 = jnp.float32

    def nrm(shape, scale):
        return jax.random.normal(next(ks), shape, f32) * scale

    def unif(shape, lo, hi):
        return jax.random.uniform(next(ks), shape, f32, lo, hi)

    d = D_MODEL
    lam_im_base = jnp.pi * jnp.arange(S5_STATE, dtype=f32)
    dt_init = jnp.exp(unif((DEPTH, 2, GDN_HEADS), float(np.log(1e-3)), float(np.log(1e-1))))
    return {
        'x': nrm((BATCH, SEQ, d), 1.0),
        'c': nrm((BATCH, d), 1.0),
        'ctx': nrm((BATCH, CTX_LEN, d), 1.0),
        'c_ctx': nrm((d,), 1.0),
        'w_mod': nrm((DEPTH, d, 6 * d), 0.5 * d ** -0.5),
        'b_mod': nrm((DEPTH, 6 * d), 0.01),
        'norm1': 1.0 + nrm((DEPTH, d), 0.02),
        'w_in': nrm((DEPTH, d, IN_COLS), d ** -0.5),
        's5_lam_re': -0.5 + nrm((DEPTH, 2, S5_GROUPS, S5_STATE), 0.01),
        's5_lam_im': lam_im_base + nrm((DEPTH, 2, S5_GROUPS, S5_STATE), 0.01),
        's5_log_step': unif((DEPTH, 2, S5_GROUPS), float(np.log(1e-3)), float(np.log(1e-1))),
        's5_b_re': nrm((DEPTH, 2, S5_GROUPS, S5_STATE, S5_GROUP), (2 * S5_GROUP) ** -0.5),
        's5_b_im': nrm((DEPTH, 2, S5_GROUPS, S5_STATE, S5_GROUP), (2 * S5_GROUP) ** -0.5),
        's5_c_re': nrm((DEPTH, 2, S5_GROUPS, S5_GROUP, S5_STATE), 0.5),
        's5_c_im': nrm((DEPTH, 2, S5_GROUPS, S5_GROUP, S5_STATE), 0.5),
        's5_d': nrm((DEPTH, S5_WIDTH), 0.5),
        's5_w_glu': nrm((DEPTH, S5_WIDTH, S5_WIDTH), S5_WIDTH ** -0.5),
        's5_b_glu': nrm((DEPTH, S5_WIDTH), 0.01),
        'gdn_conv': nrm((DEPTH, CONV_K, GDN_CONV_CH), CONV_K ** -0.5),
        'gdn_a_log': jnp.log(unif((DEPTH, 2, GDN_HEADS), 1.0, 16.0)),
        'gdn_dt_bias': jnp.log(jnp.expm1(dt_init)),
        'gdn_norm': 1.0 + nrm((DEPTH, GDN_DV), 0.02),
        'w_branch_a': nrm((DEPTH, S5_WIDTH, d), S5_WIDTH ** -0.5),
        'w_branch_b': nrm((DEPTH, GDN_V, d), GDN_V ** -0.5),
        'w_out': nrm((DEPTH, d, d), d ** -0.5),
        'norm2': 1.0 + nrm((DEPTH, d), 0.02),
        'w_router': nrm((DEPTH, d, N_EXPERTS), d ** -0.5),
        'b_router': nrm((DEPTH, N_EXPERTS), 0.01),
        'w_gate_up': nrm((DEPTH, N_EXPERTS, d, 2 * D_EXPERT), d ** -0.5),
        'b_gate_up': nrm((DEPTH, N_EXPERTS, 2 * D_EXPERT), 0.01),
        'w_down': nrm((DEPTH, N_EXPERTS, D_EXPERT, d), D_EXPERT ** -0.5),
        'b_down': nrm((DEPTH, N_EXPERTS, d), 0.01),
        'norm_f': 1.0 + nrm((d,), 0.02),
    }


def reference(x, c, ctx, c_ctx, w_mod, b_mod, norm1, w_in, s5_lam_re, s5_lam_im, s5_log_step, s5_b_re,
              s5_b_im, s5_c_re, s5_c_im, s5_d, s5_w_glu, s5_b_glu, gdn_conv, gdn_a_log, gdn_dt_bias, gdn_norm,
              w_branch_a, w_branch_b, w_out, norm2, w_router, b_router, w_gate_up, b_gate_up, w_down, b_down,
              norm_f):
    rows = x.shape[1] // GRID_W
    for layer in range(DEPTH):
        last = layer == DEPTH - 1
        mod = jax.nn.silu(c) @ w_mod[layer] + b_mod[layer]
        sh1, sc1, gt1, sh2, sc2, gt2 = jnp.split(mod[:, None, :], 6, axis=-1)
        mod_c = jax.nn.silu(c_ctx) @ w_mod[layer] + b_mod[layer]
        csh1, csc1, cgt1, csh2, csc2, cgt2 = jnp.split(mod_c, 6, axis=-1)
        h = _rmsnorm(x, norm1[layer]) * (1.0 + sc1) + sh1
        hc = _rmsnorm(ctx, norm1[layer]) * (1.0 + csc1) + csh1
        mix, mix_c = _token_mixer(h, hc, rows, w_in[layer], s5_lam_re[layer], s5_lam_im[layer],
                                  s5_log_step[layer], s5_b_re[layer], s5_b_im[layer], s5_c_re[layer],
                                  s5_c_im[layer], s5_d[layer], s5_w_glu[layer], s5_b_glu[layer],
                                  gdn_conv[layer], gdn_a_log[layer], gdn_dt_bias[layer], gdn_norm[layer],
                                  w_branch_a[layer], w_branch_b[layer], w_out[layer], not last)
        x = x + gt1 * mix
        h2 = _rmsnorm(x, norm2[layer]) * (1.0 + sc2) + sh2
        x = x + gt2 * _moe(h2, w_router[layer], b_router[layer], w_gate_up[layer], b_gate_up[layer],
                           w_down[layer], b_down[layer])
        if not last:
            ctx = ctx + cgt1 * mix_c
            hc2 = _rmsnorm(ctx, norm2[layer]) * (1.0 + csc2) + csh2
            ctx = ctx + cgt2 * _moe(hc2, w_router[layer], b_router[layer], w_gate_up[layer],
                                    b_gate_up[layer], w_down[layer], b_down[layer])
    return _rmsnorm(x, norm_f)
```

```python
import functools

import numpy as np
import jax
import jax.numpy as jnp
from jax import lax
from jax.experimental import pallas as pl
from jax.experimental.pallas import tpu as pltpu

F32 = jnp.float32
BF16 = jnp.bfloat16
HIGHEST = lax.Precision.HIGHEST

EPS = 1e-6
GRID_W = 64

S5_GROUP = 16
S5_STATE = 64
S5_Q = 16

GDN_HEADS = 4
GDN_DK = 128
GDN_CHUNK = 64
CONV_K = 5

N_EXPERTS = 32
TOP_K = 4
SWIGLU_LIMIT = 7.0
SWIGLU_ALPHA = 1.702

ROW_TILE = 256
MOE_TILE = 512
VMEM_LIMIT = 56 * 1024 * 1024


def _cparams(sem):
    return pltpu.CompilerParams(dimension_semantics=sem, vmem_limit_bytes=VMEM_LIMIT)


def _sigmoid(v):
    return 1.0 / (1.0 + jnp.exp(-v))


def _softplus(v):
    return jnp.maximum(v, 0.0) + jnp.log(1.0 + jnp.exp(-jnp.abs(v)))


def _dot(a, b):
    return jnp.dot(a, b, preferred_element_type=F32)


def _dot_nt(a, b):
    return lax.dot_general(a, b, (((1,), (1,)), ((), ())), preferred_element_type=F32)


def _dot_tn(a, b):
    return lax.dot_general(a, b, (((0,), (0,)), ((), ())), preferred_element_type=F32)


def _mod_kernel(c_ref, w_ref, b_ref, o_ref):
    cc = c_ref[...]
    s = cc * _sigmoid(cc)
    o_ref[...] = jnp.dot(s, w_ref[...], precision=HIGHEST, preferred_element_type=F32) + b_ref[...]


def _modulation(cc, w_mod, b_mod):
    rows, d = cc.shape
    n = w_mod.shape[1]
    bn = d
    return pl.pallas_call(
        _mod_kernel,
        grid=(n // bn,),
        in_specs=[pl.BlockSpec((rows, d), lambda j: (0, 0)),
                  pl.BlockSpec((d, bn), lambda j: (0, j)),
                  pl.BlockSpec((1, bn), lambda j: (0, j))],
        out_specs=pl.BlockSpec((rows, bn), lambda j: (0, j)),
        out_shape=jax.ShapeDtypeStruct((rows, n), F32),
        compiler_params=_cparams(("arbitrary",)),
        name="adaln_mod",
    )(cc, w_mod, b_mod.reshape(1, n))


def _inproj_kernel(x_ref, c_ref, sc_ref, sh_ref, g_ref, w_ref, wlo_ref, wt_ref, wtlo_ref,
                   u_ref, qkv_ref, gate_ref, br_ref, ba_ref, bat_ref, *, n_lat_blocks):
    j = pl.program_id(1)
    is_ctx = j >= n_lat_blocks
    x = jnp.where(is_ctx, c_ref[0], x_ref[0])
    ms = jnp.mean(x * x, axis=-1, keepdims=True)
    y = x * lax.rsqrt(ms + EPS) * g_ref[...]
    h = y * (1.0 + sc_ref[0]) + sh_ref[0]
    h_hi = h.astype(BF16)
    h_lo = (h - h_hi.astype(F32)).astype(BF16)
    z = _dot(h_hi, w_ref[...])
    u_ref[0] = z[:, 0:512].astype(BF16)
    qkv_ref[0] = z[:, 512:2048].astype(BF16)
    gate_ref[0] = z[:, 2048:2560].astype(BF16)
    br_ref[0] = z[:, 2560:4608].astype(BF16)
    ba_ref[0] = z[:, 4608:4736] + z[:, 4736:4864] + _dot(h_lo, wlo_ref[...])
    rt = _dot_nt(wt_ref[...], h_hi)
    bat_ref[0] = rt[0:16] + rt[16:32] + _dot_nt(wtlo_ref[...], h_lo)


def _in_projection(x, ctx, mod3, norm1, w_main, w_lo, wt, wt_lo):
    b, l, d = x.shape
    lc = ctx.shape[1]
    tm = ROW_TILE
    nl, nc = l // tm, lc // tm
    nb = mod3.shape[0] // 6 - 1
    lt = l + lc
    ncols = w_main.shape[1]

    def x_map(i, j):
        return (i, jnp.minimum(j, nl - 1), 0)

    def c_map(i, j):
        return (i, jnp.maximum(j - nl, 0), 0)

    def mod_map(k):
        return lambda i, j: (jnp.where(j >= nl, nb, i) * 6 + k, 0, 0)

    def tok(width):
        return pl.BlockSpec((1, tm, width), lambda i, j: (i, j, 0))

    const2 = lambda i, j: (0, 0)
    outs = pl.pallas_call(
        functools.partial(_inproj_kernel, n_lat_blocks=nl),
        grid=(b, nl + nc),
        in_specs=[pl.BlockSpec((1, tm, d), x_map),
                  pl.BlockSpec((1, tm, d), c_map),
                  pl.BlockSpec((1, 1, d), mod_map(1)),
                  pl.BlockSpec((1, 1, d), mod_map(0)),
                  pl.BlockSpec((1, d), const2),
                  pl.BlockSpec((d, ncols), const2),
                  pl.BlockSpec((d, 128), const2),
                  pl.BlockSpec((32, d), const2),
                  pl.BlockSpec((16, d), const2)],
        out_specs=[tok(512), tok(1536), tok(512), tok(2048), tok(128),
                   pl.BlockSpec((1, 16, tm), lambda i, j: (i, 0, j))],
        out_shape=[jax.ShapeDtypeStruct((b, lt, 512), BF16),
                   jax.ShapeDtypeStruct((b, lt, 1536), BF16),
                   jax.ShapeDtypeStruct((b, lt, 512), BF16),
                   jax.ShapeDtypeStruct((b, lt, 2048), BF16),
                   jax.ShapeDtypeStruct((b, lt, 128), F32),
                   jax.ShapeDtypeStruct((b, 16, lt), F32)],
        compiler_params=_cparams(("parallel", "arbitrary")),
        name="in_projection",
    )(x, ctx, mod3, mod3, norm1.reshape(1, d), w_main, w_lo, wt, wt_lo)
    return outs


def _s5_matrices(lam_re, lam_im, log_step, b_re, b_im, c_re, c_im, d_skip):
    q = S5_Q
    hp = dict(precision=HIGHEST)
    lr = jnp.minimum(lam_re.astype(F32), -1e-4)
    li = lam_im.astype(F32)
    dt = jnp.exp(log_step.astype(F32))[..., None]
    pw = jnp.arange(q + 1, dtype=F32)[:, None, None, None]
    mag = jnp.exp(pw * (lr * dt))
    ang = pw * (li * dt)
    pr, pi = mag * jnp.cos(ang), mag * jnp.sin(ang)
    ar, ai = pr[1], pi[1]
    den = lr * lr + li * li
    fr = ((ar - 1.0) * lr + ai * li) / den
    fi = (ai * lr - (ar - 1.0) * li) / den
    br, bi = b_re.astype(F32), b_im.astype(F32)
    bbr = fr[..., None] * br - fi[..., None] * bi
    bbi = fr[..., None] * bi + fi[..., None] * br
    cr, ci = c_re.astype(F32), c_im.astype(F32)
    car = cr[None] * pr[:, :, :, None, :] - ci[None] * pi[:, :, :, None, :]
    cai = cr[None] * pi[:, :, :, None, :] + ci[None] * pr[:, :, :, None, :]
    kern = (jnp.einsum('pdgon,dgnc->pdgoc', car, bbr, **hp)
            - jnp.einsum('pdgon,dgnc->pdgoc', cai, bbi, **hp))
    g = lr.shape[1]
    jj = np.arange(q)[:, None]
    ii = np.arange(q)[None, :]
    df = ii - jj
    mf = jnp.where((df >= 0)[:, :, None, None, None], kern[:q, 0][np.clip(df, 0, q - 1)], 0.0)
    mb = jnp.where((df <= 0)[:, :, None, None, None], kern[:q, 1][np.clip(-df, 0, q - 1)], 0.0)
    m = jnp.transpose(mf + mb, (2, 0, 4, 1, 3))
    skip = (jnp.eye(q, dtype=F32)[None, :, None, :, None]
            * jnp.eye(S5_GROUP, dtype=F32)[None, None, :, None, :]
            * d_skip.astype(F32).reshape(g, 1, S5_GROUP, 1, 1))
    m = (m + skip).reshape(g, q * S5_GROUP, q * S5_GROUP)

    def p_block(powers, d):
        ppr, ppi = pr[powers, d], pi[powers, d]
        re = ppr[..., None] * bbr[d][None] - ppi[..., None] * bbi[d][None]
        im = ppr[..., None] * bbi[d][None] + ppi[..., None] * bbr[d][None]
        to = lambda t: jnp.transpose(t, (1, 0, 3, 2)).reshape(g, q * S5_GROUP, S5_STATE)
        return to(re), to(im)

    pf_re, pf_im = p_block(np.arange(q - 1, -1, -1), 0)
    pb_re, pb_im = p_block(np.arange(q), 1)
    p = jnp.concatenate([pf_re, pb_re, pf_im, pb_im], axis=-1)

    def r_block(powers, d):
        to = lambda t: jnp.transpose(t, (1, 3, 0, 2)).reshape(g, S5_STATE, q * S5_GROUP)
        return to(car[powers, d]), to(-cai[powers, d])

    rf_re, rf_im = r_block(np.arange(1, q + 1), 0)
    rb_re, rb_im = r_block(np.arange(q, 0, -1), 1)
    r = jnp.concatenate([rf_re, rb_re, rf_im, rb_im], axis=1)
    a16 = jnp.stack([jnp.concatenate([pr[q, 0], pr[q, 1]], axis=-1),
                     jnp.concatenate([pi[q, 0], pi[q, 1]], axis=-1)], axis=1)
    return m.astype(BF16), p.astype(BF16), r.astype(BF16), a16


def _s5_kernel(u_ref, m_ref, p_ref, r_ref, a_ref, z_ref, v_ref, fre_ref, fim_ref, bre_ref, bim_ref,
               *, n_lat, n_ctx, nb):
    u = u_ref[0]
    v_ref[...] = _dot(u, p_ref[0])
    a = a_ref[0]
    a_re, a_im = a[0:1, :], a[1:2, :]
    fwd_lane = lax.broadcasted_iota(jnp.int32, (nb, 2 * S5_STATE), 1) < S5_STATE

    def scan(base, n, s0, store):
        def body(t, carry):
            s_re, s_im = carry
            rf = pl.multiple_of((base + t) * nb, nb)
            rb = pl.multiple_of((base + n - 1 - t) * nb, nb)
            if store:
                fre_ref[pl.ds(rf, nb), :] = s_re
                fim_ref[pl.ds(rf, nb), :] = s_im
                bre_ref[pl.ds(rb, nb), :] = s_re
                bim_ref[pl.ds(rb, nb), :] = s_im
            vf = v_ref[pl.ds(rf, nb), :]
            vb = v_ref[pl.ds(rb, nb), :]
            v_re = jnp.where(fwd_lane, vf[:, 0:128], vb[:, 0:128])
            v_im = jnp.where(fwd_lane, vf[:, 128:256], vb[:, 128:256])
            n_re = a_re * s_re - a_im * s_im + v_re
            n_im = a_re * s_im + a_im * s_re + v_im
            return n_re, n_im
        return lax.fori_loop(0, n, body, s0)

    zero = jnp.zeros((nb, 2 * S5_STATE), F32)
    s_ctx = scan(n_lat, n_ctx, (zero, zero), False)
    scan(0, n_lat, s_ctx, True)
    rows = n_lat * nb
    lane = lax.broadcasted_iota(jnp.int32, (rows, 2 * S5_STATE), 1) < S5_STATE
    st = jnp.concatenate([jnp.where(lane, fre_ref[...], bre_ref[...]),
                          jnp.where(lane, fim_ref[...], bim_ref[...])], axis=-1).astype(BF16)
    y = _dot(u[0:rows], m_ref[0]) + _dot(st, r_ref[0])
    z_ref[0] = jax.nn.gelu(y).astype(BF16)


def _s5_mixer(u_all, l, mats):
    m, p, r, a16 = mats
    b, lt, width = u_all.shape
    g = width // S5_GROUP
    q = S5_Q
    n_all, n_lat = lt // q, l // q
    qc = q * S5_GROUP
    ug = u_all.reshape(b, n_all, q, g, S5_GROUP)
    ug = jnp.transpose(ug, (3, 1, 0, 2, 4)).reshape(g, n_all * b, qc)
    rows = n_lat * b
    z = pl.pallas_call(
        functools.partial(_s5_kernel, n_lat=n_lat, n_ctx=n_all - n_lat, nb=b),
        grid=(g,),
        in_specs=[pl.BlockSpec((1, n_all * b, qc), lambda i: (i, 0, 0)),
                  pl.BlockSpec((1, qc, qc), lambda i: (i, 0, 0)),
                  pl.BlockSpec((1, qc, qc), lambda i: (i, 0, 0)),
                  pl.BlockSpec((1, qc, qc), lambda i: (i, 0, 0)),
                  pl.BlockSpec((1, 2, 2 * S5_STATE), lambda i: (i, 0, 0))],
        out_specs=pl.BlockSpec((1, rows, qc), lambda i: (i, 0, 0)),
        out_shape=jax.ShapeDtypeStruct((g, rows, qc), BF16),
        scratch_shapes=[pltpu.VMEM((n_all * b, qc), F32)] + [pltpu.VMEM((rows, 2 * S5_STATE), F32)] * 4,
        compiler_params=_cparams(("parallel",)),
        name="s5_scan",
    )(ug, m, p, r, a16)
    z = z.reshape(g, n_lat, b, q, S5_GROUP)
    return jnp.transpose(z, (2, 1, 3, 0, 4)).reshape(b, l, width)


def _inv_unit_triangular(a):
    c = a.shape[0]
    eye = (lax.broadcasted_iota(jnp.int32, (c, c), 0) == lax.broadcasted_iota(jnp.int32, (c, c), 1)).astype(F32)
    p = eye - a
    apow = a
    steps = int(np.log2(c)) - 1
    for _ in range(steps):
        ab = apow.astype(BF16)
        apow = _dot(ab, ab)
        p = p + _dot(p.astype(BF16), apow.astype(BF16))
    return p


def _gdn_prep_kernel(qkv_ref, cw_ref, ba_ref, bat_ref, prow_ref, pcol_ref,
                     u_ref, w_ref, qd_ref, kd_ref, qk_ref, dl_ref, *, n_lat_blocks, ctx_len):
    j = pl.program_id(1)
    rows = qkv_ref.shape[1]
    c = GDN_CHUNK
    seg = jnp.where(j >= n_lat_blocks, ctx_len, GRID_W)
    x = qkv_ref[0].astype(F32)
    pos = lax.broadcasted_iota(jnp.int32, (rows, 1), 0) & (seg - 1)
    acc = x * cw_ref[2:3, :]
    for s in (-2, -1, 1, 2):
        shifted = pltpu.roll(x, (-s) % rows, 0)
        ok = jnp.logical_and(pos + s >= 0, pos + s < seg)
        acc = acc + jnp.where(ok, shifted, 0.0) * cw_ref[2 + s:3 + s, :]
    act = acc * _sigmoid(acc)

    ii = lax.broadcasted_iota(jnp.int32, (c, c), 0)
    jj = lax.broadcasted_iota(jnp.int32, (c, c), 1)
    low_incl = ii >= jj
    tri_low = low_incl.astype(F32)
    tri_up = (ii <= jj).astype(F32)
    ones = jnp.ones((c, c), F32)
    lane = lax.broadcasted_iota(jnp.int32, (c, 128), 1)
    prow = prow_ref[...]
    pcol = pcol_ref[...]
    qscale = GDN_DK ** -0.5

    for ci in range(rows // c):
        r0 = ci * c
        ba = ba_ref[0, r0:r0 + c, :]
        beta_all = _sigmoid(ba)
        g_all = -prow[0:1, :] * _softplus(ba + prow[1:2, :])
        g_all = jnp.where(jnp.logical_and(lane >= 8, lane < 16), g_all, 0.0)
        gc_f = jnp.dot(tri_low, g_all, precision=HIGHEST, preferred_element_type=F32)
        gc_b = jnp.dot(tri_up, g_all, precision=HIGHEST, preferred_element_type=F32)
        gtot = jnp.dot(ones, g_all, precision=HIGHEST, preferred_element_type=F32)
        bat = bat_ref[0, ci]
        g_row = -pcol[:, 0:1] * _softplus(bat[8:16, :] + pcol[:, 1:2])
        gr_f = jnp.dot(g_row, tri_up, precision=HIGHEST, preferred_element_type=F32)
        gr_b = jnp.dot(g_row, tri_low, precision=HIGHEST, preferred_element_type=F32)
        for h in range(GDN_HEADS):
            qh = act[r0:r0 + c, h * 128:(h + 1) * 128]
            kh = act[r0:r0 + c, 512 + h * 128:512 + (h + 1) * 128]
            vh = act[r0:r0 + c, 1024 + h * 128:1024 + (h + 1) * 128]
            qh = qh * lax.rsqrt(jnp.sum(qh * qh, axis=-1, keepdims=True) + EPS) * qscale
            kh = kh * lax.rsqrt(jnp.sum(kh * kh, axis=-1, keepdims=True) + EPS)
            kb = kh.astype(BF16)
            kk = _dot_nt(kb, kb)
            qkm = _dot_nt(qh.astype(BF16), kb)
            for d in range(2):
                col = d * GDN_HEADS + h
                beta = beta_all[:, col:col + 1]
                gc = (gc_f if d == 0 else gc_b)[:, 8 + col:9 + col]
                gt = gtot[:, 8 + col:9 + col]
                grow = (gr_f if d == 0 else gr_b)[col:col + 1, :]
                incl = low_incl if d == 0 else (ii <= jj)
                strict = (ii > jj) if d == 0 else (ii < jj)
                decay = jnp.exp(jnp.where(incl, gc - grow, -jnp.inf))
                a_mat = jnp.where(strict, beta * kk * decay, 0.0)
                tinv = _inv_unit_triangular(a_mat).astype(BF16)
                eg = jnp.exp(gc)
                rhs = jnp.concatenate([vh * beta, kh * (beta * eg)], axis=-1).astype(BF16)
                sol = _dot(tinv, rhs)
                u_ref[0, d, h, ci] = sol[:, 0:128]
                w_ref[0, d, h, ci] = sol[:, 128:256].astype(BF16)
                qd_ref[0, d, h, ci] = (qh * eg).astype(BF16)
                kd_ref[0, d, h, ci] = (kh * jnp.exp(gt - gc)).astype(BF16)
                qk_ref[0, d, h, ci] = (qkm * decay).astype(BF16)
                dl_ref[0, d, h, ci] = jnp.broadcast_to(jnp.exp(gt[0:8, :]), (8, 128))


def _gdn_prepare(qkv, conv_w, ba, bat_chunks, prow, pcol, l):
    b, lt, width = qkv.shape
    tm = ROW_TILE
    c = GDN_CHUNK
    cpb = tm // c
    nblk = lt // tm
    nch = lt // c
    hd = (b, 2, GDN_HEADS, nch)

    def blk(shape_tail, dtype):
        return (pl.BlockSpec((1, 2, GDN_HEADS, cpb) + shape_tail, lambda i, j: (i, 0, 0, j, 0, 0)),
                jax.ShapeDtypeStruct(hd + shape_tail, dtype))

    specs = [blk((c, 128), F32), blk((c, 128), BF16), blk((c, 128), BF16), blk((c, 128), BF16),
             blk((c, c), BF16), blk((8, 128), F32)]
    return pl.pallas_call(
        functools.partial(_gdn_prep_kernel, n_lat_blocks=l // tm, ctx_len=lt - l),
        grid=(b, nblk),
        in_specs=[pl.BlockSpec((1, tm, width), lambda i, j: (i, j, 0)),
                  pl.BlockSpec((8, width), lambda i, j: (0, 0)),
                  pl.BlockSpec((1, tm, 128), lambda i, j: (i, j, 0)),
                  pl.BlockSpec((1, cpb, 16, c), lambda i, j: (i, j, 0, 0)),
                  pl.BlockSpec((2, 128), lambda i, j: (0, 0)),
                  pl.BlockSpec((8, 2), lambda i, j: (0, 0))],
        out_specs=[s for s, _ in specs],
        out_shape=[o for _, o in specs],
        compiler_params=_cparams(("parallel", "parallel")),
        name="gdn_prepare",
    )(qkv, conv_w, ba, bat_chunks, prow, pcol)


def _gdn_scan_kernel(uf, wf, qdf, kdf, qkf, dlf, ub, wb, qdb, kdb, qkb, dlb, of_ref, ob_ref, s_ref):
    t = pl.program_id(1)

    @pl.when(t == 0)
    def _():
        s_ref[...] = jnp.zeros_like(s_ref)

    for d, (u_r, w_r, qd_r, kd_r, qk_r, dl_r, o_r) in enumerate(
            ((uf, wf, qdf, kdf, qkf, dlf, of_ref), (ub, wb, qdb, kdb, qkb, dlb, ob_ref))):
        for h in range(GDN_HEADS):
            s = s_ref[d, h]
            sb = s.astype(BF16)
            v_new = u_r[0, 0, h, 0] - _dot(w_r[0, 0, h, 0], sb)
            vb = v_new.astype(BF16)
            o = _dot(qd_r[0, 0, h, 0], sb) + _dot(qk_r[0, 0, h, 0], vb)
            s_ref[d, h] = s * dl_r[0, 0, h, 0][0:1, :] + _dot_tn(kd_r[0, 0, h, 0], vb)
            o_r[0, :, h * 128:(h + 1) * 128] = o


def _gdn_scan(prep, l):
    u, w, qd, kd, qk, dl = prep
    b, _, heads, nch, c, _ = u.shape
    n_lat = l // c
    n_ctx = nch - n_lat
    lt = nch * c

    def fwd_chunk(t):
        return jnp.where(t < n_ctx, n_lat + t, t - n_ctx)

    def bwd_chunk(t):
        return nch - 1 - t

    def spec(arr, chunk_of, d):
        tail = arr.shape[4:]
        return pl.BlockSpec((1, 1, heads, 1) + tail, lambda i, t: (i, d, 0, chunk_of(t), 0, 0))

    ins = [spec(a, fwd_chunk, 0) for a in prep] + [spec(a, bwd_chunk, 1) for a in prep]
    width = heads * 128
    return pl.pallas_call(
        _gdn_scan_kernel,
        grid=(b, nch),
        in_specs=ins,
        out_specs=[pl.BlockSpec((1, c, width), lambda i, t: (i, fwd_chunk(t), 0)),
                   pl.BlockSpec((1, c, width), lambda i, t: (i, bwd_chunk(t), 0))],
        out_shape=[jax.ShapeDtypeStruct((b, lt, width), F32)] * 2,
        scratch_shapes=[pltpu.VMEM((2, heads, GDN_DK, 128), F32)],
        compiler_params=_cparams(("parallel", "arbitrary")),
        name="gdn_scan",
    )(*prep, *prep)


def _merge_kernel(x_ref, z_ref, of_ref, ob_ref, gate_ref, br_ref, gt1_ref, sc2_ref, sh2_ref,
                  wglu_ref, bglu_ref, gnorm_ref, wa_ref, wb_ref, wo_ref, n2_ref, wr_ref, brt_ref,
                  x1_ref, h2_ref, lg_ref):
    z = z_ref[0]
    zf = z.astype(F32)
    ya = zf * _sigmoid(_dot(z, wglu_ref[...]) + bglu_ref[...])
    o = of_ref[0] + ob_ref[0]
    gate = gate_ref[0].astype(F32)
    parts = []
    for h in range(GDN_HEADS):
        oh = o[:, h * 128:(h + 1) * 128]
        parts.append(oh * lax.rsqrt(jnp.mean(oh * oh, axis=-1, keepdims=True) + EPS) * gnorm_ref[...])
    yb = jnp.concatenate(parts, axis=-1) * (gate * _sigmoid(gate))
    br = br_ref[0].astype(F32)
    d = x_ref.shape[2]
    ga = _sigmoid(br[:, 0:d])
    gb = _sigmoid(br[:, d:2 * d])
    m = ga * _dot(ya.astype(BF16), wa_ref[...]) + gb * _dot(yb.astype(BF16), wb_ref[...])
    mix = _dot(m.astype(BF16), wo_ref[...])
    x1 = x_ref[0] + gt1_ref[0] * mix
    x1_ref[0] = x1
    y2 = x1 * lax.rsqrt(jnp.mean(x1 * x1, axis=-1, keepdims=True) + EPS) * n2_ref[...]
    h2 = y2 * (1.0 + sc2_ref[0]) + sh2_ref[0]
    h2_ref[0] = h2.astype(BF16)
    lg_ref[0] = jnp.dot(h2, wr_ref[...], precision=HIGHEST, preferred_element_type=F32) + brt_ref[...]


def _merge(x, z, o_f, o_b, gate, br, mod3, wglu, bglu, gnorm, wa, wb, wo, norm2, wr, brt):
    b, l, d = x.shape
    tm = ROW_TILE
    tok = lambda width: pl.BlockSpec((1, tm, width), lambda i, j: (i, j, 0))
    modspec = lambda k: pl.BlockSpec((1, 1, d), lambda i, j: (i * 6 + k, 0, 0))
    full = lambda arr: pl.BlockSpec(arr.shape, lambda i, j: (0,) * arr.ndim)
    consts = [wglu, bglu, gnorm, wa, wb, wo, norm2, wr, brt]
    return pl.pallas_call(
        _merge_kernel,
        grid=(b, l // tm),
        in_specs=[tok(d), tok(512), tok(512), tok(512), tok(512), tok(2 * d),
                  modspec(2), modspec(4), modspec(3)] + [full(a) for a in consts],
        out_specs=[tok(d), tok(d), tok(128)],
        out_shape=[jax.ShapeDtypeStruct((b, l, d), F32),
                   jax.ShapeDtypeStruct((b, l, d), BF16),
                   jax.ShapeDtypeStruct((b, l, 128), F32)],
        compiler_params=_cparams(("parallel", "parallel")),
        name="branch_merge",
    )(x, z, o_f, o_b, gate, br, mod3, mod3, mod3, *consts)


def _moe_kernel(te_ref, nu_ref, x_ref, wgu_ref, bgu_ref, wd_ref, bd_ref, y_ref, wgu_bf, wd_bf):
    i = pl.program_id(0)
    changed = jnp.logical_or(i == 0, te_ref[i] != te_ref[jnp.maximum(i - 1, 0)])

    @pl.when(jnp.logical_and(changed, i < nu_ref[0]))
    def _():
        wgu_bf[...] = wgu_ref[0].astype(BF16)
        wd_bf[...] = wd_ref[0].astype(BF16)

    @pl.when(i < nu_ref[0])
    def _():
        de = wd_ref.shape[1]
        gu = _dot(x_ref[...], wgu_bf[...]) + bgu_ref[0]
        gate = jnp.minimum(gu[:, 0:de], SWIGLU_LIMIT)
        up = jnp.clip(gu[:, de:2 * de], -SWIGLU_LIMIT, SWIGLU_LIMIT)
        act = (up + 1.0) * gate * _sigmoid(gate * SWIGLU_ALPHA)
        y_ref[...] = (_dot(act.astype(BF16), wd_bf[...]) + bd_ref[0]).astype(y_ref.dtype)

    @pl.when(i >= nu_ref[0])
    def _():
        y_ref[...] = jnp.zeros_like(y_ref)


def _moe_experts(xs, tile_expert, n_used, w_gate_up, b_gate_up, w_down, b_down):
    p, d = xs.shape
    tm = MOE_TILE
    e, _, n2 = w_gate_up.shape
    de = w_down.shape[1]
    grid_spec = pltpu.PrefetchScalarGridSpec(
        num_scalar_prefetch=2,
        grid=(p // tm,),
        in_specs=[pl.BlockSpec((tm, d), lambda i, te, nu: (jnp.minimum(i, nu[0] - 1), 0)),
                  pl.BlockSpec((1, d, n2), lambda i, te, nu: (te[i], 0, 0)),
                  pl.BlockSpec((1, 1, n2), lambda i, te, nu: (te[i], 0, 0)),
                  pl.BlockSpec((1, de, d), lambda i, te, nu: (te[i], 0, 0)),
                  pl.BlockSpec((1, 1, d), lambda i, te, nu: (te[i], 0, 0))],
        out_specs=pl.BlockSpec((tm, d), lambda i, te, nu: (i, 0)),
        scratch_shapes=[pltpu.VMEM((d, n2), BF16), pltpu.VMEM((de, d), BF16)],
    )
    return pl.pallas_call(
        _moe_kernel,
        grid_spec=grid_spec,
        out_shape=jax.ShapeDtypeStruct((p, d), BF16),
        compiler_params=_cparams(("arbitrary",)),
        name="moe_experts",
    )(tile_expert, n_used, xs, w_gate_up, b_gate_up.reshape(e, 1, n2), w_down, b_down.reshape(e, 1, d))


def _route(logits, n_tokens):
    tm = MOE_TILE
    top_val, top_idx = lax.top_k(logits, TOP_K)
    weights = jax.nn.softmax(top_val, axis=-1)
    flat_e = top_idx.reshape(-1).astype(jnp.int32)
    n_assign = flat_e.shape[0]
    order = jnp.argsort(flat_e, stable=True).astype(jnp.int32)
    sorted_e = flat_e[order]
    counts = jnp.sum(jax.nn.one_hot(flat_e, N_EXPERTS, dtype=jnp.int32), axis=0)
    padded = ((counts + tm - 1) // tm) * tm
    pad_end = jnp.cumsum(padded)
    pad_start = pad_end - padded
    raw_start = jnp.cumsum(counts) - counts
    dest = pad_start[sorted_e] + (jnp.arange(n_assign, dtype=jnp.int32) - raw_start[sorted_e])
    n_rows = n_assign + N_EXPERTS * tm
    src_token = jnp.zeros((n_rows,), jnp.int32).at[dest].set(order // TOP_K)
    pos = jnp.zeros((n_assign,), jnp.int32).at[order].set(dest).reshape(n_tokens, TOP_K)
    n_tiles = n_rows // tm
    n_used = (pad_end[-1] // tm).astype(jnp.int32)
    tile_start = jnp.arange(n_tiles, dtype=jnp.int32) * tm
    tile_expert = jnp.sum((tile_start[:, None] >= pad_end[None, :]).astype(jnp.int32), axis=1)
    last_e = jnp.sum((pad_end[-1] - 1 >= pad_end).astype(jnp.int32))
    tile_expert = jnp.minimum(tile_expert, last_e).astype(jnp.int32)
    return weights, src_token, pos, tile_expert, n_used.reshape(1)


def _final_kernel(x1_ref, y0_ref, y1_ref, y2_ref, y3_ref, wt_ref, gt2_ref, nf_ref, o_ref):
    wt = wt_ref[0]
    moe = (wt[:, 0:1] * y0_ref[0].astype(F32) + wt[:, 1:2] * y1_ref[0].astype(F32)
           + wt[:, 2:3] * y2_ref[0].astype(F32) + wt[:, 3:4] * y3_ref[0].astype(F32))
    x2 = x1_ref[0] + gt2_ref[0] * moe
    o_ref[0] = x2 * lax.rsqrt(jnp.mean(x2 * x2, axis=-1, keepdims=True) + EPS) * nf_ref[...]


def _final(x1, ys, wt, mod3, norm_f):
    b, l, d = x1.shape
    tm = ROW_TILE
    tok = lambda width: pl.BlockSpec((1, tm, width), lambda i, j: (i, j, 0))
    return pl.pallas_call(
        _final_kernel,
        grid=(b, l // tm),
        in_specs=[tok(d)] * 5 + [tok(128), pl.BlockSpec((1, 1, d), lambda i, j: (i * 6 + 5, 0, 0)),
                                 pl.BlockSpec((1, d), lambda i, j: (0, 0))],
        out_specs=tok(d),
        out_shape=jax.ShapeDtypeStruct((b, l, d), F32),
        compiler_params=_cparams(("parallel", "parallel")),
        name="combine_final_norm",
    )(x1, *ys, wt, mod3, norm_f.reshape(1, d))


def _split_hi_lo(w):
    hi = w.astype(BF16)
    lo = (w - hi.astype(F32)).astype(BF16)
    return hi, lo


def kernel(x, c, ctx, c_ctx, w_mod, b_mod, norm1, w_in, s5_lam_re, s5_lam_im, s5_log_step, s5_b_re, s5_b_im, s5_c_re, s5_c_im, s5_d, s5_w_glu, s5_b_glu, gdn_conv, gdn_a_log, gdn_dt_bias, gdn_norm, w_branch_a, w_branch_b, w_out, norm2, w_router, b_router, w_gate_up, b_gate_up, w_down, b_down, norm_f):
    b, l, d = x.shape
    lc = ctx.shape[1]
    depth = w_mod.shape[0]
    assert depth == 1, "single-layer block: the context stream has no consumer after the token mixer"
    assert l % ROW_TILE == 0 and lc % ROW_TILE == 0 and lc & (lc - 1) == 0 and b <= 8
    ly = 0

    cc = jnp.zeros((16, d), F32).at[0:b].set(c).at[b].set(c_ctx)
    mod = _modulation(cc, w_mod[ly], b_mod[ly])
    mod3 = mod[0:b + 1].reshape((b + 1) * 6, 1, d)

    wi = w_in[ly]
    o_u, o_qkv, o_gate, o_ba, o_br = 0, 512, 2048, 2560, 2576
    w_ba = wi[:, o_ba:o_br]
    ba_hi, ba_lo = _split_hi_lo(w_ba)
    pad = lambda t: jnp.pad(t, ((0, 0), (0, 128 - t.shape[1])))
    w_main = jnp.concatenate([wi[:, o_u:o_ba].astype(BF16), wi[:, o_br:].astype(BF16), pad(ba_hi), pad(ba_lo)], axis=1)
    wt = jnp.concatenate([ba_hi.T, ba_lo.T], axis=0)
    u_all, qkv, gate, br, ba, bat = _in_projection(x, ctx, mod3, norm1[ly], w_main, pad(ba_hi), wt, ba_hi.T)

    mats = _s5_matrices(s5_lam_re[ly], s5_lam_im[ly], s5_log_step[ly], s5_b_re[ly], s5_b_im[ly],
                        s5_c_re[ly], s5_c_im[ly], s5_d[ly])
    z = _s5_mixer(u_all, l, mats)

    lt = l + lc
    nch = lt // GDN_CHUNK
    bat_chunks = jnp.transpose(bat.reshape(b, 16, nch, GDN_CHUNK), (0, 2, 1, 3))
    ea = jnp.exp(gdn_a_log[ly].astype(F32)).reshape(-1)
    dtb = gdn_dt_bias[ly].astype(F32).reshape(-1)
    prow = jnp.zeros((2, 128), F32).at[0, 8:16].set(ea).at[1, 8:16].set(dtb)
    pcol = jnp.stack([ea, dtb], axis=1)
    conv_w = jnp.zeros((8, qkv.shape[2]), F32).at[0:CONV_K].set(gdn_conv[ly].astype(F32))
    prep = _gdn_prepare(qkv, conv_w, ba, bat_chunks, prow, pcol, l)
    o_f, o_b = _gdn_scan(prep, l)

    wr = jnp.pad(w_router[ly].astype(F32), ((0, 0), (0, 128 - N_EXPERTS)))
    brt = jnp.pad(b_router[ly].astype(F32), (0, 128 - N_EXPERTS)).reshape(1, 128)
    x1, h2, logits = _merge(x, z, o_f, o_b, gate, br, mod3,
                            s5_w_glu[ly].astype(BF16), s5_b_glu[ly].astype(F32).reshape(1, -1),
                            gdn_norm[ly].astype(F32).reshape(1, -1),
                            w_branch_a[ly].astype(BF16), w_branch_b[ly].astype(BF16), w_out[ly].astype(BF16),
                            norm2[ly].astype(F32).reshape(1, d), wr, brt)

    n_tok = b * l
    weights, src_token, pos, tile_expert, n_used = _route(logits.reshape(n_tok, 128)[:, 0:N_EXPERTS], n_tok)
    xs = h2.reshape(n_tok, d)[src_token]
    ys = _moe_experts(xs, tile_expert, n_used, w_gate_up[ly], b_gate_up[ly], w_down[ly], b_down[ly])
    yk = [ys[pos[:, k]].reshape(b, l, d) for k in range(TOP_K)]
    wt4 = jnp.pad(weights, ((0, 0), (0, 128 - TOP_K))).reshape(b, l, 128)
    return _final(x1, yk, wt4, mod3, norm_f)
```

```python
import functools

import numpy as np
import jax
import jax.numpy as jnp
from jax import lax
from jax.experimental import pallas as pl
from jax.experimental.pallas import tpu as pltpu

F32 = jnp.float32
BF16 = jnp.bfloat16
HIGHEST = lax.Precision.HIGHEST

EPS = 1e-6
GRID_W = 64

S5_GROUP = 16
S5_STATE = 64
S5_Q = 16

GDN_HEADS = 4
GDN_DK = 128
GDN_CHUNK = 64
CONV_K = 5

N_EXPERTS = 32
TOP_K = 4
SWIGLU_LIMIT = 7.0
SWIGLU_ALPHA = 1.702

ROW_TILE = 256
MOE_TILE = 512
VMEM_LIMIT = 56 * 1024 * 1024


def _cparams(sem):
    return pltpu.CompilerParams(dimension_semantics=sem, vmem_limit_bytes=VMEM_LIMIT)


def _sigmoid(v):
    return 1.0 / (1.0 + jnp.exp(-v))


def _softplus(v):
    return jnp.maximum(v, 0.0) + jnp.log(1.0 + jnp.exp(-jnp.abs(v)))


def _dot(a, b):
    return jnp.dot(a, b, preferred_element_type=F32)


def _dot_nt(a, b):
    return lax.dot_general(a, b, (((1,), (1,)), ((), ())), preferred_element_type=F32)


def _dot_tn(a, b):
    return lax.dot_general(a, b, (((0,), (0,)), ((), ())), preferred_element_type=F32)


def _mod_kernel(c_ref, w_ref, b_ref, o_ref):
    cc = c_ref[...]
    s = cc * _sigmoid(cc)
    o_ref[...] = jnp.dot(s, w_ref[...], precision=HIGHEST, preferred_element_type=F32) + b_ref[...]


def _modulation(cc, w_mod, b_mod):
    rows, d = cc.shape
    n = w_mod.shape[1]
    bn = d
    return pl.pallas_call(
        _mod_kernel,
        grid=(n // bn,),
        in_specs=[pl.BlockSpec((rows, d), lambda j: (0, 0)),
                  pl.BlockSpec((d, bn), lambda j: (0, j)),
                  pl.BlockSpec((1, bn), lambda j: (0, j))],
        out_specs=pl.BlockSpec((rows, bn), lambda j: (0, j)),
        out_shape=jax.ShapeDtypeStruct((rows, n), F32),
        compiler_params=_cparams(("arbitrary",)),
        name="adaln_mod",
    )(cc, w_mod, b_mod.reshape(1, n))


def _inproj_kernel(x_ref, c_ref, sc_ref, sh_ref, g_ref, w_ref, wlo_ref, wt_ref, wtlo_ref,
                   u_ref, qkv_ref, gate_ref, br_ref, ba_ref, bat_ref, *, n_lat_blocks):
    j = pl.program_id(1)
    is_ctx = j >= n_lat_blocks
    x = jnp.where(is_ctx, c_ref[0], x_ref[0])
    ms = jnp.mean(x * x, axis=-1, keepdims=True)
    y = x * lax.rsqrt(ms + EPS) * g_ref[...]
    h = y * (1.0 + sc_ref[0]) + sh_ref[0]
    h_hi = h.astype(BF16)
    h_lo = (h - h_hi.astype(F32)).astype(BF16)
    z = _dot(h_hi, w_ref[...])
    u_ref[0] = z[:, 0:512].astype(BF16)
    qkv_ref[0] = z[:, 512:2048].astype(BF16)
    gate_ref[0] = z[:, 2048:2560].astype(BF16)
    br_ref[0] = z[:, 2560:4608].astype(BF16)
    ba_ref[0] = z[:, 4608:4736] + z[:, 4736:4864] + _dot(h_lo, wlo_ref[...])
    rt = _dot_nt(wt_ref[...], h_hi)
    bat_ref[0] = rt[0:16] + rt[16:32] + _dot_nt(wtlo_ref[...], h_lo)


def _in_projection(x, ctx, mod3, norm1, w_main, w_lo, wt, wt_lo):
    b, l, d = x.shape
    lc = ctx.shape[1]
    tm = ROW_TILE
    nl, nc = l // tm, lc // tm
    nb = mod3.shape[0] // 6 - 1
    lt = l + lc
    ncols = w_main.shape[1]

    def x_map(i, j):
        return (i, jnp.minimum(j, nl - 1), 0)

    def c_map(i, j):
        return (i, jnp.maximum(j - nl, 0), 0)

    def mod_map(k):
        return lambda i, j: (jnp.where(j >= nl, nb, i) * 6 + k, 0, 0)

    def tok(width):
        return pl.BlockSpec((1, tm, width), lambda i, j: (i, j, 0))

    const2 = lambda i, j: (0, 0)
    outs = pl.pallas_call(
        functools.partial(_inproj_kernel, n_lat_blocks=nl),
        grid=(b, nl + nc),
        in_specs=[pl.BlockSpec((1, tm, d), x_map),
                  pl.BlockSpec((1, tm, d), c_map),
                  pl.BlockSpec((1, 1, d), mod_map(1)),
                  pl.BlockSpec((1, 1, d), mod_map(0)),
                  pl.BlockSpec((1, d), const2),
                  pl.BlockSpec((d, ncols), const2),
                  pl.BlockSpec((d, 128), const2),
                  pl.BlockSpec((32, d), const2),
                  pl.BlockSpec((16, d), const2)],
        out_specs=[tok(512), tok(1536), tok(512), tok(2048), tok(128),
                   pl.BlockSpec((1, 16, tm), lambda i, j: (i, 0, j))],
        out_shape=[jax.ShapeDtypeStruct((b, lt, 512), BF16),
                   jax.ShapeDtypeStruct((b, lt, 1536), BF16),
                   jax.ShapeDtypeStruct((b, lt, 512), BF16),
                   jax.ShapeDtypeStruct((b, lt, 2048), BF16),
                   jax.ShapeDtypeStruct((b, lt, 128), F32),
                   jax.ShapeDtypeStruct((b, 16, lt), F32)],
        compiler_params=_cparams(("parallel", "arbitrary")),
        name="in_projection",
    )(x, ctx, mod3, mod3, norm1.reshape(1, d), w_main, w_lo, wt, wt_lo)
    return outs


def _s5_matrices(lam_re, lam_im, log_step, b_re, b_im, c_re, c_im, d_skip):
    q = S5_Q
    hp = dict(precision=HIGHEST)
    lr = jnp.minimum(lam_re.astype(F32), -1e-4)
    li = lam_im.astype(F32)
    dt = jnp.exp(log_step.astype(F32))[..., None]
    pw = jnp.arange(q + 1, dtype=F32)[:, None, None, None]
    mag = jnp.exp(pw * (lr * dt))
    ang = pw * (li * dt)
    pr, pi = mag * jnp.cos(ang), mag * jnp.sin(ang)
    ar, ai = pr[1], pi[1]
    den = lr * lr + li * li
    fr = ((ar - 1.0) * lr + ai * li) / den
    fi = (ai * lr - (ar - 1.0) * li) / den
    br, bi = b_re.astype(F32), b_im.astype(F32)
    bbr = fr[..., None] * br - fi[..., None] * bi
    bbi = fr[..., None] * bi + fi[..., None] * br
    cr, ci = c_re.astype(F32), c_im.astype(F32)
    car = cr[None] * pr[:, :, :, None, :] - ci[None] * pi[:, :, :, None, :]
    cai = cr[None] * pi[:, :, :, None, :] + ci[None] * pr[:, :, :, None, :]
    kern = (jnp.einsum('pdgon,dgnc->pdgoc', car, bbr, **hp)
            - jnp.einsum('pdgon,dgnc->pdgoc', cai, bbi, **hp))
    g = lr.shape[1]
    jj = np.arange(q)[:, None]
    ii = np.arange(q)[None, :]
    df = ii - jj
    mf = jnp.where((df >= 0)[:, :, None, None, None], kern[:q, 0][np.clip(df, 0, q - 1)], 0.0)
    mb = jnp.where((df <= 0)[:, :, None, None, None], kern[:q, 1][np.clip(-df, 0, q - 1)], 0.0)
    m = jnp.transpose(mf + mb, (2, 0, 4, 1, 3))
    skip = (jnp.eye(q, dtype=F32)[None, :, None, :, None]
            * jnp.eye(S5_GROUP, dtype=F32)[None, None, :, None, :]
            * d_skip.astype(F32).reshape(g, 1, S5_GROUP, 1, 1))
    m = (m + skip).reshape(g, q * S5_GROUP, q * S5_GROUP)

    def p_block(powers, d):
        ppr, ppi = pr[powers, d], pi[powers, d]
        re = ppr[..., None] * bbr[d][None] - ppi[..., None] * bbi[d][None]
        im = ppr[..., None] * bbi[d][None] + ppi[..., None] * bbr[d][None]
        to = lambda t: jnp.transpose(t, (1, 0, 3, 2)).reshape(g, q * S5_GROUP, S5_STATE)
        return to(re), to(im)

    pf_re, pf_im = p_block(np.arange(q - 1, -1, -1), 0)
    pb_re, pb_im = p_block(np.arange(q), 1)
    p = jnp.concatenate([pf_re, pb_re, pf_im, pb_im], axis=-1)

    def r_block(powers, d):
        to = lambda t: jnp.transpose(t, (1, 3, 0, 2)).reshape(g, S5_STATE, q * S5_GROUP)
        return to(car[powers, d]), to(-cai[powers, d])

    rf_re, rf_im = r_block(np.arange(1, q + 1), 0)
    rb_re, rb_im = r_block(np.arange(q, 0, -1), 1)
    r = jnp.concatenate([rf_re, rb_re, rf_im, rb_im], axis=1)
    a16 = jnp.stack([jnp.concatenate([pr[q, 0], pr[q, 1]], axis=-1),
                     jnp.concatenate([pi[q, 0], pi[q, 1]], axis=-1)], axis=1)
    return m.astype(BF16), p.astype(BF16), r.astype(BF16), a16


def _s5_kernel(u_ref, m_ref, p_ref, r_ref, a_ref, z_ref, v_ref, fre_ref, fim_ref, bre_ref, bim_ref,
               *, n_lat, n_ctx, nb):
    u = u_ref[0]
    v_ref[...] = _dot(u, p_ref[0])
    a = a_ref[0]
    a_re, a_im = a[0:1, :], a[1:2, :]
    fwd_lane = lax.broadcasted_iota(jnp.int32, (nb, 2 * S5_STATE), 1) < S5_STATE

    def scan(base, n, s0, store):
        def body(t, carry):
            s_re, s_im = carry
            rf = pl.multiple_of((base + t) * nb, nb)
            rb = pl.multiple_of((base + n - 1 - t) * nb, nb)
            if store:
                fre_ref[pl.ds(rf, nb), :] = s_re
                fim_ref[pl.ds(rf, nb), :] = s_im
                bre_ref[pl.ds(rb, nb), :] = s_re
                bim_ref[pl.ds(rb, nb), :] = s_im
            vf = v_ref[pl.ds(rf, nb), :]
            vb = v_ref[pl.ds(rb, nb), :]
            v_re = jnp.where(fwd_lane, vf[:, 0:128], vb[:, 0:128])
            v_im = jnp.where(fwd_lane, vf[:, 128:256], vb[:, 128:256])
            n_re = a_re * s_re - a_im * s_im + v_re
            n_im = a_re * s_im + a_im * s_re + v_im
            return n_re, n_im
        return lax.fori_loop(0, n, body, s0)

    zero = jnp.zeros((nb, 2 * S5_STATE), F32)
    s_ctx = scan(n_lat, n_ctx, (zero, zero), False)
    scan(0, n_lat, s_ctx, True)
    rows = n_lat * nb
    lane = lax.broadcasted_iota(jnp.int32, (rows, 2 * S5_STATE), 1) < S5_STATE
    st = jnp.concatenate([jnp.where(lane, fre_ref[...], bre_ref[...]),
                          jnp.where(lane, fim_ref[...], bim_ref[...])], axis=-1).astype(BF16)
    y = _dot(u[0:rows], m_ref[0]) + _dot(st, r_ref[0])
    z_ref[0] = jax.nn.gelu(y).astype(BF16)


def _s5_mixer(u_all, l, mats):
    m, p, r, a16 = mats
    b, lt, width = u_all.shape
    g = width // S5_GROUP
    q = S5_Q
    n_all, n_lat = lt // q, l // q
    qc = q * S5_GROUP
    ug = u_all.reshape(b, n_all, q, g, S5_GROUP)
    ug = jnp.transpose(ug, (3, 1, 0, 2, 4)).reshape(g, n_all * b, qc)
    rows = n_lat * b
    z = pl.pallas_call(
        functools.partial(_s5_kernel, n_lat=n_lat, n_ctx=n_all - n_lat, nb=b),
        grid=(g,),
        in_specs=[pl.BlockSpec((1, n_all * b, qc), lambda i: (i, 0, 0)),
                  pl.BlockSpec((1, qc, qc), lambda i: (i, 0, 0)),
                  pl.BlockSpec((1, qc, qc), lambda i: (i, 0, 0)),
                  pl.BlockSpec((1, qc, qc), lambda i: (i, 0, 0)),
                  pl.BlockSpec((1, 2, 2 * S5_STATE), lambda i: (i, 0, 0))],
        out_specs=pl.BlockSpec((1, rows, qc), lambda i: (i, 0, 0)),
        out_shape=jax.ShapeDtypeStruct((g, rows, qc), BF16),
        scratch_shapes=[pltpu.VMEM((n_all * b, qc), F32)] + [pltpu.VMEM((rows, 2 * S5_STATE), F32)] * 4,
        compiler_params=_cparams(("parallel",)),
        name="s5_scan",
    )(ug, m, p, r, a16)
    z = z.reshape(g, n_lat, b, q, S5_GROUP)
    return jnp.transpose(z, (2, 1, 3, 0, 4)).reshape(b, l, width)


def _inv_unit_triangular_many(mats):
    c = mats[0].shape[0]
    eye = (lax.broadcasted_iota(jnp.int32, (c, c), 0) == lax.broadcasted_iota(jnp.int32, (c, c), 1)).astype(F32)
    prods = [eye - a for a in mats]
    pows = [a.astype(BF16) for a in mats]
    for _ in range(int(np.log2(c)) - 1):
        pows = [_dot(a, a).astype(BF16) for a in pows]
        prods = [p + _dot(p.astype(BF16), a) for p, a in zip(prods, pows)]
    return prods


def _gdn_prep_kernel(qkv_ref, cw_ref, ba_ref, bat_ref, prow_ref, pcol_ref,
                     u_ref, wq_ref, qk_ref, kdt_ref, dl_ref, *, n_lat_blocks, ctx_len):
    j = pl.program_id(1)
    rows = qkv_ref.shape[1]
    c = GDN_CHUNK
    nchunk = rows // c
    seg = jnp.where(j >= n_lat_blocks, ctx_len, GRID_W)
    x = qkv_ref[0].astype(F32)
    pos = lax.broadcasted_iota(jnp.int32, (rows, 1), 0) & (seg - 1)
    acc = x * cw_ref[2:3, :]
    for s in (-2, -1, 1, 2):
        shifted = pltpu.roll(x, (-s) % rows, 0)
        ok = jnp.logical_and(pos + s >= 0, pos + s < seg)
        acc = acc + jnp.where(ok, shifted, 0.0) * cw_ref[2 + s:3 + s, :]
    act = acc * _sigmoid(acc)

    ii = lax.broadcasted_iota(jnp.int32, (c, c), 0)
    jj = lax.broadcasted_iota(jnp.int32, (c, c), 1)
    incl = (ii >= jj, ii <= jj)
    strict = (ii > jj, ii < jj)
    tri_low = incl[0].astype(F32)
    tri_up = incl[1].astype(F32)
    eye_bf = (ii == jj).astype(BF16)
    ones = jnp.ones((c, c), F32)
    lane = lax.broadcasted_iota(jnp.int32, (c, 128), 1)
    prow = prow_ref[...]
    pcol = pcol_ref[...]
    hp = dict(precision=HIGHEST, preferred_element_type=F32)

    gates = []
    for ci in range(nchunk):
        ba = ba_ref[0, ci * c:(ci + 1) * c, :]
        g_all = -prow[0:1, :] * _softplus(ba + prow[1:2, :])
        g_all = jnp.where(jnp.logical_and(lane >= 8, lane < 16), g_all, 0.0)
        bat = bat_ref[0, ci]
        g_row = -pcol[:, 0:1] * _softplus(bat[8:16, :] + pcol[:, 1:2])
        gates.append(dict(beta=_sigmoid(ba),
                          gc=(jnp.dot(tri_low, g_all, **hp), jnp.dot(tri_up, g_all, **hp)),
                          gtot=jnp.dot(ones, g_all, **hp),
                          gr=(jnp.dot(g_row, tri_up, **hp), jnp.dot(g_row, tri_low, **hp))))

    qn, kn, vv = [], [], []
    for h in range(GDN_HEADS):
        qh = act[:, h * 128:(h + 1) * 128]
        kh = act[:, 512 + h * 128:512 + (h + 1) * 128]
        qn.append(qh * lax.rsqrt(jnp.sum(qh * qh, axis=-1, keepdims=True) + EPS) * (GDN_DK ** -0.5))
        kn.append(kh * lax.rsqrt(jnp.sum(kh * kh, axis=-1, keepdims=True) + EPS))
        vv.append(act[:, 1024 + h * 128:1024 + (h + 1) * 128])

    pairs = [(ci, h) for ci in range(nchunk) for h in range(GDN_HEADS)]
    sl = lambda t, ci: t[ci * c:(ci + 1) * c]
    kb = {p: sl(kn[p[1]], p[0]).astype(BF16) for p in pairs}
    kk = {p: _dot_nt(kb[p], kb[p]) for p in pairs}
    qkm = {p: _dot_nt(sl(qn[p[1]], p[0]).astype(BF16), kb[p]) for p in pairs}

    probs = [(ci, h, d) for ci in range(nchunk) for h in range(GDN_HEADS) for d in range(2)]
    beta, gc, gt, decay, a_mats = {}, {}, {}, {}, []
    for (ci, h, d) in probs:
        col = d * GDN_HEADS + h
        gi = gates[ci]
        beta[ci, h, d] = gi["beta"][:, col:col + 1]
        gc[ci, h, d] = gi["gc"][d][:, 8 + col:9 + col]
        gt[ci, h, d] = gi["gtot"][:, 8 + col:9 + col]
        grow = gi["gr"][d][col:col + 1, :]
        decay[ci, h, d] = jnp.exp(jnp.where(incl[d], gc[ci, h, d] - grow, -jnp.inf))
        a_mats.append(jnp.where(strict[d], beta[ci, h, d] * kk[ci, h] * decay[ci, h, d], 0.0))
    tinv = _inv_unit_triangular_many(a_mats)

    eg, sols, kdts = {}, {}, {}
    for n, (ci, h, d) in enumerate(probs):
        p = (ci, h, d)
        eg[p] = jnp.exp(gc[p])
        kh = sl(kn[h], ci)
        rhs = jnp.concatenate([sl(vv[h], ci) * beta[p], kh * (beta[p] * eg[p])], axis=-1).astype(BF16)
        sols[p] = _dot(tinv[n].astype(BF16), rhs)
        kdts[p] = _dot_tn((kh * jnp.exp(gt[p] - gc[p])).astype(BF16), eye_bf)
    for (ci, h, d) in probs:
        p = (ci, h, d)
        u_ref[0, d, h, ci] = sols[p][:, 0:128]
        wq_ref[0, d, h, ci, 0:c, :] = sols[p][:, 128:256].astype(BF16)
        wq_ref[0, d, h, ci, c:2 * c, :] = (sl(qn[h], ci) * eg[p]).astype(BF16)
        qk_ref[0, d, h, ci] = (qkm[ci, h] * decay[p]).astype(BF16)
        kdt_ref[0, d, h, ci] = kdts[p].astype(BF16)
        dl_ref[0, d, h, ci] = jnp.broadcast_to(jnp.exp(gt[p][0:8, :]), (8, 128))


def _gdn_prepare(qkv, conv_w, ba, bat_chunks, prow, pcol, l):
    b, lt, width = qkv.shape
    tm = ROW_TILE
    c = GDN_CHUNK
    cpb = tm // c
    nblk = lt // tm
    nch = lt // c
    hd = (b, 2, GDN_HEADS, nch)

    def blk(shape_tail, dtype):
        return (pl.BlockSpec((1, 2, GDN_HEADS, cpb) + shape_tail, lambda i, j: (i, 0, 0, j, 0, 0)),
                jax.ShapeDtypeStruct(hd + shape_tail, dtype))

    specs = [blk((c, 128), F32), blk((2 * c, 128), BF16), blk((c, c), BF16), blk((GDN_DK, c), BF16),
             blk((8, 128), F32)]
    return pl.pallas_call(
        functools.partial(_gdn_prep_kernel, n_lat_blocks=l // tm, ctx_len=lt - l),
        grid=(b, nblk),
        in_specs=[pl.BlockSpec((1, tm, width), lambda i, j: (i, j, 0)),
                  pl.BlockSpec((8, width), lambda i, j: (0, 0)),
                  pl.BlockSpec((1, tm, 128), lambda i, j: (i, j, 0)),
                  pl.BlockSpec((1, cpb, 16, c), lambda i, j: (i, j, 0, 0)),
                  pl.BlockSpec((2, 128), lambda i, j: (0, 0)),
                  pl.BlockSpec((8, 2), lambda i, j: (0, 0))],
        out_specs=[s for s, _ in specs],
        out_shape=[o for _, o in specs],
        compiler_params=_cparams(("parallel", "parallel")),
        name="gdn_prepare",
    )(qkv, conv_w, ba, bat_chunks, prow, pcol)


SCAN_GROUP = 2


def _gdn_scan_kernel(uf, wqf, qkf, kdtf, dlf, ub, wqb, qkb, kdtb, dlb, of_ref, ob_ref, s_ref):
    t = pl.program_id(0)
    nb = s_ref.shape[0]
    c = GDN_CHUNK

    @pl.when(t == 0)
    def _():
        s_ref[...] = jnp.zeros_like(s_ref)

    refs = ((uf, wqf, qkf, kdtf, dlf, of_ref), (ub, wqb, qkb, kdtb, dlb, ob_ref))
    for b0 in range(0, nb, SCAN_GROUP):
        chains = [(bi, d, h) for bi in range(b0, min(b0 + SCAN_GROUP, nb)) for d in range(2)
                  for h in range(GDN_HEADS)]
        s = {k: s_ref[k[0], k[1], k[2]] for k in chains}
        sb = {k: s[k].astype(BF16) for k in chains}
        r = {k: _dot(refs[k[1]][1][k[0], 0, k[2], 0], sb[k]) for k in chains}
        vb = {k: (refs[k[1]][0][k[0], 0, k[2], 0] - r[k][0:c]).astype(BF16) for k in chains}
        o = {k: r[k][c:2 * c] + _dot(refs[k[1]][2][k[0], 0, k[2], 0], vb[k]) for k in chains}
        sn = {k: s[k] * refs[k[1]][4][k[0], 0, k[2], 0][0:1, :] + _dot(refs[k[1]][3][k[0], 0, k[2], 0], vb[k])
              for k in chains}
        for k in chains:
            s_ref[k[0], k[1], k[2]] = sn[k]
            refs[k[1]][5][k[0], :, k[2] * 128:(k[2] + 1) * 128] = o[k]


def _gdn_scan(prep, l):
    u = prep[0]
    b, _, heads, nch, c, _ = u.shape
    n_lat = l // c
    n_ctx = nch - n_lat
    lt = nch * c

    def fwd_chunk(t):
        return jnp.where(t < n_ctx, n_lat + t, t - n_ctx)

    def bwd_chunk(t):
        return nch - 1 - t

    def spec(arr, chunk_of, d):
        tail = arr.shape[4:]
        return pl.BlockSpec((b, 1, heads, 1) + tail, lambda t: (0, d, 0, chunk_of(t), 0, 0))

    ins = [spec(a, fwd_chunk, 0) for a in prep] + [spec(a, bwd_chunk, 1) for a in prep]
    width = heads * 128
    return pl.pallas_call(
        _gdn_scan_kernel,
        grid=(nch,),
        in_specs=ins,
        out_specs=[pl.BlockSpec((b, c, width), lambda t: (0, fwd_chunk(t), 0)),
                   pl.BlockSpec((b, c, width), lambda t: (0, bwd_chunk(t), 0))],
        out_shape=[jax.ShapeDtypeStruct((b, lt, width), F32)] * 2,
        scratch_shapes=[pltpu.VMEM((b, 2, heads, GDN_DK, 128), F32)],
        compiler_params=_cparams(("arbitrary",)),
        name="gdn_scan",
    )(*prep, *prep)


def _merge_kernel(x_ref, z_ref, of_ref, ob_ref, gate_ref, br_ref, gt1_ref, sc2_ref, sh2_ref,
                  wglu_ref, bglu_ref, gnorm_ref, wa_ref, wb_ref, wo_ref, n2_ref, wr_ref, brt_ref,
                  x1_ref, h2_ref, lg_ref):
    z = z_ref[0]
    zf = z.astype(F32)
    ya = zf * _sigmoid(_dot(z, wglu_ref[...]) + bglu_ref[...])
    o = of_ref[0] + ob_ref[0]
    gate = gate_ref[0].astype(F32)
    parts = []
    for h in range(GDN_HEADS):
        oh = o[:, h * 128:(h + 1) * 128]
        parts.append(oh * lax.rsqrt(jnp.mean(oh * oh, axis=-1, keepdims=True) + EPS) * gnorm_ref[...])
    yb = jnp.concatenate(parts, axis=-1) * (gate * _sigmoid(gate))
    br = br_ref[0].astype(F32)
    d = x_ref.shape[2]
    ga = _sigmoid(br[:, 0:d])
    gb = _sigmoid(br[:, d:2 * d])
    m = ga * _dot(ya.astype(BF16), wa_ref[...]) + gb * _dot(yb.astype(BF16), wb_ref[...])
    mix = _dot(m.astype(BF16), wo_ref[...])
    x1 = x_ref[0] + gt1_ref[0] * mix
    x1_ref[0] = x1
    y2 = x1 * lax.rsqrt(jnp.mean(x1 * x1, axis=-1, keepdims=True) + EPS) * n2_ref[...]
    h2 = y2 * (1.0 + sc2_ref[0]) + sh2_ref[0]
    h2_ref[0] = h2.astype(BF16)
    lg_ref[0] = jnp.dot(h2, wr_ref[...], precision=HIGHEST, preferred_element_type=F32) + brt_ref[...]


def _merge(x, z, o_f, o_b, gate, br, mod3, wglu, bglu, gnorm, wa, wb, wo, norm2, wr, brt):
    b, l, d = x.shape
    tm = ROW_TILE
    tok = lambda width: pl.BlockSpec((1, tm, width), lambda i, j: (i, j, 0))
    modspec = lambda k: pl.BlockSpec((1, 1, d), lambda i, j: (i * 6 + k, 0, 0))
    full = lambda arr: pl.BlockSpec(arr.shape, lambda i, j: (0,) * arr.ndim)
    consts = [wglu, bglu, gnorm, wa, wb, wo, norm2, wr, brt]
    return pl.pallas_call(
        _merge_kernel,
        grid=(b, l // tm),
        in_specs=[tok(d), tok(512), tok(512), tok(512), tok(512), tok(2 * d),
                  modspec(2), modspec(4), modspec(3)] + [full(a) for a in consts],
        out_specs=[tok(d), tok(d), tok(128)],
        out_shape=[jax.ShapeDtypeStruct((b, l, d), F32),
                   jax.ShapeDtypeStruct((b, l, d), BF16),
                   jax.ShapeDtypeStruct((b, l, 128), F32)],
        compiler_params=_cparams(("parallel", "parallel")),
        name="branch_merge",
    )(x, z, o_f, o_b, gate, br, mod3, mod3, mod3, *consts)


def _moe_kernel(te_ref, nu_ref, *refs, n_pieces, tiles_per_piece):
    x_refs = refs[:n_pieces]
    wgu_ref, bgu_ref, wd_ref, bd_ref, y_ref, wgu_bf, wd_bf = refs[n_pieces:]
    i = pl.program_id(0)
    changed = jnp.logical_or(i == 0, te_ref[i] != te_ref[jnp.maximum(i - 1, 0)])

    @pl.when(jnp.logical_and(changed, i < nu_ref[0]))
    def _():
        wgu_bf[...] = wgu_ref[0].astype(BF16)
        wd_bf[...] = wd_ref[0].astype(BF16)

    @pl.when(i < nu_ref[0])
    def _():
        de = wd_ref.shape[1]
        piece = i // tiles_per_piece
        x = x_refs[n_pieces - 1][...]
        for p in range(n_pieces - 2, -1, -1):
            x = jnp.where(piece == p, x_refs[p][...], x)
        gu = _dot(x, wgu_bf[...]) + bgu_ref[0]
        gate = jnp.minimum(gu[:, 0:de], SWIGLU_LIMIT)
        up = jnp.clip(gu[:, de:2 * de], -SWIGLU_LIMIT, SWIGLU_LIMIT)
        act = (up + 1.0) * gate * _sigmoid(gate * SWIGLU_ALPHA)
        y_ref[...] = (_dot(act.astype(BF16), wd_bf[...]) + bd_ref[0]).astype(y_ref.dtype)

    @pl.when(i >= nu_ref[0])
    def _():
        y_ref[...] = jnp.zeros_like(y_ref)


def _moe_experts(x_pieces, tile_expert, n_used, w_gate_up, b_gate_up, w_down, b_down):
    n_pieces = len(x_pieces)
    rows_per_piece, d = x_pieces[0].shape
    tm = MOE_TILE
    tpp = rows_per_piece // tm
    e, _, n2 = w_gate_up.shape
    de = w_down.shape[1]

    def piece_spec(p):
        return pl.BlockSpec((tm, d), lambda i, te, nu: (jnp.clip(i - p * tpp, 0, tpp - 1), 0))

    grid_spec = pltpu.PrefetchScalarGridSpec(
        num_scalar_prefetch=2,
        grid=(n_pieces * tpp,),
        in_specs=[piece_spec(p) for p in range(n_pieces)] + [
                  pl.BlockSpec((1, d, n2), lambda i, te, nu: (te[i], 0, 0)),
                  pl.BlockSpec((1, 1, n2), lambda i, te, nu: (te[i], 0, 0)),
                  pl.BlockSpec((1, de, d), lambda i, te, nu: (te[i], 0, 0)),
                  pl.BlockSpec((1, 1, d), lambda i, te, nu: (te[i], 0, 0))],
        out_specs=pl.BlockSpec((tm, d), lambda i, te, nu: (i, 0)),
        scratch_shapes=[pltpu.VMEM((d, n2), BF16), pltpu.VMEM((de, d), BF16)],
    )
    return pl.pallas_call(
        functools.partial(_moe_kernel, n_pieces=n_pieces, tiles_per_piece=tpp),
        grid_spec=grid_spec,
        out_shape=jax.ShapeDtypeStruct((n_pieces * rows_per_piece, d), BF16),
        compiler_params=_cparams(("arbitrary",)),
        name="moe_experts",
    )(tile_expert, n_used, *x_pieces, w_gate_up, b_gate_up.reshape(e, 1, n2), w_down, b_down.reshape(e, 1, d))


def _route(logits, n_tokens):
    tm = MOE_TILE
    top_val, top_idx = lax.top_k(logits, TOP_K)
    weights = jax.nn.softmax(top_val, axis=-1)
    flat_e = top_idx.reshape(-1).astype(jnp.int32)
    n_assign = flat_e.shape[0]
    iota = jnp.arange(n_assign, dtype=jnp.int32)
    sorted_e, order = lax.sort((flat_e, iota), num_keys=1, is_stable=True)
    counts = jnp.sum(jax.nn.one_hot(flat_e, N_EXPERTS, dtype=jnp.int32), axis=0)
    padded = ((counts + tm - 1) // tm) * tm
    pad_end = jnp.cumsum(padded)
    pad_start = pad_end - padded
    raw_start = jnp.cumsum(counts) - counts
    dest = pad_start[sorted_e] + (iota - raw_start[sorted_e])
    _, pos = lax.sort((order, dest), num_keys=1)
    pos = pos.reshape(n_tokens, TOP_K)
    n_rows = n_assign + N_EXPERTS * tm
    n_tiles = n_rows // tm
    n_used = (pad_end[-1] // tm).astype(jnp.int32)
    tile_start = jnp.arange(n_tiles, dtype=jnp.int32) * tm
    tile_expert = jnp.sum((tile_start[:, None] >= pad_end[None, :]).astype(jnp.int32), axis=1)
    last_e = jnp.sum((pad_end[-1] - 1 >= pad_end).astype(jnp.int32))
    tile_expert = jnp.minimum(tile_expert, last_e).astype(jnp.int32)
    off = jnp.arange(tm, dtype=jnp.int32)[None, :] + (tile_start - pad_start[tile_expert])[:, None]
    valid = off < counts[tile_expert][:, None]
    sidx = jnp.clip(raw_start[tile_expert][:, None] + off, 0, n_assign - 1)
    src_token = jnp.where(valid, (order // TOP_K)[sidx], 0).reshape(-1)
    return weights, src_token, pos, tile_expert, n_used.reshape(1)


def _final_kernel(x1_ref, y0_ref, y1_ref, y2_ref, y3_ref, wt_ref, gt2_ref, nf_ref, o_ref):
    wt = wt_ref[0]
    moe = (wt[:, 0:1] * y0_ref[0].astype(F32) + wt[:, 1:2] * y1_ref[0].astype(F32)
           + wt[:, 2:3] * y2_ref[0].astype(F32) + wt[:, 3:4] * y3_ref[0].astype(F32))
    x2 = x1_ref[0] + gt2_ref[0] * moe
    o_ref[0] = x2 * lax.rsqrt(jnp.mean(x2 * x2, axis=-1, keepdims=True) + EPS) * nf_ref[...]


def _final(x1, ys, wt, mod3, norm_f):
    b, l, d = x1.shape
    tm = ROW_TILE
    tok = lambda width: pl.BlockSpec((1, tm, width), lambda i, j: (i, j, 0))
    return pl.pallas_call(
        _final_kernel,
        grid=(b, l // tm),
        in_specs=[tok(d)] * 5 + [tok(128), pl.BlockSpec((1, 1, d), lambda i, j: (i * 6 + 5, 0, 0)),
                                 pl.BlockSpec((1, d), lambda i, j: (0, 0))],
        out_specs=tok(d),
        out_shape=jax.ShapeDtypeStruct((b, l, d), F32),
        compiler_params=_cparams(("parallel", "parallel")),
        name="combine_final_norm",
    )(x1, *ys, wt, mod3, norm_f.reshape(1, d))


def _split_hi_lo(w):
    hi = w.astype(BF16)
    lo = (w - hi.astype(F32)).astype(BF16)
    return hi, lo


def kernel(x, c, ctx, c_ctx, w_mod, b_mod, norm1, w_in, s5_lam_re, s5_lam_im, s5_log_step, s5_b_re, s5_b_im, s5_c_re, s5_c_im, s5_d, s5_w_glu, s5_b_glu, gdn_conv, gdn_a_log, gdn_dt_bias, gdn_norm, w_branch_a, w_branch_b, w_out, norm2, w_router, b_router, w_gate_up, b_gate_up, w_down, b_down, norm_f):
    b, l, d = x.shape
    lc = ctx.shape[1]
    depth = w_mod.shape[0]
    assert depth == 1, "single-layer block: the context stream has no consumer after the token mixer"
    assert l % ROW_TILE == 0 and lc % ROW_TILE == 0 and lc & (lc - 1) == 0 and b <= 8
    ly = 0

    cc = jnp.zeros((16, d), F32).at[0:b].set(c).at[b].set(c_ctx)
    mod = _modulation(cc, w_mod[ly], b_mod[ly])
    mod3 = mod[0:b + 1].reshape((b + 1) * 6, 1, d)

    wi = w_in[ly]
    o_u, o_qkv, o_gate, o_ba, o_br = 0, 512, 2048, 2560, 2576
    w_ba = wi[:, o_ba:o_br]
    ba_hi, ba_lo = _split_hi_lo(w_ba)
    pad = lambda t: jnp.pad(t, ((0, 0), (0, 128 - t.shape[1])))
    w_main = jnp.concatenate([wi[:, o_u:o_ba].astype(BF16), wi[:, o_br:].astype(BF16), pad(ba_hi), pad(ba_lo)], axis=1)
    wt = jnp.concatenate([ba_hi.T, ba_lo.T], axis=0)
    u_all, qkv, gate, br, ba, bat = _in_projection(x, ctx, mod3, norm1[ly], w_main, pad(ba_hi), wt, ba_hi.T)

    mats = _s5_matrices(s5_lam_re[ly], s5_lam_im[ly], s5_log_step[ly], s5_b_re[ly], s5_b_im[ly],
                        s5_c_re[ly], s5_c_im[ly], s5_d[ly])
    z = _s5_mixer(u_all, l, mats)

    lt = l + lc
    nch = lt // GDN_CHUNK
    bat_chunks = jnp.transpose(bat.reshape(b, 16, nch, GDN_CHUNK), (0, 2, 1, 3))
    ea = jnp.exp(gdn_a_log[ly].astype(F32)).reshape(-1)
    dtb = gdn_dt_bias[ly].astype(F32).reshape(-1)
    prow = jnp.zeros((2, 128), F32).at[0, 8:16].set(ea).at[1, 8:16].set(dtb)
    pcol = jnp.stack([ea, dtb], axis=1)
    conv_w = jnp.zeros((8, qkv.shape[2]), F32).at[0:CONV_K].set(gdn_conv[ly].astype(F32))
    prep = _gdn_prepare(qkv, conv_w, ba, bat_chunks, prow, pcol, l)
    o_f, o_b = _gdn_scan(prep, l)

    wr = jnp.pad(w_router[ly].astype(F32), ((0, 0), (0, 128 - N_EXPERTS)))
    brt = jnp.pad(b_router[ly].astype(F32), (0, 128 - N_EXPERTS)).reshape(1, 128)
    x1, h2, logits = _merge(x, z, o_f, o_b, gate, br, mod3,
                            s5_w_glu[ly].astype(BF16), s5_b_glu[ly].astype(F32).reshape(1, -1),
                            gdn_norm[ly].astype(F32).reshape(1, -1),
                            w_branch_a[ly].astype(BF16), w_branch_b[ly].astype(BF16), w_out[ly].astype(BF16),
                            norm2[ly].astype(F32).reshape(1, d), wr, brt)

    n_tok = b * l
    weights, src_token, pos, tile_expert, n_used = _route(logits.reshape(n_tok, 128)[:, 0:N_EXPERTS], n_tok)
    h2f = h2.reshape(n_tok, d)
    xs = [h2f[piece] for piece in jnp.split(src_token, src_token.shape[0] // n_tok)]
    ys = _moe_experts(xs, tile_expert, n_used, w_gate_up[ly], b_gate_up[ly], w_down[ly], b_down[ly])
    yk = [ys[pos[:, k]].reshape(b, l, d) for k in range(TOP_K)]
    wt4 = jnp.pad(weights, ((0, 0), (0, 128 - TOP_K))).reshape(b, l, 128)
    return _final(x1, yk, wt4, mod3, norm_f)
```

```python
import functools

import numpy as np
import jax
import jax.numpy as jnp
from jax import lax
from jax.experimental import pallas as pl
from jax.experimental.pallas import tpu as pltpu

F32 = jnp.float32
BF16 = jnp.bfloat16
HIGHEST = lax.Precision.HIGHEST

EPS = 1e-6
GRID_W = 64

S5_GROUP = 16
S5_STATE = 64
S5_Q = 16

GDN_HEADS = 4
GDN_DK = 128
GDN_CHUNK = 64
CONV_K = 5

N_EXPERTS = 32
TOP_K = 4
SWIGLU_LIMIT = 7.0
SWIGLU_ALPHA = 1.702

ROW_TILE = 256
MOE_TILE = 512
VMEM_LIMIT = 56 * 1024 * 1024


def _cparams(sem):
    return pltpu.CompilerParams(dimension_semantics=sem, vmem_limit_bytes=VMEM_LIMIT)


def _sigmoid(v):
    return 1.0 / (1.0 + jnp.exp(-v))


def _softplus(v):
    return jnp.maximum(v, 0.0) + jnp.log(1.0 + jnp.exp(-jnp.abs(v)))


def _dot(a, b):
    return jnp.dot(a, b, preferred_element_type=F32)


def _dot_nt(a, b):
    return lax.dot_general(a, b, (((1,), (1,)), ((), ())), preferred_element_type=F32)


def _dot_tn(a, b):
    return lax.dot_general(a, b, (((0,), (0,)), ((), ())), preferred_element_type=F32)


def _mod_kernel(c_ref, w_ref, b_ref, o_ref):
    cc = c_ref[...]
    s = cc * _sigmoid(cc)
    o_ref[...] = jnp.dot(s, w_ref[...], precision=HIGHEST, preferred_element_type=F32) + b_ref[...]


def _modulation(cc, w_mod, b_mod):
    rows, d = cc.shape
    n = w_mod.shape[1]
    bn = d
    return pl.pallas_call(
        _mod_kernel,
        grid=(n // bn,),
        in_specs=[pl.BlockSpec((rows, d), lambda j: (0, 0)),
                  pl.BlockSpec((d, bn), lambda j: (0, j)),
                  pl.BlockSpec((1, bn), lambda j: (0, j))],
        out_specs=pl.BlockSpec((rows, bn), lambda j: (0, j)),
        out_shape=jax.ShapeDtypeStruct((rows, n), F32),
        compiler_params=_cparams(("arbitrary",)),
        name="adaln_mod",
    )(cc, w_mod, b_mod.reshape(1, n))


def _inproj_kernel(x_ref, c_ref, sc_ref, sh_ref, g_ref, w_ref, wlo_ref, wt_ref, wtlo_ref,
                   u_ref, qkv_ref, gate_ref, br_ref, ba_ref, bat_ref, *, n_lat_blocks):
    j = pl.program_id(1)
    is_ctx = j >= n_lat_blocks
    x = jnp.where(is_ctx, c_ref[0], x_ref[0])
    ms = jnp.mean(x * x, axis=-1, keepdims=True)
    y = x * lax.rsqrt(ms + EPS) * g_ref[...]
    h = y * (1.0 + sc_ref[0]) + sh_ref[0]
    h_hi = h.astype(BF16)
    h_lo = (h - h_hi.astype(F32)).astype(BF16)
    z = _dot(h_hi, w_ref[...])
    u_ref[0] = z[:, 0:512].astype(BF16)
    qkv_ref[0] = z[:, 512:2048].astype(BF16)
    gate_ref[0] = z[:, 2048:2560].astype(BF16)
    br_ref[0] = z[:, 2560:4608].astype(BF16)
    ba_ref[0] = z[:, 4608:4736] + z[:, 4736:4864] + _dot(h_lo, wlo_ref[...])
    rt = _dot_nt(wt_ref[...], h_hi)
    bat_ref[0] = rt[0:16] + rt[16:32] + _dot_nt(wtlo_ref[...], h_lo)


def _in_projection(x, ctx, mod3, norm1, w_main, w_lo, wt, wt_lo):
    b, l, d = x.shape
    lc = ctx.shape[1]
    tm = ROW_TILE
    nl, nc = l // tm, lc // tm
    nb = mod3.shape[0] // 6 - 1
    lt = l + lc
    ncols = w_main.shape[1]

    def x_map(i, j):
        return (i, jnp.minimum(j, nl - 1), 0)

    def c_map(i, j):
        return (i, jnp.maximum(j - nl, 0), 0)

    def mod_map(k):
        return lambda i, j: (jnp.where(j >= nl, nb, i) * 6 + k, 0, 0)

    def tok(width):
        return pl.BlockSpec((1, tm, width), lambda i, j: (i, j, 0))

    const2 = lambda i, j: (0, 0)
    outs = pl.pallas_call(
        functools.partial(_inproj_kernel, n_lat_blocks=nl),
        grid=(b, nl + nc),
        in_specs=[pl.BlockSpec((1, tm, d), x_map),
                  pl.BlockSpec((1, tm, d), c_map),
                  pl.BlockSpec((1, 1, d), mod_map(1)),
                  pl.BlockSpec((1, 1, d), mod_map(0)),
                  pl.BlockSpec((1, d), const2),
                  pl.BlockSpec((d, ncols), const2),
                  pl.BlockSpec((d, 128), const2),
                  pl.BlockSpec((32, d), const2),
                  pl.BlockSpec((16, d), const2)],
        out_specs=[tok(512), tok(1536), tok(512), tok(2048), tok(128),
                   pl.BlockSpec((1, 16, tm), lambda i, j: (i, 0, j))],
        out_shape=[jax.ShapeDtypeStruct((b, lt, 512), BF16),
                   jax.ShapeDtypeStruct((b, lt, 1536), BF16),
                   jax.ShapeDtypeStruct((b, lt, 512), BF16),
                   jax.ShapeDtypeStruct((b, lt, 2048), BF16),
                   jax.ShapeDtypeStruct((b, lt, 128), F32),
                   jax.ShapeDtypeStruct((b, 16, lt), F32)],
        compiler_params=_cparams(("parallel", "arbitrary")),
        name="in_projection",
    )(x, ctx, mod3, mod3, norm1.reshape(1, d), w_main, w_lo, wt, wt_lo)
    return outs


def _s5_matrices(lam_re, lam_im, log_step, b_re, b_im, c_re, c_im, d_skip):
    q = S5_Q
    lr = jnp.minimum(lam_re.astype(F32), -1e-4)
    li = lam_im.astype(F32)
    dt = jnp.exp(log_step.astype(F32))[..., None]
    pw = jnp.arange(q + 1, dtype=F32)[:, None, None, None]
    mag = jnp.exp(pw * (lr * dt))
    ang = pw * (li * dt)
    pr, pi = mag * jnp.cos(ang), mag * jnp.sin(ang)
    ar, ai = pr[1], pi[1]
    den = lr * lr + li * li
    fr = ((ar - 1.0) * lr + ai * li) / den
    fi = (ai * lr - (ar - 1.0) * li) / den
    br, bi = b_re.astype(F32), b_im.astype(F32)
    bbr = fr[..., None] * br - fi[..., None] * bi
    bbi = fr[..., None] * bi + fi[..., None] * br
    cr, ci = c_re.astype(F32), c_im.astype(F32)
    car = cr[None] * pr[:, :, :, None, :] - ci[None] * pi[:, :, :, None, :]
    cai = cr[None] * pi[:, :, :, None, :] + ci[None] * pr[:, :, :, None, :]
    bbr_t = jnp.swapaxes(bbr, -1, -2)[None, :, :, None]
    bbi_t = jnp.swapaxes(bbi, -1, -2)[None, :, :, None]
    kern = jnp.sum(car[..., None, :] * bbr_t - cai[..., None, :] * bbi_t, axis=-1)
    g = lr.shape[1]
    jj = np.arange(q)[:, None]
    ii = np.arange(q)[None, :]
    df = ii - jj
    mf = jnp.where((df >= 0)[:, :, None, None, None], kern[:q, 0][np.clip(df, 0, q - 1)], 0.0)
    mb = jnp.where((df <= 0)[:, :, None, None, None], kern[:q, 1][np.clip(-df, 0, q - 1)], 0.0)
    m = jnp.transpose(mf + mb, (2, 0, 4, 1, 3))
    skip = (jnp.eye(q, dtype=F32)[None, :, None, :, None]
            * jnp.eye(S5_GROUP, dtype=F32)[None, None, :, None, :]
            * d_skip.astype(F32).reshape(g, 1, S5_GROUP, 1, 1))
    m = (m + skip).reshape(g, q * S5_GROUP, q * S5_GROUP)

    def p_block(powers, d):
        ppr, ppi = pr[powers, d], pi[powers, d]
        re = ppr[..., None] * bbr[d][None] - ppi[..., None] * bbi[d][None]
        im = ppr[..., None] * bbi[d][None] + ppi[..., None] * bbr[d][None]
        to = lambda t: jnp.transpose(t, (1, 0, 3, 2)).reshape(g, q * S5_GROUP, S5_STATE)
        return to(re), to(im)

    pf_re, pf_im = p_block(np.arange(q - 1, -1, -1), 0)
    pb_re, pb_im = p_block(np.arange(q), 1)
    p = jnp.concatenate([pf_re, pb_re, pf_im, pb_im], axis=-1)

    def r_block(powers, d):
        to = lambda t: jnp.transpose(t, (1, 3, 0, 2)).reshape(g, S5_STATE, q * S5_GROUP)
        return to(car[powers, d]), to(-cai[powers, d])

    rf_re, rf_im = r_block(np.arange(1, q + 1), 0)
    rb_re, rb_im = r_block(np.arange(q, 0, -1), 1)
    r = jnp.concatenate([rf_re, rb_re, rf_im, rb_im], axis=1)
    a16 = jnp.stack([jnp.concatenate([pr[q, 0], pr[q, 1]], axis=-1),
                     jnp.concatenate([pi[q, 0], pi[q, 1]], axis=-1)], axis=1)
    return m.astype(BF16), p.astype(BF16), r.astype(BF16), a16


def _s5_kernel(u_ref, m_ref, p_ref, r_ref, a_ref, z_ref, v_ref, fre_ref, fim_ref, bre_ref, bim_ref,
               *, n_lat, n_ctx, nb):
    u = u_ref[0]
    v_ref[...] = _dot(u, p_ref[0])
    a = a_ref[0]
    a_re, a_im = a[0:1, :], a[1:2, :]
    fwd_lane = lax.broadcasted_iota(jnp.int32, (nb, 2 * S5_STATE), 1) < S5_STATE

    def scan(base, n, s0, store):
        def body(t, carry):
            s_re, s_im = carry
            rf = pl.multiple_of((base + t) * nb, nb)
            rb = pl.multiple_of((base + n - 1 - t) * nb, nb)
            if store:
                fre_ref[pl.ds(rf, nb), :] = s_re
                fim_ref[pl.ds(rf, nb), :] = s_im
                bre_ref[pl.ds(rb, nb), :] = s_re
                bim_ref[pl.ds(rb, nb), :] = s_im
            vf = v_ref[pl.ds(rf, nb), :]
            vb = v_ref[pl.ds(rb, nb), :]
            v_re = jnp.where(fwd_lane, vf[:, 0:128], vb[:, 0:128])
            v_im = jnp.where(fwd_lane, vf[:, 128:256], vb[:, 128:256])
            n_re = a_re * s_re - a_im * s_im + v_re
            n_im = a_re * s_im + a_im * s_re + v_im
            return n_re, n_im
        return lax.fori_loop(0, n, body, s0)

    zero = jnp.zeros((nb, 2 * S5_STATE), F32)
    s_ctx = scan(n_lat, n_ctx, (zero, zero), False)
    scan(0, n_lat, s_ctx, True)
    rows = n_lat * nb
    lane = lax.broadcasted_iota(jnp.int32, (rows, 2 * S5_STATE), 1) < S5_STATE
    st = jnp.concatenate([jnp.where(lane, fre_ref[...], bre_ref[...]),
                          jnp.where(lane, fim_ref[...], bim_ref[...])], axis=-1).astype(BF16)
    y = _dot(u[0:rows], m_ref[0]) + _dot(st, r_ref[0])
    z_ref[0] = jax.nn.gelu(y).astype(BF16)


def _s5_mixer(u_all, l, mats):
    m, p, r, a16 = mats
    b, lt, width = u_all.shape
    g = width // S5_GROUP
    q = S5_Q
    n_all, n_lat = lt // q, l // q
    qc = q * S5_GROUP
    ug = u_all.reshape(b, n_all, q, g, S5_GROUP)
    ug = jnp.transpose(ug, (3, 1, 0, 2, 4)).reshape(g, n_all * b, qc)
    rows = n_lat * b
    z = pl.pallas_call(
        functools.partial(_s5_kernel, n_lat=n_lat, n_ctx=n_all - n_lat, nb=b),
        grid=(g,),
        in_specs=[pl.BlockSpec((1, n_all * b, qc), lambda i: (i, 0, 0)),
                  pl.BlockSpec((1, qc, qc), lambda i: (i, 0, 0)),
                  pl.BlockSpec((1, qc, qc), lambda i: (i, 0, 0)),
                  pl.BlockSpec((1, qc, qc), lambda i: (i, 0, 0)),
                  pl.BlockSpec((1, 2, 2 * S5_STATE), lambda i: (i, 0, 0))],
        out_specs=pl.BlockSpec((1, rows, qc), lambda i: (i, 0, 0)),
        out_shape=jax.ShapeDtypeStruct((g, rows, qc), BF16),
        scratch_shapes=[pltpu.VMEM((n_all * b, qc), F32)] + [pltpu.VMEM((rows, 2 * S5_STATE), F32)] * 4,
        compiler_params=_cparams(("parallel",)),
        name="s5_scan",
    )(ug, m, p, r, a16)
    z = z.reshape(g, n_lat, b, q, S5_GROUP)
    return jnp.transpose(z, (2, 1, 3, 0, 4)).reshape(b, l, width)


def _inv_unit_triangular_many(mats):
    c = mats[0].shape[0]
    eye = (lax.broadcasted_iota(jnp.int32, (c, c), 0) == lax.broadcasted_iota(jnp.int32, (c, c), 1)).astype(F32)
    prods = [eye - a for a in mats]
    pows = [a.astype(BF16) for a in mats]
    for _ in range(int(np.log2(c)) - 1):
        pows = [_dot(a, a).astype(BF16) for a in pows]
        prods = [p + _dot(p.astype(BF16), a) for p, a in zip(prods, pows)]
    return prods


def _gdn_prep_kernel(qkv_ref, cw_ref, ba_ref, bat_ref, prow_ref, pcol_ref,
                     u_ref, wq_ref, qk_ref, kdt_ref, dl_ref, *, n_lat_blocks, ctx_len):
    j = pl.program_id(1)
    rows = qkv_ref.shape[1]
    c = GDN_CHUNK
    nchunk = rows // c
    seg = jnp.where(j >= n_lat_blocks, ctx_len, GRID_W)
    x = qkv_ref[0].astype(F32)
    pos = lax.broadcasted_iota(jnp.int32, (rows, 1), 0) & (seg - 1)
    acc = x * cw_ref[2:3, :]
    for s in (-2, -1, 1, 2):
        shifted = pltpu.roll(x, (-s) % rows, 0)
        ok = jnp.logical_and(pos + s >= 0, pos + s < seg)
        acc = acc + jnp.where(ok, shifted, 0.0) * cw_ref[2 + s:3 + s, :]
    act = acc * _sigmoid(acc)

    ii = lax.broadcasted_iota(jnp.int32, (c, c), 0)
    jj = lax.broadcasted_iota(jnp.int32, (c, c), 1)
    incl = (ii >= jj, ii <= jj)
    strict = (ii > jj, ii < jj)
    tri_low = incl[0].astype(F32)
    tri_up = incl[1].astype(F32)
    eye_bf = (ii == jj).astype(BF16)
    ones = jnp.ones((c, c), F32)
    lane = lax.broadcasted_iota(jnp.int32, (c, 128), 1)
    prow = prow_ref[...]
    pcol = pcol_ref[...]
    hp = dict(precision=HIGHEST, preferred_element_type=F32)

    gates = []
    for ci in range(nchunk):
        ba = ba_ref[0, ci * c:(ci + 1) * c, :]
        g_all = -prow[0:1, :] * _softplus(ba + prow[1:2, :])
        g_all = jnp.where(jnp.logical_and(lane >= 8, lane < 16), g_all, 0.0)
        bat = bat_ref[0, ci]
        g_row = -pcol[:, 0:1] * _softplus(bat[8:16, :] + pcol[:, 1:2])
        gates.append(dict(beta=_sigmoid(ba),
                          gc=(jnp.dot(tri_low, g_all, **hp), jnp.dot(tri_up, g_all, **hp)),
                          gtot=jnp.dot(ones, g_all, **hp),
                          gr=(jnp.dot(g_row, tri_up, **hp), jnp.dot(g_row, tri_low, **hp))))

    qn, kn, vv = [], [], []
    for h in range(GDN_HEADS):
        qh = act[:, h * 128:(h + 1) * 128]
        kh = act[:, 512 + h * 128:512 + (h + 1) * 128]
        qn.append(qh * lax.rsqrt(jnp.sum(qh * qh, axis=-1, keepdims=True) + EPS) * (GDN_DK ** -0.5))
        kn.append(kh * lax.rsqrt(jnp.sum(kh * kh, axis=-1, keepdims=True) + EPS))
        vv.append(act[:, 1024 + h * 128:1024 + (h + 1) * 128])

    pairs = [(ci, h) for ci in range(nchunk) for h in range(GDN_HEADS)]
    sl = lambda t, ci: t[ci * c:(ci + 1) * c]
    kb = {p: sl(kn[p[1]], p[0]).astype(BF16) for p in pairs}
    kk = {p: _dot_nt(kb[p], kb[p]) for p in pairs}
    qkm = {p: _dot_nt(sl(qn[p[1]], p[0]).astype(BF16), kb[p]) for p in pairs}

    probs = [(ci, h, d) for ci in range(nchunk) for h in range(GDN_HEADS) for d in range(2)]
    beta, gc, gt, decay, a_mats = {}, {}, {}, {}, []
    for (ci, h, d) in probs:
        col = d * GDN_HEADS + h
        gi = gates[ci]
        beta[ci, h, d] = gi["beta"][:, col:col + 1]
        gc[ci, h, d] = gi["gc"][d][:, 8 + col:9 + col]
        gt[ci, h, d] = gi["gtot"][:, 8 + col:9 + col]
        grow = gi["gr"][d][col:col + 1, :]
        decay[ci, h, d] = jnp.exp(jnp.where(incl[d], gc[ci, h, d] - grow, -jnp.inf))
        a_mats.append(jnp.where(strict[d], beta[ci, h, d] * kk[ci, h] * decay[ci, h, d], 0.0))
    tinv = _inv_unit_triangular_many(a_mats)

    eg, sols, kdts = {}, {}, {}
    for n, (ci, h, d) in enumerate(probs):
        p = (ci, h, d)
        eg[p] = jnp.exp(gc[p])
        kh = sl(kn[h], ci)
        rhs = jnp.concatenate([sl(vv[h], ci) * beta[p], kh * (beta[p] * eg[p])], axis=-1).astype(BF16)
        sols[p] = _dot(tinv[n].astype(BF16), rhs)
        kdts[p] = _dot_tn((kh * jnp.exp(gt[p] - gc[p])).astype(BF16), eye_bf)
    for (ci, h, d) in probs:
        p = (ci, h, d)
        u_ref[0, d, h, ci] = sols[p][:, 0:128]
        wq_ref[0, d, h, ci, 0:c, :] = sols[p][:, 128:256].astype(BF16)
        wq_ref[0, d, h, ci, c:2 * c, :] = (sl(qn[h], ci) * eg[p]).astype(BF16)
        qk_ref[0, d, h, ci] = (qkm[ci, h] * decay[p]).astype(BF16)
        kdt_ref[0, d, h, ci] = kdts[p].astype(BF16)
        dl_ref[0, d, h, ci] = jnp.broadcast_to(jnp.exp(gt[p][0:8, :]), (8, 128))


def _gdn_prepare(qkv, conv_w, ba, bat_chunks, prow, pcol, l):
    b, lt, width = qkv.shape
    tm = ROW_TILE
    c = GDN_CHUNK
    cpb = tm // c
    nblk = lt // tm
    nch = lt // c
    hd = (b, 2, GDN_HEADS, nch)

    def blk(shape_tail, dtype):
        return (pl.BlockSpec((1, 2, GDN_HEADS, cpb) + shape_tail, lambda i, j: (i, 0, 0, j, 0, 0)),
                jax.ShapeDtypeStruct(hd + shape_tail, dtype))

    specs = [blk((c, 128), F32), blk((2 * c, 128), BF16), blk((c, c), BF16), blk((GDN_DK, c), BF16),
             blk((8, 128), F32)]
    return pl.pallas_call(
        functools.partial(_gdn_prep_kernel, n_lat_blocks=l // tm, ctx_len=lt - l),
        grid=(b, nblk),
        in_specs=[pl.BlockSpec((1, tm, width), lambda i, j: (i, j, 0)),
                  pl.BlockSpec((8, width), lambda i, j: (0, 0)),
                  pl.BlockSpec((1, tm, 128), lambda i, j: (i, j, 0)),
                  pl.BlockSpec((1, cpb, 16, c), lambda i, j: (i, j, 0, 0)),
                  pl.BlockSpec((2, 128), lambda i, j: (0, 0)),
                  pl.BlockSpec((8, 2), lambda i, j: (0, 0))],
        out_specs=[s for s, _ in specs],
        out_shape=[o for _, o in specs],
        compiler_params=_cparams(("parallel", "parallel")),
        name="gdn_prepare",
    )(qkv, conv_w, ba, bat_chunks, prow, pcol)


SCAN_GROUP = 2


def _gdn_scan_kernel(uf, wqf, qkf, kdtf, dlf, ub, wqb, qkb, kdtb, dlb, of_ref, ob_ref, s_ref):
    t = pl.program_id(0)
    nb = s_ref.shape[0]
    c = GDN_CHUNK

    @pl.when(t == 0)
    def _():
        s_ref[...] = jnp.zeros_like(s_ref)

    refs = ((uf, wqf, qkf, kdtf, dlf, of_ref), (ub, wqb, qkb, kdtb, dlb, ob_ref))
    for b0 in range(0, nb, SCAN_GROUP):
        chains = [(bi, d, h) for bi in range(b0, min(b0 + SCAN_GROUP, nb)) for d in range(2)
                  for h in range(GDN_HEADS)]
        s = {k: s_ref[k[0], k[1], k[2]] for k in chains}
        sb = {k: s[k].astype(BF16) for k in chains}
        r = {k: _dot(refs[k[1]][1][k[0], 0, k[2], 0], sb[k]) for k in chains}
        vb = {k: (refs[k[1]][0][k[0], 0, k[2], 0] - r[k][0:c]).astype(BF16) for k in chains}
        o = {k: r[k][c:2 * c] + _dot(refs[k[1]][2][k[0], 0, k[2], 0], vb[k]) for k in chains}
        sn = {k: s[k] * refs[k[1]][4][k[0], 0, k[2], 0][0:1, :] + _dot(refs[k[1]][3][k[0], 0, k[2], 0], vb[k])
              for k in chains}
        for k in chains:
            s_ref[k[0], k[1], k[2]] = sn[k]
            refs[k[1]][5][k[0], :, k[2] * 128:(k[2] + 1) * 128] = o[k]


def _gdn_scan(prep, l):
    u = prep[0]
    b, _, heads, nch, c, _ = u.shape
    n_lat = l // c
    n_ctx = nch - n_lat
    lt = nch * c

    def fwd_chunk(t):
        return jnp.where(t < n_ctx, n_lat + t, t - n_ctx)

    def bwd_chunk(t):
        return nch - 1 - t

    def spec(arr, chunk_of, d):
        tail = arr.shape[4:]
        return pl.BlockSpec((b, 1, heads, 1) + tail, lambda t: (0, d, 0, chunk_of(t), 0, 0))

    ins = [spec(a, fwd_chunk, 0) for a in prep] + [spec(a, bwd_chunk, 1) for a in prep]
    width = heads * 128
    return pl.pallas_call(
        _gdn_scan_kernel,
        grid=(nch,),
        in_specs=ins,
        out_specs=[pl.BlockSpec((b, c, width), lambda t: (0, fwd_chunk(t), 0)),
                   pl.BlockSpec((b, c, width), lambda t: (0, bwd_chunk(t), 0))],
        out_shape=[jax.ShapeDtypeStruct((b, lt, width), F32)] * 2,
        scratch_shapes=[pltpu.VMEM((b, 2, heads, GDN_DK, 128), F32)],
        compiler_params=_cparams(("arbitrary",)),
        name="gdn_scan",
    )(*prep, *prep)


def _merge_kernel(x_ref, z_ref, of_ref, ob_ref, gate_ref, br_ref, gt1_ref, sc2_ref, sh2_ref,
                  wglu_ref, bglu_ref, gnorm_ref, wa_ref, wb_ref, wo_ref, n2_ref, wr_ref, brt_ref,
                  x1_ref, h2_ref, lg_ref):
    z = z_ref[0]
    zf = z.astype(F32)
    ya = zf * _sigmoid(_dot(z, wglu_ref[...]) + bglu_ref[...])
    o = of_ref[0] + ob_ref[0]
    gate = gate_ref[0].astype(F32)
    parts = []
    for h in range(GDN_HEADS):
        oh = o[:, h * 128:(h + 1) * 128]
        parts.append(oh * lax.rsqrt(jnp.mean(oh * oh, axis=-1, keepdims=True) + EPS) * gnorm_ref[...])
    yb = jnp.concatenate(parts, axis=-1) * (gate * _sigmoid(gate))
    br = br_ref[0].astype(F32)
    d = x_ref.shape[2]
    ga = _sigmoid(br[:, 0:d])
    gb = _sigmoid(br[:, d:2 * d])
    m = ga * _dot(ya.astype(BF16), wa_ref[...]) + gb * _dot(yb.astype(BF16), wb_ref[...])
    mix = _dot(m.astype(BF16), wo_ref[...])
    x1 = x_ref[0] + gt1_ref[0] * mix
    x1_ref[0] = x1
    y2 = x1 * lax.rsqrt(jnp.mean(x1 * x1, axis=-1, keepdims=True) + EPS) * n2_ref[...]
    h2 = y2 * (1.0 + sc2_ref[0]) + sh2_ref[0]
    h2_ref[0] = h2
    lg_ref[0] = jnp.dot(h2, wr_ref[...], precision=HIGHEST, preferred_element_type=F32) + brt_ref[...]


def _merge(x, z, o_f, o_b, gate, br, mod3, wglu, bglu, gnorm, wa, wb, wo, norm2, wr, brt):
    b, l, d = x.shape
    tm = ROW_TILE
    tok = lambda width: pl.BlockSpec((1, tm, width), lambda i, j: (i, j, 0))
    modspec = lambda k: pl.BlockSpec((1, 1, d), lambda i, j: (i * 6 + k, 0, 0))
    full = lambda arr: pl.BlockSpec(arr.shape, lambda i, j: (0,) * arr.ndim)
    consts = [wglu, bglu, gnorm, wa, wb, wo, norm2, wr, brt]
    return pl.pallas_call(
        _merge_kernel,
        grid=(b, l // tm),
        in_specs=[tok(d), tok(512), tok(512), tok(512), tok(512), tok(2 * d),
                  modspec(2), modspec(4), modspec(3)] + [full(a) for a in consts],
        out_specs=[tok(d), tok(d), tok(128)],
        out_shape=[jax.ShapeDtypeStruct((b, l, d), F32),
                   jax.ShapeDtypeStruct((b, l, d), F32),
                   jax.ShapeDtypeStruct((b, l, 128), F32)],
        compiler_params=_cparams(("parallel", "parallel")),
        name="branch_merge",
    )(x, z, o_f, o_b, gate, br, mod3, mod3, mod3, *consts)


def _moe_kernel(te_ref, nu_ref, x_ref, wgu_ref, bgu_ref, wd_ref, bd_ref, *rest):
    y_ref, wgu_bf, wd_bf = rest[-3:]
    i = pl.program_id(0)
    changed = jnp.logical_or(i == 0, te_ref[i] != te_ref[jnp.maximum(i - 1, 0)])

    @pl.when(jnp.logical_and(changed, i < nu_ref[0]))
    def _():
        wgu_bf[...] = wgu_ref[0].astype(BF16)
        wd_bf[...] = wd_ref[0].astype(BF16)

    @pl.when(i < nu_ref[0])
    def _():
        de = wd_ref.shape[1]
        gu = _dot(x_ref[...].astype(BF16), wgu_bf[...]) + bgu_ref[0]
        gate = jnp.minimum(gu[:, 0:de], SWIGLU_LIMIT)
        up = jnp.clip(gu[:, de:2 * de], -SWIGLU_LIMIT, SWIGLU_LIMIT)
        act = (up + 1.0) * gate * _sigmoid(gate * SWIGLU_ALPHA)
        y_ref[...] = (_dot(act.astype(BF16), wd_bf[...]) + bd_ref[0]).astype(y_ref.dtype)

    @pl.when(i >= nu_ref[0])
    def _():
        y_ref[...] = jnp.zeros_like(y_ref)


def _moe_experts(x_pieces, tile_expert, n_used, w_gate_up, b_gate_up, w_down, b_down):
    n_pieces = len(x_pieces)
    rows_per_piece, dh = x_pieces[0].shape
    tm = MOE_TILE
    tpp = rows_per_piece // tm
    e, d, n2 = w_gate_up.shape
    de = w_down.shape[1]
    bgu = b_gate_up.reshape(e, 1, n2)
    bd = b_down.reshape(e, 1, d)
    ys = None
    for p, xp in enumerate(x_pieces):
        in_specs = [pl.BlockSpec((tm, dh), lambda i, te, nu: (i, 0)),
                    pl.BlockSpec((1, d, n2), lambda i, te, nu: (te[i], 0, 0)),
                    pl.BlockSpec((1, 1, n2), lambda i, te, nu: (te[i], 0, 0)),
                    pl.BlockSpec((1, de, d), lambda i, te, nu: (te[i], 0, 0)),
                    pl.BlockSpec((1, 1, d), lambda i, te, nu: (te[i], 0, 0))]
        args = [tile_expert[p * tpp:(p + 1) * tpp], jnp.clip(n_used - p * tpp, 0, tpp), xp,
                w_gate_up, bgu, w_down, bd]
        aliases = {}
        if ys is not None:
            in_specs.append(pl.BlockSpec(memory_space=pl.ANY))
            args.append(ys)
            aliases = {len(args) - 1: 0}
        ys = pl.pallas_call(
            _moe_kernel,
            grid_spec=pltpu.PrefetchScalarGridSpec(
                num_scalar_prefetch=2,
                grid=(tpp,),
                in_specs=in_specs,
                out_specs=pl.BlockSpec((tm, d), lambda i, te, nu, p=p: (p * tpp + i, 0)),
                scratch_shapes=[pltpu.VMEM((d, n2), BF16), pltpu.VMEM((de, d), BF16)],
            ),
            out_shape=jax.ShapeDtypeStruct((n_pieces * rows_per_piece, d), BF16),
            input_output_aliases=aliases,
            compiler_params=_cparams(("arbitrary",)),
            name=f"moe_experts_{p}",
        )(*args)
    return ys


def _route(logits, n_tokens):
    tm = MOE_TILE
    top_val, top_idx = lax.top_k(logits, TOP_K)
    weights = jax.nn.softmax(top_val, axis=-1)
    flat_e = top_idx.reshape(-1).astype(jnp.int32)
    n_assign = flat_e.shape[0]
    iota = jnp.arange(n_assign, dtype=jnp.int32)
    sorted_e, order = lax.sort((flat_e, iota), num_keys=1, is_stable=True)
    counts = jnp.sum(jax.nn.one_hot(flat_e, N_EXPERTS, dtype=jnp.int32), axis=0)
    padded = ((counts + tm - 1) // tm) * tm
    pad_end = jnp.cumsum(padded)
    pad_start = pad_end - padded
    raw_start = jnp.cumsum(counts) - counts
    dest = pad_start[sorted_e] + (iota - raw_start[sorted_e])
    _, pos = lax.sort((order, dest), num_keys=1)
    pos = pos.reshape(n_tokens, TOP_K)
    n_rows = n_assign + N_EXPERTS * tm
    n_tiles = n_rows // tm
    n_used = (pad_end[-1] // tm).astype(jnp.int32)
    tile_start = jnp.arange(n_tiles, dtype=jnp.int32) * tm
    tile_expert = jnp.sum((tile_start[:, None] >= pad_end[None, :]).astype(jnp.int32), axis=1)
    last_e = jnp.sum((pad_end[-1] - 1 >= pad_end).astype(jnp.int32))
    tile_expert = jnp.minimum(tile_expert, last_e).astype(jnp.int32)
    off = jnp.arange(tm, dtype=jnp.int32)[None, :] + (tile_start - pad_start[tile_expert])[:, None]
    valid = off < counts[tile_expert][:, None]
    sidx = jnp.clip(raw_start[tile_expert][:, None] + off, 0, n_assign - 1)
    src_token = jnp.where(valid, (order // TOP_K)[sidx], 0).reshape(-1)
    return weights, src_token, pos, tile_expert, n_used.reshape(1)


def _final_kernel(x1_ref, y0_ref, y1_ref, y2_ref, y3_ref, wt_ref, gt2_ref, nf_ref, o_ref):
    wt = wt_ref[0]
    moe = (wt[:, 0:1] * y0_ref[0, 0].astype(F32) + wt[:, 1:2] * y1_ref[0, 0].astype(F32)
           + wt[:, 2:3] * y2_ref[0, 0].astype(F32) + wt[:, 3:4] * y3_ref[0, 0].astype(F32))
    x2 = x1_ref[0] + gt2_ref[0] * moe
    o_ref[0] = x2 * lax.rsqrt(jnp.mean(x2 * x2, axis=-1, keepdims=True) + EPS) * nf_ref[...]


def _final(x1, yg, wt, mod3, norm_f):
    b, l, d = x1.shape
    tm = ROW_TILE
    tok = lambda width: pl.BlockSpec((1, tm, width), lambda i, j: (i, j, 0))
    ysel = lambda k: pl.BlockSpec((1, 1, tm, d), lambda i, j: (k, i, j, 0))
    return pl.pallas_call(
        _final_kernel,
        grid=(b, l // tm),
        in_specs=[tok(d)] + [ysel(k) for k in range(TOP_K)] + [tok(128), pl.BlockSpec((1, 1, d), lambda i, j: (i * 6 + 5, 0, 0)),
                                 pl.BlockSpec((1, d), lambda i, j: (0, 0))],
        out_specs=tok(d),
        out_shape=jax.ShapeDtypeStruct((b, l, d), F32),
        compiler_params=_cparams(("parallel", "parallel")),
        name="combine_final_norm",
    )(x1, *([yg] * TOP_K), wt, mod3, norm_f.reshape(1, d))


def _split_hi_lo(w):
    hi = w.astype(BF16)
    lo = (w - hi.astype(F32)).astype(BF16)
    return hi, lo


def kernel(x, c, ctx, c_ctx, w_mod, b_mod, norm1, w_in, s5_lam_re, s5_lam_im, s5_log_step, s5_b_re, s5_b_im, s5_c_re, s5_c_im, s5_d, s5_w_glu, s5_b_glu, gdn_conv, gdn_a_log, gdn_dt_bias, gdn_norm, w_branch_a, w_branch_b, w_out, norm2, w_router, b_router, w_gate_up, b_gate_up, w_down, b_down, norm_f):
    b, l, d = x.shape
    lc = ctx.shape[1]
    depth = w_mod.shape[0]
    assert depth == 1, "single-layer block: the context stream has no consumer after the token mixer"
    assert l % ROW_TILE == 0 and lc % ROW_TILE == 0 and lc & (lc - 1) == 0 and b <= 8
    ly = 0

    cc = jnp.zeros((16, d), F32).at[0:b].set(c).at[b].set(c_ctx)
    mod = _modulation(cc, w_mod[ly], b_mod[ly])
    mod3 = mod[0:b + 1].reshape((b + 1) * 6, 1, d)

    wi = w_in[ly]
    o_u, o_qkv, o_gate, o_ba, o_br = 0, 512, 2048, 2560, 2576
    w_ba = wi[:, o_ba:o_br]
    ba_hi, ba_lo = _split_hi_lo(w_ba)
    pad = lambda t: jnp.pad(t, ((0, 0), (0, 128 - t.shape[1])))
    w_main = jnp.concatenate([wi[:, o_u:o_ba].astype(BF16), wi[:, o_br:].astype(BF16), pad(ba_hi), pad(ba_lo)], axis=1)
    wt = jnp.concatenate([ba_hi.T, ba_lo.T], axis=0)
    u_all, qkv, gate, br, ba, bat = _in_projection(x, ctx, mod3, norm1[ly], w_main, pad(ba_hi), wt, ba_hi.T)

    mats = _s5_matrices(s5_lam_re[ly], s5_lam_im[ly], s5_log_step[ly], s5_b_re[ly], s5_b_im[ly],
                        s5_c_re[ly], s5_c_im[ly], s5_d[ly])
    z = _s5_mixer(u_all, l, mats)

    lt = l + lc
    nch = lt // GDN_CHUNK
    bat_chunks = jnp.transpose(bat.reshape(b, 16, nch, GDN_CHUNK), (0, 2, 1, 3))
    ea = jnp.exp(gdn_a_log[ly].astype(F32)).reshape(-1)
    dtb = gdn_dt_bias[ly].astype(F32).reshape(-1)
    prow = jnp.zeros((2, 128), F32).at[0, 8:16].set(ea).at[1, 8:16].set(dtb)
    pcol = jnp.stack([ea, dtb], axis=1)
    conv_w = jnp.zeros((8, qkv.shape[2]), F32).at[0:CONV_K].set(gdn_conv[ly].astype(F32))
    prep = _gdn_prepare(qkv, conv_w, ba, bat_chunks, prow, pcol, l)
    o_f, o_b = _gdn_scan(prep, l)

    wr = jnp.pad(w_router[ly].astype(F32), ((0, 0), (0, 128 - N_EXPERTS)))
    brt = jnp.pad(b_router[ly].astype(F32), (0, 128 - N_EXPERTS)).reshape(1, 128)
    x1, h2, logits = _merge(x, z, o_f, o_b, gate, br, mod3,
                            s5_w_glu[ly].astype(BF16), s5_b_glu[ly].astype(F32).reshape(1, -1),
                            gdn_norm[ly].astype(F32).reshape(1, -1),
                            w_branch_a[ly].astype(BF16), w_branch_b[ly].astype(BF16), w_out[ly].astype(BF16),
                            norm2[ly].astype(F32).reshape(1, d), wr, brt)

    n_tok = b * l
    weights, src_token, pos, tile_expert, n_used = _route(logits.reshape(n_tok, 128)[:, 0:N_EXPERTS], n_tok)
    h2f = h2.reshape(n_tok, d)
    xs = [h2f[piece] for piece in jnp.split(src_token, src_token.shape[0] // n_tok)]
    ys = _moe_experts(xs, tile_expert, n_used, w_gate_up[ly], b_gate_up[ly], w_down[ly], b_down[ly])
    yg = ys[pos.T.reshape(-1)].reshape(TOP_K, b, l, d)
    wt4 = jnp.pad(weights, ((0, 0), (0, 128 - TOP_K))).reshape(b, l, 128)
    return _final(x1, yg, wt4, mod3, norm_f)
```

```python
import functools

import numpy as np
import jax
import jax.numpy as jnp
from jax import lax
from jax.experimental import pallas as pl
from jax.experimental.pallas import tpu as pltpu

F32 = jnp.float32
BF16 = jnp.bfloat16
HIGHEST = lax.Precision.HIGHEST

EPS = 1e-6
GRID_W = 64

S5_GROUP = 16
S5_STATE = 64
S5_Q = 16

GDN_HEADS = 4
GDN_DK = 128
GDN_CHUNK = 64
CONV_K = 5

N_EXPERTS = 32
TOP_K = 4
SWIGLU_LIMIT = 7.0
SWIGLU_ALPHA = 1.702

ROW_TILE = 256
MERGE_TILE = 512
MOE_TILE = 512
VMEM_LIMIT = 56 * 1024 * 1024


def _cparams(sem):
    return pltpu.CompilerParams(dimension_semantics=sem, vmem_limit_bytes=VMEM_LIMIT)


def _sigmoid(v):
    return 1.0 / (1.0 + jnp.exp(-v))


def _softplus(v):
    return jnp.maximum(v, 0.0) + jnp.log(1.0 + jnp.exp(-jnp.abs(v)))


def _dot(a, b):
    return jnp.dot(a, b, preferred_element_type=F32)


def _dot_nt(a, b):
    return lax.dot_general(a, b, (((1,), (1,)), ((), ())), preferred_element_type=F32)


def _dot_tn(a, b):
    return lax.dot_general(a, b, (((0,), (0,)), ((), ())), preferred_element_type=F32)


def _mod_kernel(c_ref, w_ref, b_ref, o_ref):
    cc = c_ref[...]
    s = cc * _sigmoid(cc)
    o_ref[...] = jnp.dot(s, w_ref[...], precision=HIGHEST, preferred_element_type=F32) + b_ref[...]


def _modulation(cc, w_mod, b_mod):
    rows, d = cc.shape
    n = w_mod.shape[1]
    bn = d
    return pl.pallas_call(
        _mod_kernel,
        grid=(n // bn,),
        in_specs=[pl.BlockSpec((rows, d), lambda j: (0, 0)),
                  pl.BlockSpec((d, bn), lambda j: (0, j)),
                  pl.BlockSpec((1, bn), lambda j: (0, j))],
        out_specs=pl.BlockSpec((rows, bn), lambda j: (0, j)),
        out_shape=jax.ShapeDtypeStruct((rows, n), F32),
        compiler_params=_cparams(("arbitrary",)),
        name="adaln_mod",
    )(cc, w_mod, b_mod.reshape(1, n))


def _inproj_kernel(x_ref, c_ref, sc_ref, sh_ref, g_ref, w_ref, wlo_ref, wt_ref, wtlo_ref,
                   u_ref, qkv_ref, gate_ref, br_ref, ba_ref, bat_ref, *, n_lat_blocks):
    j = pl.program_id(1)
    is_ctx = j >= n_lat_blocks
    x = jnp.where(is_ctx, c_ref[0], x_ref[0])
    ms = jnp.mean(x * x, axis=-1, keepdims=True)
    y = x * lax.rsqrt(ms + EPS) * g_ref[...]
    h = y * (1.0 + sc_ref[0]) + sh_ref[0]
    h_hi = h.astype(BF16)
    h_lo = (h - h_hi.astype(F32)).astype(BF16)
    z = _dot(h_hi, w_ref[...])
    u_ref[0] = z[:, 0:512].astype(BF16)
    qkv_ref[0] = z[:, 512:2048].astype(BF16)
    gate_ref[0] = z[:, 2048:2560].astype(BF16)
    br_ref[0] = z[:, 2560:4608].astype(BF16)
    ba_ref[0] = z[:, 4608:4736] + z[:, 4736:4864] + _dot(h_lo, wlo_ref[...])
    rt = _dot_nt(wt_ref[...], h_hi)
    bat_ref[0] = rt[0:16] + rt[16:32] + _dot_nt(wtlo_ref[...], h_lo)


def _in_projection(x, ctx, mod3, norm1, w_main, w_lo, wt, wt_lo):
    b, l, d = x.shape
    lc = ctx.shape[1]
    tm = ROW_TILE
    nl, nc = l // tm, lc // tm
    nb = mod3.shape[0] // 6 - 1
    lt = l + lc
    ncols = w_main.shape[1]

    def x_map(i, j):
        return (i, jnp.minimum(j, nl - 1), 0)

    def c_map(i, j):
        return (i, jnp.maximum(j - nl, 0), 0)

    def mod_map(k):
        return lambda i, j: (jnp.where(j >= nl, nb, i) * 6 + k, 0, 0)

    def tok(width):
        return pl.BlockSpec((1, tm, width), lambda i, j: (i, j, 0))

    const2 = lambda i, j: (0, 0)
    outs = pl.pallas_call(
        functools.partial(_inproj_kernel, n_lat_blocks=nl),
        grid=(b, nl + nc),
        in_specs=[pl.BlockSpec((1, tm, d), x_map),
                  pl.BlockSpec((1, tm, d), c_map),
                  pl.BlockSpec((1, 1, d), mod_map(1)),
                  pl.BlockSpec((1, 1, d), mod_map(0)),
                  pl.BlockSpec((1, d), const2),
                  pl.BlockSpec((d, ncols), const2),
                  pl.BlockSpec((d, 128), const2),
                  pl.BlockSpec((32, d), const2),
                  pl.BlockSpec((16, d), const2)],
        out_specs=[tok(512), tok(1536), tok(512), tok(2048), tok(128),
                   pl.BlockSpec((1, 16, tm), lambda i, j: (i, 0, j))],
        out_shape=[jax.ShapeDtypeStruct((b, lt, 512), BF16),
                   jax.ShapeDtypeStruct((b, lt, 1536), BF16),
                   jax.ShapeDtypeStruct((b, lt, 512), BF16),
                   jax.ShapeDtypeStruct((b, lt, 2048), BF16),
                   jax.ShapeDtypeStruct((b, lt, 128), F32),
                   jax.ShapeDtypeStruct((b, 16, lt), F32)],
        compiler_params=_cparams(("parallel", "arbitrary")),
        name="in_projection",
    )(x, ctx, mod3, mod3, norm1.reshape(1, d), w_main, w_lo, wt, wt_lo)
    return outs


def _s5_matrices(lam_re, lam_im, log_step, b_re, b_im, c_re, c_im, d_skip):
    q = S5_Q
    lr = jnp.minimum(lam_re.astype(F32), -1e-4)
    li = lam_im.astype(F32)
    dt = jnp.exp(log_step.astype(F32))[..., None]
    pw = jnp.arange(q + 1, dtype=F32)[:, None, None, None]
    mag = jnp.exp(pw * (lr * dt))
    ang = pw * (li * dt)
    pr, pi = mag * jnp.cos(ang), mag * jnp.sin(ang)
    ar, ai = pr[1], pi[1]
    den = lr * lr + li * li
    fr = ((ar - 1.0) * lr + ai * li) / den
    fi = (ai * lr - (ar - 1.0) * li) / den
    br, bi = b_re.astype(F32), b_im.astype(F32)
    bbr = fr[..., None] * br - fi[..., None] * bi
    bbi = fr[..., None] * bi + fi[..., None] * br
    cr, ci = c_re.astype(F32), c_im.astype(F32)
    car = cr[None] * pr[:, :, :, None, :] - ci[None] * pi[:, :, :, None, :]
    cai = cr[None] * pi[:, :, :, None, :] + ci[None] * pr[:, :, :, None, :]
    bbr_t = jnp.swapaxes(bbr, -1, -2)[None, :, :, None]
    bbi_t = jnp.swapaxes(bbi, -1, -2)[None, :, :, None]
    kern = jnp.sum(car[..., None, :] * bbr_t - cai[..., None, :] * bbi_t, axis=-1)
    g = lr.shape[1]
    jj = np.arange(q)[:, None]
    ii = np.arange(q)[None, :]
    df = ii - jj
    mf = jnp.where((df >= 0)[:, :, None, None, None], kern[:q, 0][np.clip(df, 0, q - 1)], 0.0)
    mb = jnp.where((df <= 0)[:, :, None, None, None], kern[:q, 1][np.clip(-df, 0, q - 1)], 0.0)
    m = jnp.transpose(mf + mb, (2, 0, 4, 1, 3))
    skip = (jnp.eye(q, dtype=F32)[None, :, None, :, None]
            * jnp.eye(S5_GROUP, dtype=F32)[None, None, :, None, :]
            * d_skip.astype(F32).reshape(g, 1, S5_GROUP, 1, 1))
    m = (m + skip).reshape(g, q * S5_GROUP, q * S5_GROUP)

    def p_block(powers, d):
        ppr, ppi = pr[powers, d], pi[powers, d]
        re = ppr[..., None] * bbr[d][None] - ppi[..., None] * bbi[d][None]
        im = ppr[..., None] * bbi[d][None] + ppi[..., None] * bbr[d][None]
        to = lambda t: jnp.transpose(t, (1, 0, 3, 2)).reshape(g, q * S5_GROUP, S5_STATE)
        return to(re), to(im)

    pf_re, pf_im = p_block(np.arange(q - 1, -1, -1), 0)
    pb_re, pb_im = p_block(np.arange(q), 1)
    p = jnp.concatenate([pf_re, pb_re, pf_im, pb_im], axis=-1)

    def r_block(powers, d):
        to = lambda t: jnp.transpose(t, (1, 3, 0, 2)).reshape(g, S5_STATE, q * S5_GROUP)
        return to(car[powers, d]), to(-cai[powers, d])

    rf_re, rf_im = r_block(np.arange(1, q + 1), 0)
    rb_re, rb_im = r_block(np.arange(q, 0, -1), 1)
    r = jnp.concatenate([rf_re, rb_re, rf_im, rb_im], axis=1)
    a16 = jnp.stack([jnp.concatenate([pr[q, 0], pr[q, 1]], axis=-1),
                     jnp.concatenate([pi[q, 0], pi[q, 1]], axis=-1)], axis=1)
    return m.astype(BF16), p.astype(BF16), r.astype(BF16), a16


def _s5_kernel(u_ref, m_ref, p_ref, r_ref, a_ref, z_ref, v_ref, fre_ref, fim_ref, bre_ref, bim_ref,
               *, n_lat, n_ctx, nb):
    u = u_ref[0]
    v_ref[...] = _dot(u, p_ref[0])
    a = a_ref[0]
    a_re, a_im = a[0:1, :], a[1:2, :]
    fwd_lane = lax.broadcasted_iota(jnp.int32, (nb, 2 * S5_STATE), 1) < S5_STATE

    def scan(base, n, s0, store):
        def body(t, carry):
            s_re, s_im = carry
            rf = pl.multiple_of((base + t) * nb, nb)
            rb = pl.multiple_of((base + n - 1 - t) * nb, nb)
            if store:
                fre_ref[pl.ds(rf, nb), :] = s_re
                fim_ref[pl.ds(rf, nb), :] = s_im
                bre_ref[pl.ds(rb, nb), :] = s_re
                bim_ref[pl.ds(rb, nb), :] = s_im
            vf = v_ref[pl.ds(rf, nb), :]
            vb = v_ref[pl.ds(rb, nb), :]
            v_re = jnp.where(fwd_lane, vf[:, 0:128], vb[:, 0:128])
            v_im = jnp.where(fwd_lane, vf[:, 128:256], vb[:, 128:256])
            n_re = a_re * s_re - a_im * s_im + v_re
            n_im = a_re * s_im + a_im * s_re + v_im
            return n_re, n_im
        return lax.fori_loop(0, n, body, s0)

    zero = jnp.zeros((nb, 2 * S5_STATE), F32)
    s_ctx = scan(n_lat, n_ctx, (zero, zero), False)
    scan(0, n_lat, s_ctx, True)
    rows = n_lat * nb
    lane = lax.broadcasted_iota(jnp.int32, (rows, 2 * S5_STATE), 1) < S5_STATE
    st = jnp.concatenate([jnp.where(lane, fre_ref[...], bre_ref[...]),
                          jnp.where(lane, fim_ref[...], bim_ref[...])], axis=-1).astype(BF16)
    y = _dot(u[0:rows], m_ref[0]) + _dot(st, r_ref[0])
    z_ref[0] = jax.nn.gelu(y).astype(BF16)


def _s5_mixer(u_all, l, mats):
    m, p, r, a16 = mats
    b, lt, width = u_all.shape
    g = width // S5_GROUP
    q = S5_Q
    n_all, n_lat = lt // q, l // q
    qc = q * S5_GROUP
    ug = u_all.reshape(b, n_all, q, g, S5_GROUP)
    ug = jnp.transpose(ug, (3, 1, 0, 2, 4)).reshape(g, n_all * b, qc)
    rows = n_lat * b
    z = pl.pallas_call(
        functools.partial(_s5_kernel, n_lat=n_lat, n_ctx=n_all - n_lat, nb=b),
        grid=(g,),
        in_specs=[pl.BlockSpec((1, n_all * b, qc), lambda i: (i, 0, 0)),
                  pl.BlockSpec((1, qc, qc), lambda i: (i, 0, 0)),
                  pl.BlockSpec((1, qc, qc), lambda i: (i, 0, 0)),
                  pl.BlockSpec((1, qc, qc), lambda i: (i, 0, 0)),
                  pl.BlockSpec((1, 2, 2 * S5_STATE), lambda i: (i, 0, 0))],
        out_specs=pl.BlockSpec((1, rows, qc), lambda i: (i, 0, 0)),
        out_shape=jax.ShapeDtypeStruct((g, rows, qc), BF16),
        scratch_shapes=[pltpu.VMEM((n_all * b, qc), F32)] + [pltpu.VMEM((rows, 2 * S5_STATE), F32)] * 4,
        compiler_params=_cparams(("parallel",)),
        name="s5_scan",
    )(ug, m, p, r, a16)
    z = z.reshape(g, n_lat, b, q, S5_GROUP)
    return jnp.transpose(z, (2, 1, 3, 0, 4)).reshape(b, l, width)


def _inv_unit_triangular_many(mats):
    c = mats[0].shape[0]
    eye = (lax.broadcasted_iota(jnp.int32, (c, c), 0) == lax.broadcasted_iota(jnp.int32, (c, c), 1)).astype(F32)
    prods = [eye - a for a in mats]
    pows = [a.astype(BF16) for a in mats]
    for _ in range(int(np.log2(c)) - 1):
        pows = [_dot(a, a).astype(BF16) for a in pows]
        prods = [p + _dot(p.astype(BF16), a) for p, a in zip(prods, pows)]
    return prods


def _gdn_prep_kernel(qkv_ref, cw_ref, ba_ref, bat_ref, prow_ref, pcol_ref,
                     u_ref, wq_ref, qk_ref, kdt_ref, dl_ref, *, n_lat_blocks, ctx_len):
    j = pl.program_id(1)
    rows = qkv_ref.shape[1]
    c = GDN_CHUNK
    nchunk = rows // c
    seg = jnp.where(j >= n_lat_blocks, ctx_len, GRID_W)
    x = qkv_ref[0].astype(F32)
    pos = lax.broadcasted_iota(jnp.int32, (rows, 1), 0) & (seg - 1)
    acc = x * cw_ref[2:3, :]
    for s in (-2, -1, 1, 2):
        shifted = pltpu.roll(x, (-s) % rows, 0)
        ok = jnp.logical_and(pos + s >= 0, pos + s < seg)
        acc = acc + jnp.where(ok, shifted, 0.0) * cw_ref[2 + s:3 + s, :]
    act = acc * _sigmoid(acc)

    ii = lax.broadcasted_iota(jnp.int32, (c, c), 0)
    jj = lax.broadcasted_iota(jnp.int32, (c, c), 1)
    incl = (ii >= jj, ii <= jj)
    strict = (ii > jj, ii < jj)
    tri_low = incl[0].astype(F32)
    tri_up = incl[1].astype(F32)
    eye_bf = (ii == jj).astype(BF16)
    ones = jnp.ones((c, c), F32)
    lane = lax.broadcasted_iota(jnp.int32, (c, 128), 1)
    prow = prow_ref[...]
    pcol = pcol_ref[...]
    hp = dict(precision=HIGHEST, preferred_element_type=F32)

    gates = []
    for ci in range(nchunk):
        ba = ba_ref[0, ci * c:(ci + 1) * c, :]
        g_all = -prow[0:1, :] * _softplus(ba + prow[1:2, :])
        g_all = jnp.where(jnp.logical_and(lane >= 8, lane < 16), g_all, 0.0)
        bat = bat_ref[0, ci]
        g_row = -pcol[:, 0:1] * _softplus(bat[8:16, :] + pcol[:, 1:2])
        gates.append(dict(beta=_sigmoid(ba),
                          gc=(jnp.dot(tri_low, g_all, **hp), jnp.dot(tri_up, g_all, **hp)),
                          gtot=jnp.dot(ones, g_all, **hp),
                          gr=(jnp.dot(g_row, tri_up, **hp), jnp.dot(g_row, tri_low, **hp))))

    qn, kn, vv = [], [], []
    for h in range(GDN_HEADS):
        qh = act[:, h * 128:(h + 1) * 128]
        kh = act[:, 512 + h * 128:512 + (h + 1) * 128]
        qn.append(qh * lax.rsqrt(jnp.sum(qh * qh, axis=-1, keepdims=True) + EPS) * (GDN_DK ** -0.5))
        kn.append(kh * lax.rsqrt(jnp.sum(kh * kh, axis=-1, keepdims=True) + EPS))
        vv.append(act[:, 1024 + h * 128:1024 + (h + 1) * 128])

    pairs = [(ci, h) for ci in range(nchunk) for h in range(GDN_HEADS)]
    sl = lambda t, ci: t[ci * c:(ci + 1) * c]
    kb = {p: sl(kn[p[1]], p[0]).astype(BF16) for p in pairs}
    kk = {p: _dot_nt(kb[p], kb[p]) for p in pairs}
    qkm = {p: _dot_nt(sl(qn[p[1]], p[0]).astype(BF16), kb[p]) for p in pairs}

    probs = [(ci, h, d) for ci in range(nchunk) for h in range(GDN_HEADS) for d in range(2)]
    beta, gc, gt, decay, a_mats = {}, {}, {}, {}, []
    for (ci, h, d) in probs:
        col = d * GDN_HEADS + h
        gi = gates[ci]
        beta[ci, h, d] = gi["beta"][:, col:col + 1]
        gc[ci, h, d] = gi["gc"][d][:, 8 + col:9 + col]
        gt[ci, h, d] = gi["gtot"][:, 8 + col:9 + col]
        grow = gi["gr"][d][col:col + 1, :]
        decay[ci, h, d] = jnp.exp(jnp.where(incl[d], gc[ci, h, d] - grow, -jnp.inf))
        a_mats.append(jnp.where(strict[d], beta[ci, h, d] * kk[ci, h] * decay[ci, h, d], 0.0))
    tinv = _inv_unit_triangular_many(a_mats)

    eg, sols, kdts = {}, {}, {}
    for n, (ci, h, d) in enumerate(probs):
        p = (ci, h, d)
        eg[p] = jnp.exp(gc[p])
        kh = sl(kn[h], ci)
        rhs = jnp.concatenate([sl(vv[h], ci) * beta[p], kh * (beta[p] * eg[p])], axis=-1).astype(BF16)
        sols[p] = _dot(tinv[n].astype(BF16), rhs)
        kdts[p] = _dot_tn((kh * jnp.exp(gt[p] - gc[p])).astype(BF16), eye_bf)
    for (ci, h, d) in probs:
        p = (ci, h, d)
        u_ref[0, d, h, ci] = sols[p][:, 0:128]
        wq_ref[0, d, h, ci, 0:c, :] = sols[p][:, 128:256].astype(BF16)
        wq_ref[0, d, h, ci, c:2 * c, :] = (sl(qn[h], ci) * eg[p]).astype(BF16)
        qk_ref[0, d, h, ci] = (qkm[ci, h] * decay[p]).astype(BF16)
        kdt_ref[0, d, h, ci] = kdts[p].astype(BF16)
        dl_ref[0, d, h, ci] = jnp.broadcast_to(jnp.exp(gt[p][0:8, :]), (8, 128))


def _gdn_prepare(qkv, conv_w, ba, bat_chunks, prow, pcol, l):
    b, lt, width = qkv.shape
    tm = ROW_TILE
    c = GDN_CHUNK
    cpb = tm // c
    nblk = lt // tm
    nch = lt // c
    hd = (b, 2, GDN_HEADS, nch)

    def blk(shape_tail, dtype):
        return (pl.BlockSpec((1, 2, GDN_HEADS, cpb) + shape_tail, lambda i, j: (i, 0, 0, j, 0, 0)),
                jax.ShapeDtypeStruct(hd + shape_tail, dtype))

    specs = [blk((c, 128), F32), blk((2 * c, 128), BF16), blk((c, c), BF16), blk((GDN_DK, c), BF16),
             blk((8, 128), F32)]
    return pl.pallas_call(
        functools.partial(_gdn_prep_kernel, n_lat_blocks=l // tm, ctx_len=lt - l),
        grid=(b, nblk),
        in_specs=[pl.BlockSpec((1, tm, width), lambda i, j: (i, j, 0)),
                  pl.BlockSpec((8, width), lambda i, j: (0, 0)),
                  pl.BlockSpec((1, tm, 128), lambda i, j: (i, j, 0)),
                  pl.BlockSpec((1, cpb, 16, c), lambda i, j: (i, j, 0, 0)),
                  pl.BlockSpec((2, 128), lambda i, j: (0, 0)),
                  pl.BlockSpec((8, 2), lambda i, j: (0, 0))],
        out_specs=[s for s, _ in specs],
        out_shape=[o for _, o in specs],
        compiler_params=_cparams(("parallel", "parallel")),
        name="gdn_prepare",
    )(qkv, conv_w, ba, bat_chunks, prow, pcol)


SCAN_GROUP = 2


def _gdn_scan_kernel(uf, wqf, qkf, kdtf, dlf, ub, wqb, qkb, kdtb, dlb, of_ref, ob_ref, s_ref):
    t = pl.program_id(0)
    nb = s_ref.shape[0]
    c = GDN_CHUNK

    @pl.when(t == 0)
    def _():
        s_ref[...] = jnp.zeros_like(s_ref)

    refs = ((uf, wqf, qkf, kdtf, dlf, of_ref), (ub, wqb, qkb, kdtb, dlb, ob_ref))
    for b0 in range(0, nb, SCAN_GROUP):
        chains = [(bi, d, h) for bi in range(b0, min(b0 + SCAN_GROUP, nb)) for d in range(2)
                  for h in range(GDN_HEADS)]
        s = {k: s_ref[k[0], k[1], k[2]] for k in chains}
        sb = {k: s[k].astype(BF16) for k in chains}
        r = {k: _dot(refs[k[1]][1][k[0], 0, k[2], 0], sb[k]) for k in chains}
        vb = {k: (refs[k[1]][0][k[0], 0, k[2], 0] - r[k][0:c]).astype(BF16) for k in chains}
        o = {k: r[k][c:2 * c] + _dot(refs[k[1]][2][k[0], 0, k[2], 0], vb[k]) for k in chains}
        sn = {k: s[k] * refs[k[1]][4][k[0], 0, k[2], 0][0:1, :] + _dot(refs[k[1]][3][k[0], 0, k[2], 0], vb[k])
              for k in chains}
        for k in chains:
            s_ref[k[0], k[1], k[2]] = sn[k]
            refs[k[1]][5][k[0], :, k[2] * 128:(k[2] + 1) * 128] = o[k]


def _gdn_scan(prep, l):
    u = prep[0]
    b, _, heads, nch, c, _ = u.shape
    n_lat = l // c
    n_ctx = nch - n_lat
    lt = nch * c

    def fwd_chunk(t):
        return jnp.where(t < n_ctx, n_lat + t, t - n_ctx)

    def bwd_chunk(t):
        return nch - 1 - t

    def spec(arr, chunk_of, d):
        tail = arr.shape[4:]
        return pl.BlockSpec((b, 1, heads, 1) + tail, lambda t: (0, d, 0, chunk_of(t), 0, 0))

    ins = [spec(a, fwd_chunk, 0) for a in prep] + [spec(a, bwd_chunk, 1) for a in prep]
    width = heads * 128
    return pl.pallas_call(
        _gdn_scan_kernel,
        grid=(nch,),
        in_specs=ins,
        out_specs=[pl.BlockSpec((b, c, width), lambda t: (0, fwd_chunk(t), 0)),
                   pl.BlockSpec((b, c, width), lambda t: (0, bwd_chunk(t), 0))],
        out_shape=[jax.ShapeDtypeStruct((b, lt, width), F32)] * 2,
        scratch_shapes=[pltpu.VMEM((b, 2, heads, GDN_DK, 128), F32)],
        compiler_params=_cparams(("arbitrary",)),
        name="gdn_scan",
    )(*prep, *prep)


def _merge_kernel(x_ref, z_ref, of_ref, ob_ref, gate_ref, br_ref, gt1_ref, sc2_ref, sh2_ref,
                  wglu_ref, bglu_ref, gnorm_ref, wa_ref, wb_ref, wo_ref, n2_ref, wr_ref, brt_ref,
                  x1_ref, h2_ref, lg_ref):
    z = z_ref[0]
    zf = z.astype(F32)
    ya = zf * _sigmoid(_dot(z, wglu_ref[...]) + bglu_ref[...])
    o = of_ref[0] + ob_ref[0]
    gate = gate_ref[0].astype(F32)
    parts = []
    for h in range(GDN_HEADS):
        oh = o[:, h * 128:(h + 1) * 128]
        parts.append(oh * lax.rsqrt(jnp.mean(oh * oh, axis=-1, keepdims=True) + EPS) * gnorm_ref[...])
    yb = jnp.concatenate(parts, axis=-1) * (gate * _sigmoid(gate))
    br = br_ref[0].astype(F32)
    d = x_ref.shape[2]
    ga = _sigmoid(br[:, 0:d])
    gb = _sigmoid(br[:, d:2 * d])
    m = ga * _dot(ya.astype(BF16), wa_ref[...]) + gb * _dot(yb.astype(BF16), wb_ref[...])
    mix = _dot(m.astype(BF16), wo_ref[...])
    x1 = x_ref[0] + gt1_ref[0] * mix
    x1_ref[0] = x1
    y2 = x1 * lax.rsqrt(jnp.mean(x1 * x1, axis=-1, keepdims=True) + EPS) * n2_ref[...]
    h2 = y2 * (1.0 + sc2_ref[0]) + sh2_ref[0]
    h2b = h2.astype(BF16)
    h2_ref[0, 0] = h2b
    h2_ref[1, 0] = h2b
    lg_ref[0] = jnp.dot(h2, wr_ref[...], precision=HIGHEST, preferred_element_type=F32) + brt_ref[...]


def _merge(x, z, o_f, o_b, gate, br, mod3, wglu, bglu, gnorm, wa, wb, wo, norm2, wr, brt):
    b, l, d = x.shape
    tm = min(MERGE_TILE, l)
    tok = lambda width: pl.BlockSpec((1, tm, width), lambda i, j: (i, j, 0))
    modspec = lambda k: pl.BlockSpec((1, 1, d), lambda i, j: (i * 6 + k, 0, 0))
    full = lambda arr: pl.BlockSpec(arr.shape, lambda i, j: (0,) * arr.ndim)
    consts = [wglu, bglu, gnorm, wa, wb, wo, norm2, wr, brt]
    return pl.pallas_call(
        _merge_kernel,
        grid=(b, l // tm),
        in_specs=[tok(d), tok(512), tok(512), tok(512), tok(512), tok(2 * d),
                  modspec(2), modspec(4), modspec(3)] + [full(a) for a in consts],
        out_specs=[tok(d), pl.BlockSpec((2, 1, tm, d), lambda i, j: (0, i, j, 0)), tok(128)],
        out_shape=[jax.ShapeDtypeStruct((b, l, d), F32),
                   jax.ShapeDtypeStruct((2, b, l, d), BF16),
                   jax.ShapeDtypeStruct((b, l, 128), F32)],
        compiler_params=_cparams(("parallel", "parallel")),
        name="branch_merge",
    )(x, z, o_f, o_b, gate, br, mod3, mod3, mod3, *consts)


def _moe_kernel(te_ref, nu_ref, *refs, n_pieces, tiles_per_piece):
    x_refs = refs[:n_pieces]
    wgu_ref, bgu_ref, wd_ref, bd_ref, y_ref, wgu_bf, wd_bf = refs[n_pieces:]
    i = pl.program_id(0)
    changed = jnp.logical_or(i == 0, te_ref[i] != te_ref[jnp.maximum(i - 1, 0)])

    @pl.when(jnp.logical_and(changed, i < nu_ref[0]))
    def _():
        wgu_bf[...] = wgu_ref[0].astype(BF16)
        wd_bf[...] = wd_ref[0].astype(BF16)

    @pl.when(i < nu_ref[0])
    def _():
        de = wd_ref.shape[1]
        piece = i // tiles_per_piece
        x = x_refs[n_pieces - 1][...]
        for p in range(n_pieces - 2, -1, -1):
            x = jnp.where(piece == p, x_refs[p][...], x)
        gu = _dot(x, wgu_bf[...]) + bgu_ref[0]
        gate = jnp.minimum(gu[:, 0:de], SWIGLU_LIMIT)
        up = jnp.clip(gu[:, de:2 * de], -SWIGLU_LIMIT, SWIGLU_LIMIT)
        act = (up + 1.0) * gate * _sigmoid(gate * SWIGLU_ALPHA)
        y_ref[...] = (_dot(act.astype(BF16), wd_bf[...]) + bd_ref[0]).astype(y_ref.dtype)

    @pl.when(i >= nu_ref[0])
    def _():
        y_ref[...] = jnp.zeros_like(y_ref)


def _moe_experts(x_pieces, tile_expert, n_used, w_gate_up, b_gate_up, w_down, b_down):
    n_pieces = len(x_pieces)
    rows_per_piece, d = x_pieces[0].shape
    tm = MOE_TILE
    tpp = rows_per_piece // tm
    e, _, n2 = w_gate_up.shape
    de = w_down.shape[1]

    def piece_spec(p):
        return pl.BlockSpec((tm, d), lambda i, te, nu: (jnp.clip(i - p * tpp, 0, tpp - 1), 0))

    grid_spec = pltpu.PrefetchScalarGridSpec(
        num_scalar_prefetch=2,
        grid=(n_pieces * tpp,),
        in_specs=[piece_spec(p) for p in range(n_pieces)] + [
                  pl.BlockSpec((1, d, n2), lambda i, te, nu: (te[i], 0, 0)),
                  pl.BlockSpec((1, 1, n2), lambda i, te, nu: (te[i], 0, 0)),
                  pl.BlockSpec((1, de, d), lambda i, te, nu: (te[i], 0, 0)),
                  pl.BlockSpec((1, 1, d), lambda i, te, nu: (te[i], 0, 0))],
        out_specs=pl.BlockSpec((tm, d), lambda i, te, nu: (i, 0)),
        scratch_shapes=[pltpu.VMEM((d, n2), BF16), pltpu.VMEM((de, d), BF16)],
    )
    return pl.pallas_call(
        functools.partial(_moe_kernel, n_pieces=n_pieces, tiles_per_piece=tpp),
        grid_spec=grid_spec,
        out_shape=jax.ShapeDtypeStruct((n_pieces * rows_per_piece, d), BF16),
        compiler_params=_cparams(("arbitrary",)),
        name="moe_experts",
    )(tile_expert, n_used, *x_pieces, w_gate_up, b_gate_up.reshape(e, 1, n2), w_down, b_down.reshape(e, 1, d))


def _route(logits, n_tokens):
    tm = MOE_TILE
    top_val, top_idx = lax.top_k(logits, TOP_K)
    weights = jax.nn.softmax(top_val, axis=-1)
    flat_e = top_idx.reshape(-1).astype(jnp.int32)
    n_assign = flat_e.shape[0]
    iota = jnp.arange(n_assign, dtype=jnp.int32)
    sorted_e, order = lax.sort((flat_e, iota), num_keys=1, is_stable=True)
    counts = jnp.sum(jax.nn.one_hot(flat_e, N_EXPERTS, dtype=jnp.int32), axis=0)
    padded = ((counts + tm - 1) // tm) * tm
    pad_end = jnp.cumsum(padded)
    pad_start = pad_end - padded
    raw_start = jnp.cumsum(counts) - counts
    dest = pad_start[sorted_e] + (iota - raw_start[sorted_e])
    _, pos = lax.sort((order, dest), num_keys=1)
    pos = pos.reshape(n_tokens, TOP_K)
    n_rows = n_assign + N_EXPERTS * tm
    n_tiles = n_rows // tm
    n_used = (pad_end[-1] // tm).astype(jnp.int32)
    tile_start = jnp.arange(n_tiles, dtype=jnp.int32) * tm
    tile_expert = jnp.sum((tile_start[:, None] >= pad_end[None, :]).astype(jnp.int32), axis=1)
    last_e = jnp.sum((pad_end[-1] - 1 >= pad_end).astype(jnp.int32))
    tile_expert = jnp.minimum(tile_expert, last_e).astype(jnp.int32)
    off = jnp.arange(tm, dtype=jnp.int32)[None, :] + (tile_start - pad_start[tile_expert])[:, None]
    valid = off < counts[tile_expert][:, None]
    sidx = jnp.clip(raw_start[tile_expert][:, None] + off, 0, n_assign - 1)
    src_token = jnp.where(valid, (order // TOP_K)[sidx], 0).reshape(-1)
    return weights, src_token, pos, tile_expert, n_used.reshape(1)


def _final_kernel(x1_ref, y0_ref, y1_ref, y2_ref, y3_ref, wt_ref, gt2_ref, nf_ref, o_ref):
    wt = wt_ref[0]
    moe = (wt[:, 0:1] * y0_ref[0, 0].astype(F32) + wt[:, 1:2] * y1_ref[0, 0].astype(F32)
           + wt[:, 2:3] * y2_ref[0, 0].astype(F32) + wt[:, 3:4] * y3_ref[0, 0].astype(F32))
    x2 = x1_ref[0] + gt2_ref[0] * moe
    o_ref[0] = x2 * lax.rsqrt(jnp.mean(x2 * x2, axis=-1, keepdims=True) + EPS) * nf_ref[...]


def _final(x1, yg, wt, mod3, norm_f):
    b, l, d = x1.shape
    tm = ROW_TILE
    tok = lambda width: pl.BlockSpec((1, tm, width), lambda i, j: (i, j, 0))
    ysel = lambda k: pl.BlockSpec((1, 1, tm, d), lambda i, j: (k, i, j, 0))
    return pl.pallas_call(
        _final_kernel,
        grid=(b, l // tm),
        in_specs=[tok(d)] + [ysel(k) for k in range(TOP_K)] + [tok(128), pl.BlockSpec((1, 1, d), lambda i, j: (i * 6 + 5, 0, 0)),
                                 pl.BlockSpec((1, d), lambda i, j: (0, 0))],
        out_specs=tok(d),
        out_shape=jax.ShapeDtypeStruct((b, l, d), F32),
        compiler_params=_cparams(("parallel", "parallel")),
        name="combine_final_norm",
    )(x1, *([yg] * TOP_K), wt, mod3, norm_f.reshape(1, d))


def _split_hi_lo(w):
    hi = w.astype(BF16)
    lo = (w - hi.astype(F32)).astype(BF16)
    return hi, lo


def kernel(x, c, ctx, c_ctx, w_mod, b_mod, norm1, w_in, s5_lam_re, s5_lam_im, s5_log_step, s5_b_re, s5_b_im, s5_c_re, s5_c_im, s5_d, s5_w_glu, s5_b_glu, gdn_conv, gdn_a_log, gdn_dt_bias, gdn_norm, w_branch_a, w_branch_b, w_out, norm2, w_router, b_router, w_gate_up, b_gate_up, w_down, b_down, norm_f):
    b, l, d = x.shape
    lc = ctx.shape[1]
    depth = w_mod.shape[0]
    assert depth == 1, "single-layer block: the context stream has no consumer after the token mixer"
    assert l % ROW_TILE == 0 and lc % ROW_TILE == 0 and lc & (lc - 1) == 0 and b <= 8
    ly = 0

    cc = jnp.zeros((16, d), F32).at[0:b].set(c).at[b].set(c_ctx)
    mod = _modulation(cc, w_mod[ly], b_mod[ly])
    mod3 = mod[0:b + 1].reshape((b + 1) * 6, 1, d)

    wi = w_in[ly]
    o_u, o_qkv, o_gate, o_ba, o_br = 0, 512, 2048, 2560, 2576
    w_ba = wi[:, o_ba:o_br]
    ba_hi, ba_lo = _split_hi_lo(w_ba)
    pad = lambda t: jnp.pad(t, ((0, 0), (0, 128 - t.shape[1])))
    w_main = jnp.concatenate([wi[:, o_u:o_ba].astype(BF16), wi[:, o_br:].astype(BF16), pad(ba_hi), pad(ba_lo)], axis=1)
    wt = jnp.concatenate([ba_hi.T, ba_lo.T], axis=0)
    u_all, qkv, gate, br, ba, bat = _in_projection(x, ctx, mod3, norm1[ly], w_main, pad(ba_hi), wt, ba_hi.T)

    mats = _s5_matrices(s5_lam_re[ly], s5_lam_im[ly], s5_log_step[ly], s5_b_re[ly], s5_b_im[ly],
                        s5_c_re[ly], s5_c_im[ly], s5_d[ly])
    z = _s5_mixer(u_all, l, mats)

    lt = l + lc
    nch = lt // GDN_CHUNK
    bat_chunks = jnp.transpose(bat.reshape(b, 16, nch, GDN_CHUNK), (0, 2, 1, 3))
    ea = jnp.exp(gdn_a_log[ly].astype(F32)).reshape(-1)
    dtb = gdn_dt_bias[ly].astype(F32).reshape(-1)
    prow = jnp.zeros((2, 128), F32).at[0, 8:16].set(ea).at[1, 8:16].set(dtb)
    pcol = jnp.stack([ea, dtb], axis=1)
    conv_w = jnp.zeros((8, qkv.shape[2]), F32).at[0:CONV_K].set(gdn_conv[ly].astype(F32))
    prep = _gdn_prepare(qkv, conv_w, ba, bat_chunks, prow, pcol, l)
    o_f, o_b = _gdn_scan(prep, l)

    wr = jnp.pad(w_router[ly].astype(F32), ((0, 0), (0, 128 - N_EXPERTS)))
    brt = jnp.pad(b_router[ly].astype(F32), (0, 128 - N_EXPERTS)).reshape(1, 128)
    x1, h2, logits = _merge(x, z, o_f, o_b, gate, br, mod3,
                            s5_w_glu[ly].astype(BF16), s5_b_glu[ly].astype(F32).reshape(1, -1),
                            gdn_norm[ly].astype(F32).reshape(1, -1),
                            w_branch_a[ly].astype(BF16), w_branch_b[ly].astype(BF16), w_out[ly].astype(BF16),
                            norm2[ly].astype(F32).reshape(1, d), wr, brt)

    n_tok = b * l
    weights, src_token, pos, tile_expert, n_used = _route(logits.reshape(n_tok, 128)[:, 0:N_EXPERTS], n_tok)
    h2f = h2.reshape(2 * n_tok, d)
    xs = [h2f[piece] for piece in jnp.split(src_token, src_token.shape[0] // n_tok)]
    ys = _moe_experts(xs, tile_expert, n_used, w_gate_up[ly], b_gate_up[ly], w_down[ly], b_down[ly])
    yg = ys[pos.T.reshape(-1)].reshape(TOP_K, b, l, d)
    wt4 = jnp.pad(weights, ((0, 0), (0, 128 - TOP_K))).reshape(b, l, 128)
    return _final(x1, yg, wt4, mod3, norm_f)
```

```python
import functools

import numpy as np
import jax
import jax.numpy as jnp
from jax import lax
from jax.experimental import pallas as pl
from jax.experimental.pallas import tpu as pltpu

F32 = jnp.float32
BF16 = jnp.bfloat16
HIGHEST = lax.Precision.HIGHEST

EPS = 1e-6
GRID_W = 64

S5_GROUP = 16
S5_STATE = 64
S5_Q = 16

GDN_HEADS = 4
GDN_DK = 128
GDN_CHUNK = 64
CONV_K = 5

N_EXPERTS = 32
TOP_K = 4
SWIGLU_LIMIT = 7.0
SWIGLU_ALPHA = 1.702

ROW_TILE = 256
MERGE_TILE = 512
MOE_TILE = 512
VMEM_LIMIT = 56 * 1024 * 1024


def _cparams(sem):
    return pltpu.CompilerParams(dimension_semantics=sem, vmem_limit_bytes=VMEM_LIMIT)


def _sigmoid(v):
    return 1.0 / (1.0 + jnp.exp(-v))


def _softplus(v):
    return jnp.maximum(v, 0.0) + jnp.log(1.0 + jnp.exp(-jnp.abs(v)))


def _dot(a, b):
    return jnp.dot(a, b, preferred_element_type=F32)


def _dot_nt(a, b):
    return lax.dot_general(a, b, (((1,), (1,)), ((), ())), preferred_element_type=F32)


def _dot_tn(a, b):
    return lax.dot_general(a, b, (((0,), (0,)), ((), ())), preferred_element_type=F32)


def _mod_kernel(c_ref, w_ref, b_ref, o_ref):
    cc = c_ref[...]
    s = cc * _sigmoid(cc)
    o_ref[...] = jnp.dot(s, w_ref[...], precision=HIGHEST, preferred_element_type=F32) + b_ref[...]


def _modulation(cc, w_mod, b_mod):
    rows, d = cc.shape
    n = w_mod.shape[1]
    bn = d
    return pl.pallas_call(
        _mod_kernel,
        grid=(n // bn,),
        in_specs=[pl.BlockSpec((rows, d), lambda j: (0, 0)),
                  pl.BlockSpec((d, bn), lambda j: (0, j)),
                  pl.BlockSpec((1, bn), lambda j: (0, j))],
        out_specs=pl.BlockSpec((rows, bn), lambda j: (0, j)),
        out_shape=jax.ShapeDtypeStruct((rows, n), F32),
        compiler_params=_cparams(("arbitrary",)),
        name="adaln_mod",
    )(cc, w_mod, b_mod.reshape(1, n))


def _inproj_kernel(x_ref, c_ref, sc_ref, sh_ref, g_ref, w_ref, wlo_ref, wt_ref, wtlo_ref,
                   u_ref, qkv_ref, gate_ref, br_ref, ba_ref, bat_ref, *, n_lat_blocks):
    j = pl.program_id(1)
    is_ctx = j >= n_lat_blocks
    x = jnp.where(is_ctx, c_ref[0], x_ref[0])
    ms = jnp.mean(x * x, axis=-1, keepdims=True)
    y = x * lax.rsqrt(ms + EPS) * g_ref[...]
    h = y * (1.0 + sc_ref[0]) + sh_ref[0]
    h_hi = h.astype(BF16)
    h_lo = (h - h_hi.astype(F32)).astype(BF16)
    z = _dot(h_hi, w_ref[...])
    u_ref[0] = z[:, 0:512].astype(BF16)
    qkv_ref[0] = z[:, 512:2048].astype(BF16)
    gate_ref[0] = z[:, 2048:2560].astype(BF16)
    br_ref[0] = z[:, 2560:4608].astype(BF16)
    ba_ref[0] = z[:, 4608:4736] + z[:, 4736:4864] + _dot(h_lo, wlo_ref[...])
    rt = _dot_nt(wt_ref[...], h_hi)
    bat_ref[0] = rt[0:16] + rt[16:32] + _dot_nt(wtlo_ref[...], h_lo)


def _in_projection(x, ctx, mod3, norm1, w_main, w_lo, wt, wt_lo):
    b, l, d = x.shape
    lc = ctx.shape[1]
    tm = ROW_TILE
    nl, nc = l // tm, lc // tm
    nb = mod3.shape[0] // 6 - 1
    lt = l + lc
    ncols = w_main.shape[1]

    def x_map(i, j):
        return (i, jnp.minimum(j, nl - 1), 0)

    def c_map(i, j):
        return (i, jnp.maximum(j - nl, 0), 0)

    def mod_map(k):
        return lambda i, j: (jnp.where(j >= nl, nb, i) * 6 + k, 0, 0)

    def tok(width):
        return pl.BlockSpec((1, tm, width), lambda i, j: (i, j, 0))

    const2 = lambda i, j: (0, 0)
    outs = pl.pallas_call(
        functools.partial(_inproj_kernel, n_lat_blocks=nl),
        grid=(b, nl + nc),
        in_specs=[pl.BlockSpec((1, tm, d), x_map),
                  pl.BlockSpec((1, tm, d), c_map),
                  pl.BlockSpec((1, 1, d), mod_map(1)),
                  pl.BlockSpec((1, 1, d), mod_map(0)),
                  pl.BlockSpec((1, d), const2),
                  pl.BlockSpec((d, ncols), const2),
                  pl.BlockSpec((d, 128), const2),
                  pl.BlockSpec((32, d), const2),
                  pl.BlockSpec((16, d), const2)],
        out_specs=[tok(512), tok(1536), tok(512), tok(2048), tok(128),
                   pl.BlockSpec((1, 16, tm), lambda i, j: (i, 0, j))],
        out_shape=[jax.ShapeDtypeStruct((b, lt, 512), BF16),
                   jax.ShapeDtypeStruct((b, lt, 1536), BF16),
                   jax.ShapeDtypeStruct((b, lt, 512), BF16),
                   jax.ShapeDtypeStruct((b, lt, 2048), BF16),
                   jax.ShapeDtypeStruct((b, lt, 128), F32),
                   jax.ShapeDtypeStruct((b, 16, lt), F32)],
        compiler_params=_cparams(("parallel", "arbitrary")),
        name="in_projection",
    )(x, ctx, mod3, mod3, norm1.reshape(1, d), w_main, w_lo, wt, wt_lo)
    return outs


def _s5_matrices(lam_re, lam_im, log_step, b_re, b_im, c_re, c_im, d_skip):
    q = S5_Q
    lr = jnp.minimum(lam_re.astype(F32), -1e-4)
    li = lam_im.astype(F32)
    dt = jnp.exp(log_step.astype(F32))[..., None]
    pw = jnp.arange(q + 1, dtype=F32)[:, None, None, None]
    mag = jnp.exp(pw * (lr * dt))
    ang = pw * (li * dt)
    pr, pi = mag * jnp.cos(ang), mag * jnp.sin(ang)
    ar, ai = pr[1], pi[1]
    den = lr * lr + li * li
    fr = ((ar - 1.0) * lr + ai * li) / den
    fi = (ai * lr - (ar - 1.0) * li) / den
    br, bi = b_re.astype(F32), b_im.astype(F32)
    bbr = fr[..., None] * br - fi[..., None] * bi
    bbi = fr[..., None] * bi + fi[..., None] * br
    cr, ci = c_re.astype(F32), c_im.astype(F32)
    car = cr[None] * pr[:, :, :, None, :] - ci[None] * pi[:, :, :, None, :]
    cai = cr[None] * pi[:, :, :, None, :] + ci[None] * pr[:, :, :, None, :]
    bbr_t = jnp.swapaxes(bbr, -1, -2)[None, :, :, None]
    bbi_t = jnp.swapaxes(bbi, -1, -2)[None, :, :, None]
    kern = jnp.sum(car[..., None, :] * bbr_t - cai[..., None, :] * bbi_t, axis=-1)
    g = lr.shape[1]
    jj = np.arange(q)[:, None]
    ii = np.arange(q)[None, :]
    df = ii - jj
    mf = jnp.where((df >= 0)[:, :, None, None, None], kern[:q, 0][np.clip(df, 0, q - 1)], 0.0)
    mb = jnp.where((df <= 0)[:, :, None, None, None], kern[:q, 1][np.clip(-df, 0, q - 1)], 0.0)
    m = jnp.transpose(mf + mb, (2, 0, 4, 1, 3))
    skip = (jnp.eye(q, dtype=F32)[None, :, None, :, None]
            * jnp.eye(S5_GROUP, dtype=F32)[None, None, :, None, :]
            * d_skip.astype(F32).reshape(g, 1, S5_GROUP, 1, 1))
    m = (m + skip).reshape(g, q * S5_GROUP, q * S5_GROUP)

    def p_block(powers, d):
        ppr, ppi = pr[powers, d], pi[powers, d]
        re = ppr[..., None] * bbr[d][None] - ppi[..., None] * bbi[d][None]
        im = ppr[..., None] * bbi[d][None] + ppi[..., None] * bbr[d][None]
        to = lambda t: jnp.transpose(t, (1, 0, 3, 2)).reshape(g, q * S5_GROUP, S5_STATE)
        return to(re), to(im)

    pf_re, pf_im = p_block(np.arange(q - 1, -1, -1), 0)
    pb_re, pb_im = p_block(np.arange(q), 1)
    p = jnp.concatenate([pf_re, pb_re, pf_im, pb_im], axis=-1)

    def r_block(powers, d):
        to = lambda t: jnp.transpose(t, (1, 3, 0, 2)).reshape(g, S5_STATE, q * S5_GROUP)
        return to(car[powers, d]), to(-cai[powers, d])

    rf_re, rf_im = r_block(np.arange(1, q + 1), 0)
    rb_re, rb_im = r_block(np.arange(q, 0, -1), 1)
    r = jnp.concatenate([rf_re, rb_re, rf_im, rb_im], axis=1)
    a16 = jnp.stack([jnp.concatenate([pr[q, 0], pr[q, 1]], axis=-1),
                     jnp.concatenate([pi[q, 0], pi[q, 1]], axis=-1)], axis=1)
    return m.astype(BF16), p.astype(BF16), r.astype(BF16), a16


def _s5_kernel(u_ref, m_ref, p_ref, r_ref, a_ref, z_ref, v_ref, fre_ref, fim_ref, bre_ref, bim_ref,
               *, n_lat, n_ctx, nb):
    u = u_ref[0]
    v_ref[...] = _dot(u, p_ref[0])
    a = a_ref[0]
    a_re, a_im = a[0:1, :], a[1:2, :]
    fwd_lane = lax.broadcasted_iota(jnp.int32, (nb, 2 * S5_STATE), 1) < S5_STATE

    def scan(base, n, s0, store):
        def body(t, carry):
            s_re, s_im = carry
            rf = pl.multiple_of((base + t) * nb, nb)
            rb = pl.multiple_of((base + n - 1 - t) * nb, nb)
            if store:
                fre_ref[pl.ds(rf, nb), :] = s_re
                fim_ref[pl.ds(rf, nb), :] = s_im
                bre_ref[pl.ds(rb, nb), :] = s_re
                bim_ref[pl.ds(rb, nb), :] = s_im
            vf = v_ref[pl.ds(rf, nb), :]
            vb = v_ref[pl.ds(rb, nb), :]
            v_re = jnp.where(fwd_lane, vf[:, 0:128], vb[:, 0:128])
            v_im = jnp.where(fwd_lane, vf[:, 128:256], vb[:, 128:256])
            n_re = a_re * s_re - a_im * s_im + v_re
            n_im = a_re * s_im + a_im * s_re + v_im
            return n_re, n_im
        return lax.fori_loop(0, n, body, s0)

    zero = jnp.zeros((nb, 2 * S5_STATE), F32)
    s_ctx = scan(n_lat, n_ctx, (zero, zero), False)
    scan(0, n_lat, s_ctx, True)
    rows = n_lat * nb
    lane = lax.broadcasted_iota(jnp.int32, (rows, 2 * S5_STATE), 1) < S5_STATE
    st = jnp.concatenate([jnp.where(lane, fre_ref[...], bre_ref[...]),
                          jnp.where(lane, fim_ref[...], bim_ref[...])], axis=-1).astype(BF16)
    y = _dot(u[0:rows], m_ref[0]) + _dot(st, r_ref[0])
    z_ref[0] = jax.nn.gelu(y).astype(BF16)


def _s5_mixer(u_all, l, mats):
    m, p, r, a16 = mats
    b, lt, width = u_all.shape
    g = width // S5_GROUP
    q = S5_Q
    n_all, n_lat = lt // q, l // q
    qc = q * S5_GROUP
    ug = u_all.reshape(b, n_all, q, g, S5_GROUP)
    ug = jnp.transpose(ug, (3, 1, 0, 2, 4)).reshape(g, n_all * b, qc)
    rows = n_lat * b
    z = pl.pallas_call(
        functools.partial(_s5_kernel, n_lat=n_lat, n_ctx=n_all - n_lat, nb=b),
        grid=(g,),
        in_specs=[pl.BlockSpec((1, n_all * b, qc), lambda i: (i, 0, 0)),
                  pl.BlockSpec((1, qc, qc), lambda i: (i, 0, 0)),
                  pl.BlockSpec((1, qc, qc), lambda i: (i, 0, 0)),
                  pl.BlockSpec((1, qc, qc), lambda i: (i, 0, 0)),
                  pl.BlockSpec((1, 2, 2 * S5_STATE), lambda i: (i, 0, 0))],
        out_specs=pl.BlockSpec((1, rows, qc), lambda i: (i, 0, 0)),
        out_shape=jax.ShapeDtypeStruct((g, rows, qc), BF16),
        scratch_shapes=[pltpu.VMEM((n_all * b, qc), F32)] + [pltpu.VMEM((rows, 2 * S5_STATE), F32)] * 4,
        compiler_params=_cparams(("parallel",)),
        name="s5_scan",
    )(ug, m, p, r, a16)
    z = z.reshape(g, n_lat, b, q, S5_GROUP)
    return jnp.transpose(z, (2, 1, 3, 0, 4)).reshape(b, l, width)


def _inv_unit_triangular_many(mats):
    c = mats[0].shape[0]
    eye = (lax.broadcasted_iota(jnp.int32, (c, c), 0) == lax.broadcasted_iota(jnp.int32, (c, c), 1)).astype(F32)
    prods = [eye - a for a in mats]
    pows = [a.astype(BF16) for a in mats]
    for _ in range(int(np.log2(c)) - 1):
        pows = [_dot(a, a).astype(BF16) for a in pows]
        prods = [p + _dot(p.astype(BF16), a) for p, a in zip(prods, pows)]
    return prods


def _gdn_prep_kernel(qkv_ref, cw_ref, ba_ref, bat_ref, prow_ref, pcol_ref,
                     u_ref, wq_ref, qk_ref, kdt_ref, dl_ref, *, n_lat_blocks, ctx_len):
    j = pl.program_id(1)
    rows = qkv_ref.shape[1]
    c = GDN_CHUNK
    nchunk = rows // c
    seg = jnp.where(j >= n_lat_blocks, ctx_len, GRID_W)
    x = qkv_ref[0].astype(F32)
    pos = lax.broadcasted_iota(jnp.int32, (rows, 1), 0) & (seg - 1)
    acc = x * cw_ref[2:3, :]
    for s in (-2, -1, 1, 2):
        shifted = pltpu.roll(x, (-s) % rows, 0)
        ok = jnp.logical_and(pos + s >= 0, pos + s < seg)
        acc = acc + jnp.where(ok, shifted, 0.0) * cw_ref[2 + s:3 + s, :]
    act = acc * _sigmoid(acc)

    ii = lax.broadcasted_iota(jnp.int32, (c, c), 0)
    jj = lax.broadcasted_iota(jnp.int32, (c, c), 1)
    incl = (ii >= jj, ii <= jj)
    strict = (ii > jj, ii < jj)
    tri_low = incl[0].astype(F32)
    tri_up = incl[1].astype(F32)
    eye_bf = (ii == jj).astype(BF16)
    ones = jnp.ones((c, c), F32)
    lane = lax.broadcasted_iota(jnp.int32, (c, 128), 1)
    prow = prow_ref[...]
    pcol = pcol_ref[...]
    hp = dict(precision=HIGHEST, preferred_element_type=F32)

    gates = []
    for ci in range(nchunk):
        ba = ba_ref[0, ci * c:(ci + 1) * c, :]
        g_all = -prow[0:1, :] * _softplus(ba + prow[1:2, :])
        g_all = jnp.where(jnp.logical_and(lane >= 8, lane < 16), g_all, 0.0)
        bat = bat_ref[0, ci]
        g_row = -pcol[:, 0:1] * _softplus(bat[8:16, :] + pcol[:, 1:2])
        gates.append(dict(beta=_sigmoid(ba),
                          gc=(jnp.dot(tri_low, g_all, **hp), jnp.dot(tri_up, g_all, **hp)),
                          gtot=jnp.dot(ones, g_all, **hp),
                          gr=(jnp.dot(g_row, tri_up, **hp), jnp.dot(g_row, tri_low, **hp))))

    qn, kn, vv = [], [], []
    for h in range(GDN_HEADS):
        qh = act[:, h * 128:(h + 1) * 128]
        kh = act[:, 512 + h * 128:512 + (h + 1) * 128]
        qn.append(qh * lax.rsqrt(jnp.sum(qh * qh, axis=-1, keepdims=True) + EPS) * (GDN_DK ** -0.5))
        kn.append(kh * lax.rsqrt(jnp.sum(kh * kh, axis=-1, keepdims=True) + EPS))
        vv.append(act[:, 1024 + h * 128:1024 + (h + 1) * 128])

    pairs = [(ci, h) for ci in range(nchunk) for h in range(GDN_HEADS)]
    sl = lambda t, ci: t[ci * c:(ci + 1) * c]
    kb = {p: sl(kn[p[1]], p[0]).astype(BF16) for p in pairs}
    kk = {p: _dot_nt(kb[p], kb[p]) for p in pairs}
    qkm = {p: _dot_nt(sl(qn[p[1]], p[0]).astype(BF16), kb[p]) for p in pairs}

    probs = [(ci, h, d) for ci in range(nchunk) for h in range(GDN_HEADS) for d in range(2)]
    beta, gc, gt, decay, a_mats = {}, {}, {}, {}, []
    for (ci, h, d) in probs:
        col = d * GDN_HEADS + h
        gi = gates[ci]
        beta[ci, h, d] = gi["beta"][:, col:col + 1]
        gc[ci, h, d] = gi["gc"][d][:, 8 + col:9 + col]
        gt[ci, h, d] = gi["gtot"][:, 8 + col:9 + col]
        grow = gi["gr"][d][col:col + 1, :]
        decay[ci, h, d] = jnp.exp(jnp.where(incl[d], gc[ci, h, d] - grow, -jnp.inf))
        a_mats.append(jnp.where(strict[d], beta[ci, h, d] * kk[ci, h] * decay[ci, h, d], 0.0))
    tinv = _inv_unit_triangular_many(a_mats)

    eg, sols, kdts = {}, {}, {}
    for n, (ci, h, d) in enumerate(probs):
        p = (ci, h, d)
        eg[p] = jnp.exp(gc[p])
        kh = sl(kn[h], ci)
        rhs = jnp.concatenate([sl(vv[h], ci) * beta[p], kh * (beta[p] * eg[p])], axis=-1).astype(BF16)
        sols[p] = _dot(tinv[n].astype(BF16), rhs)
        kdts[p] = _dot_tn((kh * jnp.exp(gt[p] - gc[p])).astype(BF16), eye_bf)
    for (ci, h, d) in probs:
        p = (ci, h, d)
        u_ref[0, d, h, ci] = sols[p][:, 0:128]
        wq_ref[0, d, h, ci, 0:c, :] = sols[p][:, 128:256].astype(BF16)
        wq_ref[0, d, h, ci, c:2 * c, :] = (sl(qn[h], ci) * eg[p]).astype(BF16)
        qk_ref[0, d, h, ci] = (qkm[ci, h] * decay[p]).astype(BF16)
        kdt_ref[0, d, h, ci] = kdts[p].astype(BF16)
        dl_ref[0, d, h, ci] = jnp.broadcast_to(jnp.exp(gt[p][0:8, :]), (8, 128))


def _gdn_prepare(qkv, conv_w, ba, bat_chunks, prow, pcol, l):
    b, lt, width = qkv.shape
    tm = ROW_TILE
    c = GDN_CHUNK
    cpb = tm // c
    nblk = lt // tm
    nch = lt // c
    hd = (b, 2, GDN_HEADS, nch)

    def blk(shape_tail, dtype):
        return (pl.BlockSpec((1, 2, GDN_HEADS, cpb) + shape_tail, lambda i, j: (i, 0, 0, j, 0, 0)),
                jax.ShapeDtypeStruct(hd + shape_tail, dtype))

    specs = [blk((c, 128), F32), blk((2 * c, 128), BF16), blk((c, c), BF16), blk((GDN_DK, c), BF16),
             blk((8, 128), F32)]
    return pl.pallas_call(
        functools.partial(_gdn_prep_kernel, n_lat_blocks=l // tm, ctx_len=lt - l),
        grid=(b, nblk),
        in_specs=[pl.BlockSpec((1, tm, width), lambda i, j: (i, j, 0)),
                  pl.BlockSpec((8, width), lambda i, j: (0, 0)),
                  pl.BlockSpec((1, tm, 128), lambda i, j: (i, j, 0)),
                  pl.BlockSpec((1, cpb, 16, c), lambda i, j: (i, j, 0, 0)),
                  pl.BlockSpec((2, 128), lambda i, j: (0, 0)),
                  pl.BlockSpec((8, 2), lambda i, j: (0, 0))],
        out_specs=[s for s, _ in specs],
        out_shape=[o for _, o in specs],
        compiler_params=_cparams(("parallel", "parallel")),
        name="gdn_prepare",
    )(qkv, conv_w, ba, bat_chunks, prow, pcol)


SCAN_GROUP = 2


def _gdn_scan_kernel(uf, wqf, qkf, kdtf, dlf, ub, wqb, qkb, kdtb, dlb, of_ref, ob_ref, s_ref):
    t = pl.program_id(0)
    nb = s_ref.shape[0]
    c = GDN_CHUNK

    @pl.when(t == 0)
    def _():
        s_ref[...] = jnp.zeros_like(s_ref)

    refs = ((uf, wqf, qkf, kdtf, dlf, of_ref), (ub, wqb, qkb, kdtb, dlb, ob_ref))
    for b0 in range(0, nb, SCAN_GROUP):
        chains = [(bi, d, h) for bi in range(b0, min(b0 + SCAN_GROUP, nb)) for d in range(2)
                  for h in range(GDN_HEADS)]
        s = {k: s_ref[k[0], k[1], k[2]] for k in chains}
        sb = {k: s[k].astype(BF16) for k in chains}
        r = {k: _dot(refs[k[1]][1][k[0], 0, k[2], 0], sb[k]) for k in chains}
        vb = {k: (refs[k[1]][0][k[0], 0, k[2], 0] - r[k][0:c]).astype(BF16) for k in chains}
        o = {k: r[k][c:2 * c] + _dot(refs[k[1]][2][k[0], 0, k[2], 0], vb[k]) for k in chains}
        sn = {k: s[k] * refs[k[1]][4][k[0], 0, k[2], 0][0:1, :] + _dot(refs[k[1]][3][k[0], 0, k[2], 0], vb[k])
              for k in chains}
        for k in chains:
            s_ref[k[0], k[1], k[2]] = sn[k]
            refs[k[1]][5][k[0], :, k[2] * 128:(k[2] + 1) * 128] = o[k]


def _gdn_scan(prep, l):
    u = prep[0]
    b, _, heads, nch, c, _ = u.shape
    n_lat = l // c
    n_ctx = nch - n_lat
    lt = nch * c

    def fwd_chunk(t):
        return jnp.where(t < n_ctx, n_lat + t, t - n_ctx)

    def bwd_chunk(t):
        return nch - 1 - t

    def spec(arr, chunk_of, d):
        tail = arr.shape[4:]
        return pl.BlockSpec((b, 1, heads, 1) + tail, lambda t: (0, d, 0, chunk_of(t), 0, 0))

    ins = [spec(a, fwd_chunk, 0) for a in prep] + [spec(a, bwd_chunk, 1) for a in prep]
    width = heads * 128
    return pl.pallas_call(
        _gdn_scan_kernel,
        grid=(nch,),
        in_specs=ins,
        out_specs=[pl.BlockSpec((b, c, width), lambda t: (0, fwd_chunk(t), 0)),
                   pl.BlockSpec((b, c, width), lambda t: (0, bwd_chunk(t), 0))],
        out_shape=[jax.ShapeDtypeStruct((b, lt, width), F32)] * 2,
        scratch_shapes=[pltpu.VMEM((b, 2, heads, GDN_DK, 128), F32)],
        compiler_params=_cparams(("arbitrary",)),
        name="gdn_scan",
    )(*prep, *prep)


def _merge_kernel(x_ref, z_ref, of_ref, ob_ref, gate_ref, br_ref, gt1_ref, sc2_ref, sh2_ref,
                  wglu_ref, bglu_ref, gnorm_ref, wa_ref, wb_ref, wo_ref, n2_ref, wr_ref, brt_ref,
                  x1_ref, h2_ref, lg_ref):
    z = z_ref[0]
    zf = z.astype(F32)
    ya = zf * _sigmoid(_dot(z, wglu_ref[...]) + bglu_ref[...])
    o = of_ref[0] + ob_ref[0]
    gate = gate_ref[0].astype(F32)
    parts = []
    for h in range(GDN_HEADS):
        oh = o[:, h * 128:(h + 1) * 128]
        parts.append(oh * lax.rsqrt(jnp.mean(oh * oh, axis=-1, keepdims=True) + EPS) * gnorm_ref[...])
    yb = jnp.concatenate(parts, axis=-1) * (gate * _sigmoid(gate))
    br = br_ref[0].astype(F32)
    d = x_ref.shape[2]
    ga = _sigmoid(br[:, 0:d])
    gb = _sigmoid(br[:, d:2 * d])
    m = ga * _dot(ya.astype(BF16), wa_ref[...]) + gb * _dot(yb.astype(BF16), wb_ref[...])
    mix = _dot(m.astype(BF16), wo_ref[...])
    x1 = x_ref[0] + gt1_ref[0] * mix
    x1_ref[0] = x1
    y2 = x1 * lax.rsqrt(jnp.mean(x1 * x1, axis=-1, keepdims=True) + EPS) * n2_ref[...]
    h2 = y2 * (1.0 + sc2_ref[0]) + sh2_ref[0]
    h2b = h2.astype(BF16)
    h2_ref[0, 0] = h2b
    h2_ref[1, 0] = h2b
    lg_ref[0] = jnp.dot(h2, wr_ref[...], precision=HIGHEST, preferred_element_type=F32) + brt_ref[...]


def _merge(x, z, o_f, o_b, gate, br, mod3, wglu, bglu, gnorm, wa, wb, wo, norm2, wr, brt):
    b, l, d = x.shape
    tm = min(MERGE_TILE, l)
    tok = lambda width: pl.BlockSpec((1, tm, width), lambda i, j: (i, j, 0))
    modspec = lambda k: pl.BlockSpec((1, 1, d), lambda i, j: (i * 6 + k, 0, 0))
    full = lambda arr: pl.BlockSpec(arr.shape, lambda i, j: (0,) * arr.ndim)
    consts = [wglu, bglu, gnorm, wa, wb, wo, norm2, wr, brt]
    return pl.pallas_call(
        _merge_kernel,
        grid=(b, l // tm),
        in_specs=[tok(d), tok(512), tok(512), tok(512), tok(512), tok(2 * d),
                  modspec(2), modspec(4), modspec(3)] + [full(a) for a in consts],
        out_specs=[tok(d), pl.BlockSpec((2, 1, tm, d), lambda i, j: (0, i, j, 0)), tok(128)],
        out_shape=[jax.ShapeDtypeStruct((b, l, d), F32),
                   jax.ShapeDtypeStruct((2, b, l, d), BF16),
                   jax.ShapeDtypeStruct((b, l, 128), F32)],
        compiler_params=_cparams(("parallel", "parallel")),
        name="branch_merge",
    )(x, z, o_f, o_b, gate, br, mod3, mod3, mod3, *consts)


def _moe_kernel(te_ref, nu_ref, *refs, n_pieces, tiles_per_piece):
    x_refs = refs[:n_pieces]
    wgu_ref, bgu_ref, wd_ref, bd_ref, y_ref, wgu_bf, wd_bf = refs[n_pieces:]
    i = pl.program_id(0)
    changed = jnp.logical_or(i == 0, te_ref[i] != te_ref[jnp.maximum(i - 1, 0)])

    @pl.when(jnp.logical_and(changed, i < nu_ref[0]))
    def _():
        wgu_bf[...] = wgu_ref[0].astype(BF16)
        wd_bf[...] = wd_ref[0].astype(BF16)

    @pl.when(i < nu_ref[0])
    def _():
        de = wd_ref.shape[1]
        piece = i // tiles_per_piece
        x = x_refs[n_pieces - 1][...]
        for p in range(n_pieces - 2, -1, -1):
            x = jnp.where(piece == p, x_refs[p][...], x)
        gu = _dot(x, wgu_bf[...]) + bgu_ref[0]
        gate = jnp.minimum(gu[:, 0:de], SWIGLU_LIMIT)
        up = jnp.clip(gu[:, de:2 * de], -SWIGLU_LIMIT, SWIGLU_LIMIT)
        act = (up + 1.0) * gate * _sigmoid(gate * SWIGLU_ALPHA)
        y_ref[...] = (_dot(act.astype(BF16), wd_bf[...]) + bd_ref[0]).astype(y_ref.dtype)

    @pl.when(i >= nu_ref[0])
    def _():
        y_ref[...] = jnp.zeros_like(y_ref)


def _moe_experts(x_pieces, tile_expert, n_used, w_gate_up, b_gate_up, w_down, b_down):
    n_pieces = len(x_pieces)
    rows_per_piece, d = x_pieces[0].shape
    tm = MOE_TILE
    tpp = rows_per_piece // tm
    e, _, n2 = w_gate_up.shape
    de = w_down.shape[1]

    def piece_spec(p):
        return pl.BlockSpec((tm, d), lambda i, te, nu: (jnp.clip(i - p * tpp, 0, tpp - 1), 0))

    grid_spec = pltpu.PrefetchScalarGridSpec(
        num_scalar_prefetch=2,
        grid=(n_pieces * tpp,),
        in_specs=[piece_spec(p) for p in range(n_pieces)] + [
                  pl.BlockSpec((1, d, n2), lambda i, te, nu: (te[i], 0, 0)),
                  pl.BlockSpec((1, 1, n2), lambda i, te, nu: (te[i], 0, 0)),
                  pl.BlockSpec((1, de, d), lambda i, te, nu: (te[i], 0, 0)),
                  pl.BlockSpec((1, 1, d), lambda i, te, nu: (te[i], 0, 0))],
        out_specs=pl.BlockSpec((tm, d), lambda i, te, nu: (i, 0)),
        scratch_shapes=[pltpu.VMEM((d, n2), BF16), pltpu.VMEM((de, d), BF16)],
    )
    return pl.pallas_call(
        functools.partial(_moe_kernel, n_pieces=n_pieces, tiles_per_piece=tpp),
        grid_spec=grid_spec,
        out_shape=jax.ShapeDtypeStruct((n_pieces * rows_per_piece, d), BF16),
        compiler_params=_cparams(("arbitrary",)),
        name="moe_experts",
    )(tile_expert, n_used, *x_pieces, w_gate_up, b_gate_up.reshape(e, 1, n2), w_down, b_down.reshape(e, 1, d))


def _route(logits, n_tokens):
    tm = MOE_TILE
    top_val, top_idx = lax.top_k(logits, TOP_K)
    weights = jax.nn.softmax(top_val, axis=-1)
    flat_e = top_idx.reshape(-1).astype(jnp.int32)
    n_assign = flat_e.shape[0]
    iota = jnp.arange(n_assign, dtype=jnp.int32)
    sorted_e, order = lax.sort((flat_e, iota), num_keys=1, is_stable=True)
    counts = jnp.sum(jax.nn.one_hot(flat_e, N_EXPERTS, dtype=jnp.int32), axis=0)
    padded = ((counts + tm - 1) // tm) * tm
    pad_end = jnp.cumsum(padded)
    pad_start = pad_end - padded
    raw_start = jnp.cumsum(counts) - counts
    dest = pad_start[sorted_e] + (iota - raw_start[sorted_e])
    _, pos = lax.sort((order, dest), num_keys=1)
    pos = pos.reshape(n_tokens, TOP_K)
    n_rows = n_assign + N_EXPERTS * tm
    n_tiles = n_rows // tm
    n_used = (pad_end[-1] // tm).astype(jnp.int32)
    tile_start = jnp.arange(n_tiles, dtype=jnp.int32) * tm
    tile_expert = jnp.sum((tile_start[:, None] >= pad_end[None, :]).astype(jnp.int32), axis=1)
    last_e = jnp.sum((pad_end[-1] - 1 >= pad_end).astype(jnp.int32))
    tile_expert = jnp.minimum(tile_expert, last_e).astype(jnp.int32)
    lane = jnp.arange(tm, dtype=jnp.int32)[None, :]
    off = lane + (tile_start - pad_start[tile_expert])[:, None]
    valid = off < counts[tile_expert][:, None]
    sidx = jnp.clip(raw_start[tile_expert][:, None] + off, 0, n_assign - 1)
    filler = (tile_start[:, None] + lane) % n_tokens
    src_token = jnp.where(valid, (order // TOP_K)[sidx], filler).reshape(-1)
    return weights, src_token, pos, tile_expert, n_used.reshape(1)


def _final_kernel(x1_ref, y0_ref, y1_ref, y2_ref, y3_ref, wt_ref, gt2_ref, nf_ref, o_ref):
    wt = wt_ref[0]
    moe = (wt[:, 0:1] * y0_ref[0, 0].astype(F32) + wt[:, 1:2] * y1_ref[0, 0].astype(F32)
           + wt[:, 2:3] * y2_ref[0, 0].astype(F32) + wt[:, 3:4] * y3_ref[0, 0].astype(F32))
    x2 = x1_ref[0] + gt2_ref[0] * moe
    o_ref[0] = x2 * lax.rsqrt(jnp.mean(x2 * x2, axis=-1, keepdims=True) + EPS) * nf_ref[...]


def _final(x1, yg, wt, mod3, norm_f):
    b, l, d = x1.shape
    tm = ROW_TILE
    tok = lambda width: pl.BlockSpec((1, tm, width), lambda i, j: (i, j, 0))
    ysel = lambda k: pl.BlockSpec((1, 1, tm, d), lambda i, j: (k, i, j, 0))
    return pl.pallas_call(
        _final_kernel,
        grid=(b, l // tm),
        in_specs=[tok(d)] + [ysel(k) for k in range(TOP_K)] + [tok(128), pl.BlockSpec((1, 1, d), lambda i, j: (i * 6 + 5, 0, 0)),
                                 pl.BlockSpec((1, d), lambda i, j: (0, 0))],
        out_specs=tok(d),
        out_shape=jax.ShapeDtypeStruct((b, l, d), F32),
        compiler_params=_cparams(("parallel", "parallel")),
        name="combine_final_norm",
    )(x1, *([yg] * TOP_K), wt, mod3, norm_f.reshape(1, d))


def _split_hi_lo(w):
    hi = w.astype(BF16)
    lo = (w - hi.astype(F32)).astype(BF16)
    return hi, lo


def kernel(x, c, ctx, c_ctx, w_mod, b_mod, norm1, w_in, s5_lam_re, s5_lam_im, s5_log_step, s5_b_re, s5_b_im, s5_c_re, s5_c_im, s5_d, s5_w_glu, s5_b_glu, gdn_conv, gdn_a_log, gdn_dt_bias, gdn_norm, w_branch_a, w_branch_b, w_out, norm2, w_router, b_router, w_gate_up, b_gate_up, w_down, b_down, norm_f):
    b, l, d = x.shape
    lc = ctx.shape[1]
    depth = w_mod.shape[0]
    assert depth == 1, "single-layer block: the context stream has no consumer after the token mixer"
    assert l % ROW_TILE == 0 and lc % ROW_TILE == 0 and lc & (lc - 1) == 0 and b <= 8
    ly = 0

    cc = jnp.zeros((16, d), F32).at[0:b].set(c).at[b].set(c_ctx)
    mod = _modulation(cc, w_mod[ly], b_mod[ly])
    mod3 = mod[0:b + 1].reshape((b + 1) * 6, 1, d)

    wi = w_in[ly]
    o_u, o_qkv, o_gate, o_ba, o_br = 0, 512, 2048, 2560, 2576
    w_ba = wi[:, o_ba:o_br]
    ba_hi, ba_lo = _split_hi_lo(w_ba)
    pad = lambda t: jnp.pad(t, ((0, 0), (0, 128 - t.shape[1])))
    w_main = jnp.concatenate([wi[:, o_u:o_ba].astype(BF16), wi[:, o_br:].astype(BF16), pad(ba_hi), pad(ba_lo)], axis=1)
    wt = jnp.concatenate([ba_hi.T, ba_lo.T], axis=0)
    u_all, qkv, gate, br, ba, bat = _in_projection(x, ctx, mod3, norm1[ly], w_main, pad(ba_hi), wt, ba_hi.T)

    mats = _s5_matrices(s5_lam_re[ly], s5_lam_im[ly], s5_log_step[ly], s5_b_re[ly], s5_b_im[ly],
                        s5_c_re[ly], s5_c_im[ly], s5_d[ly])
    z = _s5_mixer(u_all, l, mats)

    lt = l + lc
    nch = lt // GDN_CHUNK
    bat_chunks = jnp.transpose(bat.reshape(b, 16, nch, GDN_CHUNK), (0, 2, 1, 3))
    ea = jnp.exp(gdn_a_log[ly].astype(F32)).reshape(-1)
    dtb = gdn_dt_bias[ly].astype(F32).reshape(-1)
    prow = jnp.zeros((2, 128), F32).at[0, 8:16].set(ea).at[1, 8:16].set(dtb)
    pcol = jnp.stack([ea, dtb], axis=1)
    conv_w = jnp.zeros((8, qkv.shape[2]), F32).at[0:CONV_K].set(gdn_conv[ly].astype(F32))
    prep = _gdn_prepare(qkv, conv_w, ba, bat_chunks, prow, pcol, l)
    o_f, o_b = _gdn_scan(prep, l)

    wr = jnp.pad(w_router[ly].astype(F32), ((0, 0), (0, 128 - N_EXPERTS)))
    brt = jnp.pad(b_router[ly].astype(F32), (0, 128 - N_EXPERTS)).reshape(1, 128)
    x1, h2, logits = _merge(x, z, o_f, o_b, gate, br, mod3,
                            s5_w_glu[ly].astype(BF16), s5_b_glu[ly].astype(F32).reshape(1, -1),
                            gdn_norm[ly].astype(F32).reshape(1, -1),
                            w_branch_a[ly].astype(BF16), w_branch_b[ly].astype(BF16), w_out[ly].astype(BF16),
                            norm2[ly].astype(F32).reshape(1, d), wr, brt)

    n_tok = b * l
    weights, src_token, pos, tile_expert, n_used = _route(logits.reshape(n_tok, 128)[:, 0:N_EXPERTS], n_tok)
    h2f = h2.reshape(2 * n_tok, d)
    xs = [h2f[piece] for piece in jnp.split(src_token, src_token.shape[0] // n_tok)]
    ys = _moe_experts(xs, tile_expert, n_used, w_gate_up[ly], b_gate_up[ly], w_down[ly], b_down[ly])
    yg = ys[pos.T.reshape(-1)].reshape(TOP_K, b, l, d)
    wt4 = jnp.pad(weights, ((0, 0), (0, 128 - TOP_K))).reshape(b, l, 128)
    return _final(x1, yg, wt4, mod3, norm_f)
```

```python
import functools

import numpy as np
import jax
import jax.numpy as jnp
from jax import lax
from jax.experimental import pallas as pl
from jax.experimental.pallas import tpu as pltpu

F32 = jnp.float32
BF16 = jnp.bfloat16
HIGHEST = lax.Precision.HIGHEST

EPS = 1e-6
GRID_W = 64

S5_GROUP = 16
S5_STATE = 64
S5_Q = 16

GDN_HEADS = 4
GDN_DK = 128
GDN_CHUNK = 64
CONV_K = 5

N_EXPERTS = 32
TOP_K = 4
SWIGLU_LIMIT = 7.0
SWIGLU_ALPHA = 1.702

ROW_TILE = 256
MERGE_TILE = 512
MOE_TILE = 512
VMEM_LIMIT = 56 * 1024 * 1024


def _cparams(sem):
    return pltpu.CompilerParams(dimension_semantics=sem, vmem_limit_bytes=VMEM_LIMIT)


def _sigmoid(v):
    return 1.0 / (1.0 + jnp.exp(-v))


def _softplus(v):
    return jnp.maximum(v, 0.0) + jnp.log(1.0 + jnp.exp(-jnp.abs(v)))


def _dot(a, b):
    return jnp.dot(a, b, preferred_element_type=F32)


def _dot_nt(a, b):
    return lax.dot_general(a, b, (((1,), (1,)), ((), ())), preferred_element_type=F32)


def _dot_tn(a, b):
    return lax.dot_general(a, b, (((0,), (0,)), ((), ())), preferred_element_type=F32)


def _mod_kernel(c_ref, w_ref, b_ref, o_ref):
    cc = c_ref[...]
    s = cc * _sigmoid(cc)
    o_ref[...] = jnp.dot(s, w_ref[...], precision=HIGHEST, preferred_element_type=F32) + b_ref[...]


def _modulation(cc, w_mod, b_mod):
    rows, d = cc.shape
    n = w_mod.shape[1]
    bn = d
    return pl.pallas_call(
        _mod_kernel,
        grid=(n // bn,),
        in_specs=[pl.BlockSpec((rows, d), lambda j: (0, 0)),
                  pl.BlockSpec((d, bn), lambda j: (0, j)),
                  pl.BlockSpec((1, bn), lambda j: (0, j))],
        out_specs=pl.BlockSpec((rows, bn), lambda j: (0, j)),
        out_shape=jax.ShapeDtypeStruct((rows, n), F32),
        compiler_params=_cparams(("arbitrary",)),
        name="adaln_mod",
    )(cc, w_mod, b_mod.reshape(1, n))


def _inproj_kernel(x_ref, c_ref, sc_ref, sh_ref, g_ref, w_ref, wlo_ref, wt_ref, wtlo_ref,
                   ut_ref, qkv_ref, gate_ref, br_ref, ba_ref, bat_ref, us_ref, *, n_lat_blocks):
    j = pl.program_id(1)
    is_ctx = j >= n_lat_blocks
    x = jnp.where(is_ctx, c_ref[0], x_ref[0])
    ms = jnp.mean(x * x, axis=-1, keepdims=True)
    y = x * lax.rsqrt(ms + EPS) * g_ref[...]
    h = y * (1.0 + sc_ref[0]) + sh_ref[0]
    h_hi = h.astype(BF16)
    h_lo = (h - h_hi.astype(F32)).astype(BF16)
    z = _dot(h_hi, w_ref[...])
    nchunk = us_ref.shape[1] // S5_Q
    for c in range(us_ref.shape[0]):
        us_ref[c] = z[:, c * 128:(c + 1) * 128]
        for t in range(S5_Q):
            ut_ref[t, 0, :, c * 128:(c + 1) * 128] = us_ref[c, pl.ds(t, nchunk, stride=S5_Q), :].astype(BF16)
    qkv_ref[0] = z[:, 512:2048].astype(BF16)
    gate_ref[0] = z[:, 2048:2560].astype(BF16)
    br_ref[0] = z[:, 2560:4608].astype(BF16)
    ba_ref[0] = z[:, 4608:4736] + z[:, 4736:4864] + _dot(h_lo, wlo_ref[...])
    rt = _dot_nt(wt_ref[...], h_hi)
    bat_ref[0] = rt[0:16] + rt[16:32] + _dot_nt(wtlo_ref[...], h_lo)


def _in_projection(x, ctx, mod3, norm1, w_main, w_lo, wt, wt_lo):
    b, l, d = x.shape
    lc = ctx.shape[1]
    tm = ROW_TILE
    nl, nc = l // tm, lc // tm
    nb = mod3.shape[0] // 6 - 1
    lt = l + lc
    ncols = w_main.shape[1]

    def x_map(i, j):
        return (i, jnp.minimum(j, nl - 1), 0)

    def c_map(i, j):
        return (i, jnp.maximum(j - nl, 0), 0)

    def mod_map(k):
        return lambda i, j: (jnp.where(j >= nl, nb, i) * 6 + k, 0, 0)

    def tok(width):
        return pl.BlockSpec((1, tm, width), lambda i, j: (i, j, 0))

    const2 = lambda i, j: (0, 0)
    outs = pl.pallas_call(
        functools.partial(_inproj_kernel, n_lat_blocks=nl),
        grid=(b, nl + nc),
        in_specs=[pl.BlockSpec((1, tm, d), x_map),
                  pl.BlockSpec((1, tm, d), c_map),
                  pl.BlockSpec((1, 1, d), mod_map(1)),
                  pl.BlockSpec((1, 1, d), mod_map(0)),
                  pl.BlockSpec((1, d), const2),
                  pl.BlockSpec((d, ncols), const2),
                  pl.BlockSpec((d, 128), const2),
                  pl.BlockSpec((32, d), const2),
                  pl.BlockSpec((16, d), const2)],
        out_specs=[pl.BlockSpec((S5_Q, 1, tm // S5_Q, 512), lambda i, j: (0, i, j, 0)),
                   tok(1536), tok(512), tok(2048), tok(128),
                   pl.BlockSpec((1, 16, tm), lambda i, j: (i, 0, j))],
        out_shape=[jax.ShapeDtypeStruct((S5_Q, b, lt // S5_Q, 512), BF16),
                   jax.ShapeDtypeStruct((b, lt, 1536), BF16),
                   jax.ShapeDtypeStruct((b, lt, 512), BF16),
                   jax.ShapeDtypeStruct((b, lt, 2048), BF16),
                   jax.ShapeDtypeStruct((b, lt, 128), F32),
                   jax.ShapeDtypeStruct((b, 16, lt), F32)],
        scratch_shapes=[pltpu.VMEM((4, tm, 128), F32)],
        compiler_params=_cparams(("parallel", "arbitrary")),
        name="in_projection",
    )(x, ctx, mod3, mod3, norm1.reshape(1, d), w_main, w_lo, wt, wt_lo)
    return outs


def _s5_matrices(lam_re, lam_im, log_step, b_re, b_im, c_re, c_im, d_skip):
    q = S5_Q
    lr = jnp.minimum(lam_re.astype(F32), -1e-4)
    li = lam_im.astype(F32)
    dt = jnp.exp(log_step.astype(F32))[..., None]
    pw = jnp.arange(q + 1, dtype=F32)[:, None, None, None]
    mag = jnp.exp(pw * (lr * dt))
    ang = pw * (li * dt)
    pr, pi = mag * jnp.cos(ang), mag * jnp.sin(ang)
    ar, ai = pr[1], pi[1]
    den = lr * lr + li * li
    fr = ((ar - 1.0) * lr + ai * li) / den
    fi = (ai * lr - (ar - 1.0) * li) / den
    br, bi = b_re.astype(F32), b_im.astype(F32)
    bbr = fr[..., None] * br - fi[..., None] * bi
    bbi = fr[..., None] * bi + fi[..., None] * br
    cr, ci = c_re.astype(F32), c_im.astype(F32)
    car = cr[None] * pr[:, :, :, None, :] - ci[None] * pi[:, :, :, None, :]
    cai = cr[None] * pi[:, :, :, None, :] + ci[None] * pr[:, :, :, None, :]
    bbr_t = jnp.swapaxes(bbr, -1, -2)[None, :, :, None]
    bbi_t = jnp.swapaxes(bbi, -1, -2)[None, :, :, None]
    kern = jnp.sum(car[..., None, :] * bbr_t - cai[..., None, :] * bbi_t, axis=-1)
    g = lr.shape[1]
    jj = np.arange(q)[:, None]
    ii = np.arange(q)[None, :]
    df = ii - jj
    mf = jnp.where((df >= 0)[:, :, None, None, None], kern[:q, 0][np.clip(df, 0, q - 1)], 0.0)
    mb = jnp.where((df <= 0)[:, :, None, None, None], kern[:q, 1][np.clip(-df, 0, q - 1)], 0.0)
    m = jnp.transpose(mf + mb, (2, 0, 4, 1, 3))
    skip = (jnp.eye(q, dtype=F32)[None, :, None, :, None]
            * jnp.eye(S5_GROUP, dtype=F32)[None, None, :, None, :]
            * d_skip.astype(F32).reshape(g, 1, S5_GROUP, 1, 1))
    m = (m + skip).reshape(g, q * S5_GROUP, q * S5_GROUP)

    def p_block(powers, d):
        ppr, ppi = pr[powers, d], pi[powers, d]
        re = ppr[..., None] * bbr[d][None] - ppi[..., None] * bbi[d][None]
        im = ppr[..., None] * bbi[d][None] + ppi[..., None] * bbr[d][None]
        to = lambda t: jnp.transpose(t, (1, 0, 3, 2)).reshape(g, q * S5_GROUP, S5_STATE)
        return to(re), to(im)

    pf_re, pf_im = p_block(np.arange(q - 1, -1, -1), 0)
    pb_re, pb_im = p_block(np.arange(q), 1)
    p = jnp.concatenate([pf_re, pb_re, pf_im, pb_im], axis=-1)

    def r_block(powers, d):
        to = lambda t: jnp.transpose(t, (1, 3, 0, 2)).reshape(g, S5_STATE, q * S5_GROUP)
        return to(car[powers, d]), to(-cai[powers, d])

    rf_re, rf_im = r_block(np.arange(1, q + 1), 0)
    rb_re, rb_im = r_block(np.arange(q, 0, -1), 1)
    r = jnp.concatenate([rf_re, rb_re, rf_im, rb_im], axis=1)
    a16 = jnp.stack([jnp.concatenate([pr[q, 0], pr[q, 1]], axis=-1),
                     jnp.concatenate([pi[q, 0], pi[q, 1]], axis=-1)], axis=1)
    return m.astype(BF16), p.astype(BF16), r.astype(BF16), a16


S5_LANE_GROUPS = 8
S5_BATCH = 4


def _s5_kernel(ut_ref, m_ref, p_ref, r_ref, a_ref, zt_ref,
               pi_ref, x_ref, xp_ref, v_ref, sk_ref, srb_ref, y_ref, yp_ref, *, n_lat, n_ctx, bb):
    q, ng = S5_Q, S5_LANE_GROUPS
    qc = q * S5_GROUP
    width = ng * qc
    n_all = n_lat + n_ctx
    rows_lat = n_lat * bb

    @pl.when(jnp.logical_and(pl.program_id(0) == 0, pl.program_id(1) == 0))
    def _():
        src = lax.broadcasted_iota(jnp.int32, (width, qc), 0)
        dst = lax.broadcasted_iota(jnp.int32, (width, qc), 1)
        dest_in_group = ((src >> 7) << 4) + (src & 15)
        grp = (src >> 4) & (ng - 1)
        for g in range(ng):
            hit = jnp.logical_and(grp == g, dest_in_group == dst)
            pi_ref[:, g * qc:(g + 1) * qc] = jnp.where(hit, 1.0, 0.0).astype(BF16)

    for b in range(bb):
        for t in range(q):
            x_ref[t, pl.ds(b, n_all, stride=bb), :] = ut_ref[t, b].astype(F32)
    xb = jnp.concatenate([x_ref[t] for t in range(q)], axis=1).astype(BF16)
    for g in range(ng):
        cols = slice(g * qc, (g + 1) * qc)
        xg = _dot(xb, pi_ref[:, cols]).astype(BF16)
        xp_ref[:, cols] = xg
        v_ref[:, cols] = _dot(xg, p_ref[g])

    a = a_ref[...]
    a1 = jnp.concatenate([a[g, 0:1] for g in range(ng) for _ in range(2)], axis=1)
    a2 = jnp.concatenate([s * a[g, 1:2] for g in range(ng) for s in (-1.0, 1.0)], axis=1)
    fwd_lane = (lax.broadcasted_iota(jnp.int32, (bb, width), 1) & 127) < S5_STATE

    def partner(s):
        tiles = [s[:, i * 128:(i + 1) * 128] for i in range(2 * ng)]
        return jnp.concatenate([tiles[i ^ 1] for i in range(2 * ng)], axis=1)

    def advance(s, vf, vb):
        return a1 * s + a2 * partner(s) + jnp.where(fwd_lane, vf, vb)

    def scan(base, n, s0, store):
        def body(t, s):
            rf = pl.multiple_of((base + 2 * t) * bb, 2 * bb)
            rb = pl.multiple_of((base + n - 2 - 2 * t) * bb, 2 * bb)
            vf = v_ref[pl.ds(rf, 2 * bb), :]
            vb = v_ref[pl.ds(rb, 2 * bb), :]
            s1 = advance(s, vf[0:bb], vb[bb:2 * bb])
            s2 = advance(s1, vf[bb:2 * bb], vb[0:bb])
            if store:
                sk_ref[pl.ds(rf, 2 * bb), :] = jnp.concatenate([s, s1], axis=0)
                srb_ref[pl.ds(rb, 2 * bb), :] = jnp.concatenate([s1, s], axis=0)
            return s2
        return lax.fori_loop(0, n // 2, body, s0)

    s_ctx = scan(n_lat, n_ctx, jnp.zeros((bb, width), F32), False)
    scan(0, n_lat, s_ctx, True)
    lane = (lax.broadcasted_iota(jnp.int32, (rows_lat, width), 1) & 127) < S5_STATE
    st = jnp.where(lane, sk_ref[...], srb_ref[...]).astype(BF16)
    for g in range(ng):
        cols = slice(g * qc, (g + 1) * qc)
        y = _dot(xp_ref[0:rows_lat, cols], m_ref[g]) + _dot(st[:, cols], r_ref[g])
        y_ref[:, cols] = jax.nn.gelu(y).astype(BF16)
    yb = y_ref[...]
    for t in range(q):
        yp_ref[...] = _dot_nt(yb, pi_ref[t * 128:(t + 1) * 128, :])
        for b in range(bb):
            zt_ref[t, b] = yp_ref[pl.ds(b, n_lat, stride=bb), :].astype(BF16)


def _s5_mixer(ut, l, mats):
    m, p, r, a16 = mats
    q, b, n_all, width = ut.shape
    n_lat = l // q
    ng = S5_LANE_GROUPS
    qc = q * S5_GROUP
    bb = min(S5_BATCH, b)
    nj = width // 128
    wspec = pl.BlockSpec((ng, qc, qc), lambda j, i: (j, 0, 0))
    return pl.pallas_call(
        functools.partial(_s5_kernel, n_lat=n_lat, n_ctx=n_all - n_lat, bb=bb),
        grid=(nj, b // bb),
        in_specs=[pl.BlockSpec((q, bb, n_all, 128), lambda j, i: (0, i, 0, j)),
                  wspec, wspec, wspec,
                  pl.BlockSpec((ng, 2, 2 * S5_STATE), lambda j, i: (j, 0, 0))],
        out_specs=pl.BlockSpec((q, bb, n_lat, 128), lambda j, i: (0, i, 0, j)),
        out_shape=jax.ShapeDtypeStruct((q, b, n_lat, width), BF16),
        scratch_shapes=[pltpu.VMEM((ng * qc, ng * qc), BF16),
                        pltpu.VMEM((q, n_all * bb, 128), F32),
                        pltpu.VMEM((n_all * bb, ng * qc), BF16),
                        pltpu.VMEM((n_all * bb, ng * qc), F32),
                        pltpu.VMEM((n_lat * bb, ng * qc), F32),
                        pltpu.VMEM((n_lat * bb, ng * qc), F32),
                        pltpu.VMEM((n_lat * bb, ng * qc), BF16),
                        pltpu.VMEM((n_lat * bb, 128), F32)],
        compiler_params=_cparams(("arbitrary", "arbitrary")),
        name="s5_scan",
    )(ut, m, p, r, a16)


def _inv_unit_triangular_many(mats):
    c = mats[0].shape[0]
    eye = (lax.broadcasted_iota(jnp.int32, (c, c), 0) == lax.broadcasted_iota(jnp.int32, (c, c), 1)).astype(F32)
    prods = [eye - a for a in mats]
    pows = [a.astype(BF16) for a in mats]
    for _ in range(int(np.log2(c)) - 1):
        pows = [_dot(a, a).astype(BF16) for a in pows]
        prods = [p + _dot(p.astype(BF16), a) for p, a in zip(prods, pows)]
    return prods


def _gdn_prep_kernel(qkv_ref, cw_ref, ba_ref, bat_ref, prow_ref, pcol_ref,
                     u_ref, wq_ref, qk_ref, kdt_ref, dl_ref, *, n_lat_blocks, ctx_len):
    j = pl.program_id(1)
    rows = qkv_ref.shape[1]
    c = GDN_CHUNK
    nchunk = rows // c
    seg = jnp.where(j >= n_lat_blocks, ctx_len, GRID_W)
    x = qkv_ref[0].astype(F32)
    pos = lax.broadcasted_iota(jnp.int32, (rows, 1), 0) & (seg - 1)
    acc = x * cw_ref[2:3, :]
    for s in (-2, -1, 1, 2):
        shifted = pltpu.roll(x, (-s) % rows, 0)
        ok = jnp.logical_and(pos + s >= 0, pos + s < seg)
        acc = acc + jnp.where(ok, shifted, 0.0) * cw_ref[2 + s:3 + s, :]
    act = acc * _sigmoid(acc)

    ii = lax.broadcasted_iota(jnp.int32, (c, c), 0)
    jj = lax.broadcasted_iota(jnp.int32, (c, c), 1)
    incl = (ii >= jj, ii <= jj)
    strict = (ii > jj, ii < jj)
    tri_low = incl[0].astype(F32)
    tri_up = incl[1].astype(F32)
    eye_bf = (ii == jj).astype(BF16)
    ones = jnp.ones((c, c), F32)
    lane = lax.broadcasted_iota(jnp.int32, (c, 128), 1)
    prow = prow_ref[...]
    pcol = pcol_ref[...]
    hp = dict(precision=HIGHEST, preferred_element_type=F32)

    gates = []
    for ci in range(nchunk):
        ba = ba_ref[0, ci * c:(ci + 1) * c, :]
        g_all = -prow[0:1, :] * _softplus(ba + prow[1:2, :])
        g_all = jnp.where(jnp.logical_and(lane >= 8, lane < 16), g_all, 0.0)
        bat = bat_ref[0, ci]
        g_row = -pcol[:, 0:1] * _softplus(bat[8:16, :] + pcol[:, 1:2])
        gates.append(dict(beta=_sigmoid(ba),
                          gc=(jnp.dot(tri_low, g_all, **hp), jnp.dot(tri_up, g_all, **hp)),
                          gtot=jnp.dot(ones, g_all, **hp),
                          gr=(jnp.dot(g_row, tri_up, **hp), jnp.dot(g_row, tri_low, **hp))))

    qn, kn, vv = [], [], []
    for h in range(GDN_HEADS):
        qh = act[:, h * 128:(h + 1) * 128]
        kh = act[:, 512 + h * 128:512 + (h + 1) * 128]
        qn.append(qh * lax.rsqrt(jnp.sum(qh * qh, axis=-1, keepdims=True) + EPS) * (GDN_DK ** -0.5))
        kn.append(kh * lax.rsqrt(jnp.sum(kh * kh, axis=-1, keepdims=True) + EPS))
        vv.append(act[:, 1024 + h * 128:1024 + (h + 1) * 128])

    pairs = [(ci, h) for ci in range(nchunk) for h in range(GDN_HEADS)]
    sl = lambda t, ci: t[ci * c:(ci + 1) * c]
    kb = {p: sl(kn[p[1]], p[0]).astype(BF16) for p in pairs}
    kk = {p: _dot_nt(kb[p], kb[p]) for p in pairs}
    qkm = {p: _dot_nt(sl(qn[p[1]], p[0]).astype(BF16), kb[p]) for p in pairs}

    probs = [(ci, h, d) for ci in range(nchunk) for h in range(GDN_HEADS) for d in range(2)]
    beta, gc, gt, decay, a_mats = {}, {}, {}, {}, []
    for (ci, h, d) in probs:
        col = d * GDN_HEADS + h
        gi = gates[ci]
        beta[ci, h, d] = gi["beta"][:, col:col + 1]
        gc[ci, h, d] = gi["gc"][d][:, 8 + col:9 + col]
        gt[ci, h, d] = gi["gtot"][:, 8 + col:9 + col]
        grow = gi["gr"][d][col:col + 1, :]
        decay[ci, h, d] = jnp.exp(jnp.where(incl[d], gc[ci, h, d] - grow, -jnp.inf))
        a_mats.append(jnp.where(strict[d], beta[ci, h, d] * kk[ci, h] * decay[ci, h, d], 0.0))
    tinv = _inv_unit_triangular_many(a_mats)

    eg, sols, kdts = {}, {}, {}
    for n, (ci, h, d) in enumerate(probs):
        p = (ci, h, d)
        eg[p] = jnp.exp(gc[p])
        kh = sl(kn[h], ci)
        rhs = jnp.concatenate([sl(vv[h], ci) * beta[p], kh * (beta[p] * eg[p])], axis=-1).astype(BF16)
        sols[p] = _dot(tinv[n].astype(BF16), rhs)
        kdts[p] = _dot_tn((kh * jnp.exp(gt[p] - gc[p])).astype(BF16), eye_bf)
    for (ci, h, d) in probs:
        p = (ci, h, d)
        u_ref[0, d, h, ci] = sols[p][:, 0:128]
        wq_ref[0, d, h, ci, 0:c, :] = sols[p][:, 128:256].astype(BF16)
        wq_ref[0, d, h, ci, c:2 * c, :] = (sl(qn[h], ci) * eg[p]).astype(BF16)
        qk_ref[0, d, h, ci] = (qkm[ci, h] * decay[p]).astype(BF16)
        kdt_ref[0, d, h, ci] = kdts[p].astype(BF16)
        dl_ref[0, d, h, ci] = jnp.broadcast_to(jnp.exp(gt[p][0:8, :]), (8, 128))


def _gdn_prepare(qkv, conv_w, ba, bat_chunks, prow, pcol, l):
    b, lt, width = qkv.shape
    tm = ROW_TILE
    c = GDN_CHUNK
    cpb = tm // c
    nblk = lt // tm
    nch = lt // c
    hd = (b, 2, GDN_HEADS, nch)

    def blk(shape_tail, dtype):
        return (pl.BlockSpec((1, 2, GDN_HEADS, cpb) + shape_tail, lambda i, j: (i, 0, 0, j, 0, 0)),
                jax.ShapeDtypeStruct(hd + shape_tail, dtype))

    specs = [blk((c, 128), F32), blk((2 * c, 128), BF16), blk((c, c), BF16), blk((GDN_DK, c), BF16),
             blk((8, 128), F32)]
    return pl.pallas_call(
        functools.partial(_gdn_prep_kernel, n_lat_blocks=l // tm, ctx_len=lt - l),
        grid=(b, nblk),
        in_specs=[pl.BlockSpec((1, tm, width), lambda i, j: (i, j, 0)),
                  pl.BlockSpec((8, width), lambda i, j: (0, 0)),
                  pl.BlockSpec((1, tm, 128), lambda i, j: (i, j, 0)),
                  pl.BlockSpec((1, cpb, 16, c), lambda i, j: (i, j, 0, 0)),
                  pl.BlockSpec((2, 128), lambda i, j: (0, 0)),
                  pl.BlockSpec((8, 2), lambda i, j: (0, 0))],
        out_specs=[s for s, _ in specs],
        out_shape=[o for _, o in specs],
        compiler_params=_cparams(("parallel", "parallel")),
        name="gdn_prepare",
    )(qkv, conv_w, ba, bat_chunks, prow, pcol)


SCAN_GROUP = 2


def _gdn_scan_kernel(uf, wqf, qkf, kdtf, dlf, ub, wqb, qkb, kdtb, dlb, of_ref, ob_ref, s_ref):
    t = pl.program_id(0)
    nb = s_ref.shape[0]
    c = GDN_CHUNK

    @pl.when(t == 0)
    def _():
        s_ref[...] = jnp.zeros_like(s_ref)

    refs = ((uf, wqf, qkf, kdtf, dlf, of_ref), (ub, wqb, qkb, kdtb, dlb, ob_ref))
    for b0 in range(0, nb, SCAN_GROUP):
        chains = [(bi, d, h) for bi in range(b0, min(b0 + SCAN_GROUP, nb)) for d in range(2)
                  for h in range(GDN_HEADS)]
        s = {k: s_ref[k[0], k[1], k[2]] for k in chains}
        sb = {k: s[k].astype(BF16) for k in chains}
        r = {k: _dot(refs[k[1]][1][k[0], 0, k[2], 0], sb[k]) for k in chains}
        vb = {k: (refs[k[1]][0][k[0], 0, k[2], 0] - r[k][0:c]).astype(BF16) for k in chains}
        o = {k: r[k][c:2 * c] + _dot(refs[k[1]][2][k[0], 0, k[2], 0], vb[k]) for k in chains}
        sn = {k: s[k] * refs[k[1]][4][k[0], 0, k[2], 0][0:1, :] + _dot(refs[k[1]][3][k[0], 0, k[2], 0], vb[k])
              for k in chains}
        for k in chains:
            s_ref[k[0], k[1], k[2]] = sn[k]
            refs[k[1]][5][k[0], :, k[2] * 128:(k[2] + 1) * 128] = o[k]


def _gdn_scan(prep, l):
    u = prep[0]
    b, _, heads, nch, c, _ = u.shape
    n_lat = l // c
    n_ctx = nch - n_lat
    lt = nch * c

    def fwd_chunk(t):
        return jnp.where(t < n_ctx, n_lat + t, t - n_ctx)

    def bwd_chunk(t):
        return nch - 1 - t

    def spec(arr, chunk_of, d):
        tail = arr.shape[4:]
        return pl.BlockSpec((b, 1, heads, 1) + tail, lambda t: (0, d, 0, chunk_of(t), 0, 0))

    ins = [spec(a, fwd_chunk, 0) for a in prep] + [spec(a, bwd_chunk, 1) for a in prep]
    width = heads * 128
    return pl.pallas_call(
        _gdn_scan_kernel,
        grid=(nch,),
        in_specs=ins,
        out_specs=[pl.BlockSpec((b, c, width), lambda t: (0, fwd_chunk(t), 0)),
                   pl.BlockSpec((b, c, width), lambda t: (0, bwd_chunk(t), 0))],
        out_shape=[jax.ShapeDtypeStruct((b, lt, width), F32)] * 2,
        scratch_shapes=[pltpu.VMEM((b, 2, heads, GDN_DK, 128), F32)],
        compiler_params=_cparams(("arbitrary",)),
        name="gdn_scan",
    )(*prep, *prep)


def _merge_kernel(x_ref, z_ref, of_ref, ob_ref, gate_ref, br_ref, gt1_ref, sc2_ref, sh2_ref,
                  wglu_ref, bglu_ref, gnorm_ref, wa_ref, wb_ref, wo_ref, n2_ref, wr_ref, brt_ref,
                  x1_ref, h2_ref, lg_ref, zs_ref):
    nchunk = zs_ref.shape[1] // S5_Q
    for t in range(S5_Q):
        zt = z_ref[t, 0].astype(F32)
        for c in range(zs_ref.shape[0]):
            zs_ref[c, pl.ds(t, nchunk, stride=S5_Q), :] = zt[:, c * 128:(c + 1) * 128]
    zf = jnp.concatenate([zs_ref[c] for c in range(zs_ref.shape[0])], axis=1)
    z = zf.astype(BF16)
    ya = zf * _sigmoid(_dot(z, wglu_ref[...]) + bglu_ref[...])
    o = of_ref[0] + ob_ref[0]
    gate = gate_ref[0].astype(F32)
    parts = []
    for h in range(GDN_HEADS):
        oh = o[:, h * 128:(h + 1) * 128]
        parts.append(oh * lax.rsqrt(jnp.mean(oh * oh, axis=-1, keepdims=True) + EPS) * gnorm_ref[...])
    yb = jnp.concatenate(parts, axis=-1) * (gate * _sigmoid(gate))
    br = br_ref[0].astype(F32)
    d = x_ref.shape[2]
    ga = _sigmoid(br[:, 0:d])
    gb = _sigmoid(br[:, d:2 * d])
    m = ga * _dot(ya.astype(BF16), wa_ref[...]) + gb * _dot(yb.astype(BF16), wb_ref[...])
    mix = _dot(m.astype(BF16), wo_ref[...])
    x1 = x_ref[0] + gt1_ref[0] * mix
    x1_ref[0] = x1
    y2 = x1 * lax.rsqrt(jnp.mean(x1 * x1, axis=-1, keepdims=True) + EPS) * n2_ref[...]
    h2 = y2 * (1.0 + sc2_ref[0]) + sh2_ref[0]
    h2b = h2.astype(BF16)
    h2_ref[0, 0] = h2b
    h2_ref[1, 0] = h2b
    lg_ref[0] = jnp.dot(h2, wr_ref[...], precision=HIGHEST, preferred_element_type=F32) + brt_ref[...]


def _merge(x, z, o_f, o_b, gate, br, mod3, wglu, bglu, gnorm, wa, wb, wo, norm2, wr, brt):
    b, l, d = x.shape
    tm = min(MERGE_TILE, l)
    tok = lambda width: pl.BlockSpec((1, tm, width), lambda i, j: (i, j, 0))
    modspec = lambda k: pl.BlockSpec((1, 1, d), lambda i, j: (i * 6 + k, 0, 0))
    full = lambda arr: pl.BlockSpec(arr.shape, lambda i, j: (0,) * arr.ndim)
    consts = [wglu, bglu, gnorm, wa, wb, wo, norm2, wr, brt]
    return pl.pallas_call(
        _merge_kernel,
        grid=(b, l // tm),
        in_specs=[tok(d), pl.BlockSpec((S5_Q, 1, tm // S5_Q, 512), lambda i, j: (0, i, j, 0)),
                  tok(512), tok(512), tok(512), tok(2 * d),
                  modspec(2), modspec(4), modspec(3)] + [full(a) for a in consts],
        out_specs=[tok(d), pl.BlockSpec((2, 1, tm, d), lambda i, j: (0, i, j, 0)), tok(128)],
        out_shape=[jax.ShapeDtypeStruct((b, l, d), F32),
                   jax.ShapeDtypeStruct((2, b, l, d), BF16),
                   jax.ShapeDtypeStruct((b, l, 128), F32)],
        scratch_shapes=[pltpu.VMEM((4, tm, 128), F32)],
        compiler_params=_cparams(("parallel", "parallel")),
        name="branch_merge",
    )(x, z, o_f, o_b, gate, br, mod3, mod3, mod3, *consts)


def _moe_kernel(te_ref, nu_ref, *refs, n_pieces, tiles_per_piece):
    x_refs = refs[:n_pieces]
    wgu_ref, bgu_ref, wd_ref, bd_ref, y_ref, wgu_bf, wd_bf = refs[n_pieces:]
    i = pl.program_id(0)
    changed = jnp.logical_or(i == 0, te_ref[i] != te_ref[jnp.maximum(i - 1, 0)])

    @pl.when(jnp.logical_and(changed, i < nu_ref[0]))
    def _():
        wgu_bf[...] = wgu_ref[0].astype(BF16)
        wd_bf[...] = wd_ref[0].astype(BF16)

    @pl.when(i < nu_ref[0])
    def _():
        de = wd_ref.shape[1]
        piece = i // tiles_per_piece
        x = x_refs[n_pieces - 1][...]
        for p in range(n_pieces - 2, -1, -1):
            x = jnp.where(piece == p, x_refs[p][...], x)
        gu = _dot(x, wgu_bf[...]) + bgu_ref[0]
        gate = jnp.minimum(gu[:, 0:de], SWIGLU_LIMIT)
        up = jnp.clip(gu[:, de:2 * de], -SWIGLU_LIMIT, SWIGLU_LIMIT)
        act = (up + 1.0) * gate * _sigmoid(gate * SWIGLU_ALPHA)
        y_ref[...] = (_dot(act.astype(BF16), wd_bf[...]) + bd_ref[0]).astype(y_ref.dtype)

    @pl.when(i >= nu_ref[0])
    def _():
        y_ref[...] = jnp.zeros_like(y_ref)


def _moe_experts(x_pieces, tile_expert, n_used, w_gate_up, b_gate_up, w_down, b_down):
    n_pieces = len(x_pieces)
    rows_per_piece, d = x_pieces[0].shape
    tm = MOE_TILE
    tpp = rows_per_piece // tm
    e, _, n2 = w_gate_up.shape
    de = w_down.shape[1]

    def piece_spec(p):
        return pl.BlockSpec((tm, d), lambda i, te, nu: (jnp.clip(i - p * tpp, 0, tpp - 1), 0))

    grid_spec = pltpu.PrefetchScalarGridSpec(
        num_scalar_prefetch=2,
        grid=(n_pieces * tpp,),
        in_specs=[piece_spec(p) for p in range(n_pieces)] + [
                  pl.BlockSpec((1, d, n2), lambda i, te, nu: (te[i], 0, 0)),
                  pl.BlockSpec((1, 1, n2), lambda i, te, nu: (te[i], 0, 0)),
                  pl.BlockSpec((1, de, d), lambda i, te, nu: (te[i], 0, 0)),
                  pl.BlockSpec((1, 1, d), lambda i, te, nu: (te[i], 0, 0))],
        out_specs=pl.BlockSpec((tm, d), lambda i, te, nu: (i, 0)),
        scratch_shapes=[pltpu.VMEM((d, n2), BF16), pltpu.VMEM((de, d), BF16)],
    )
    return pl.pallas_call(
        functools.partial(_moe_kernel, n_pieces=n_pieces, tiles_per_piece=tpp),
        grid_spec=grid_spec,
        out_shape=jax.ShapeDtypeStruct((n_pieces * rows_per_piece, d), BF16),
        compiler_params=_cparams(("arbitrary",)),
        name="moe_experts",
    )(tile_expert, n_used, *x_pieces, w_gate_up, b_gate_up.reshape(e, 1, n2), w_down, b_down.reshape(e, 1, d))


def _route(logits, n_tokens):
    tm = MOE_TILE
    top_val, top_idx = lax.top_k(logits, TOP_K)
    weights = jax.nn.softmax(top_val, axis=-1)
    flat_e = top_idx.reshape(-1).astype(jnp.int32)
    n_assign = flat_e.shape[0]
    iota = jnp.arange(n_assign, dtype=jnp.int32)
    sorted_e, order = lax.sort((flat_e, iota), num_keys=1, is_stable=True)
    counts = jnp.sum(jax.nn.one_hot(flat_e, N_EXPERTS, dtype=jnp.int32), axis=0)
    padded = ((counts + tm - 1) // tm) * tm
    pad_end = jnp.cumsum(padded)
    pad_start = pad_end - padded
    raw_start = jnp.cumsum(counts) - counts
    dest = pad_start[sorted_e] + (iota - raw_start[sorted_e])
    _, pos = lax.sort((order, dest), num_keys=1)
    pos = pos.reshape(n_tokens, TOP_K)
    n_rows = n_assign + N_EXPERTS * tm
    n_tiles = n_rows // tm
    n_used = (pad_end[-1] // tm).astype(jnp.int32)
    tile_start = jnp.arange(n_tiles, dtype=jnp.int32) * tm
    tile_expert = jnp.sum((tile_start[:, None] >= pad_end[None, :]).astype(jnp.int32), axis=1)
    last_e = jnp.sum((pad_end[-1] - 1 >= pad_end).astype(jnp.int32))
    tile_expert = jnp.minimum(tile_expert, last_e).astype(jnp.int32)
    lane = jnp.arange(tm, dtype=jnp.int32)[None, :]
    off = lane + (tile_start - pad_start[tile_expert])[:, None]
    valid = off < counts[tile_expert][:, None]
    sidx = jnp.clip(raw_start[tile_expert][:, None] + off, 0, n_assign - 1)
    filler = (tile_start[:, None] + lane) % n_tokens
    src_token = jnp.where(valid, (order // TOP_K)[sidx], filler).reshape(-1)
    return weights, src_token, pos, tile_expert, n_used.reshape(1)


def _final_kernel(x1_ref, y0_ref, y1_ref, y2_ref, y3_ref, wt_ref, gt2_ref, nf_ref, o_ref):
    wt = wt_ref[0]
    moe = (wt[:, 0:1] * y0_ref[0, 0].astype(F32) + wt[:, 1:2] * y1_ref[0, 0].astype(F32)
           + wt[:, 2:3] * y2_ref[0, 0].astype(F32) + wt[:, 3:4] * y3_ref[0, 0].astype(F32))
    x2 = x1_ref[0] + gt2_ref[0] * moe
    o_ref[0] = x2 * lax.rsqrt(jnp.mean(x2 * x2, axis=-1, keepdims=True) + EPS) * nf_ref[...]


def _final(x1, yg, wt, mod3, norm_f):
    b, l, d = x1.shape
    tm = ROW_TILE
    tok = lambda width: pl.BlockSpec((1, tm, width), lambda i, j: (i, j, 0))
    ysel = lambda k: pl.BlockSpec((1, 1, tm, d), lambda i, j: (k, i, j, 0))
    return pl.pallas_call(
        _final_kernel,
        grid=(b, l // tm),
        in_specs=[tok(d)] + [ysel(k) for k in range(TOP_K)] + [tok(128), pl.BlockSpec((1, 1, d), lambda i, j: (i * 6 + 5, 0, 0)),
                                 pl.BlockSpec((1, d), lambda i, j: (0, 0))],
        out_specs=tok(d),
        out_shape=jax.ShapeDtypeStruct((b, l, d), F32),
        compiler_params=_cparams(("parallel", "parallel")),
        name="combine_final_norm",
    )(x1, *([yg] * TOP_K), wt, mod3, norm_f.reshape(1, d))


def _split_hi_lo(w):
    hi = w.astype(BF16)
    lo = (w - hi.astype(F32)).astype(BF16)
    return hi, lo


def kernel(x, c, ctx, c_ctx, w_mod, b_mod, norm1, w_in, s5_lam_re, s5_lam_im, s5_log_step, s5_b_re, s5_b_im, s5_c_re, s5_c_im, s5_d, s5_w_glu, s5_b_glu, gdn_conv, gdn_a_log, gdn_dt_bias, gdn_norm, w_branch_a, w_branch_b, w_out, norm2, w_router, b_router, w_gate_up, b_gate_up, w_down, b_down, norm_f):
    b, l, d = x.shape
    lc = ctx.shape[1]
    depth = w_mod.shape[0]
    assert depth == 1, "single-layer block: the context stream has no consumer after the token mixer"
    assert l % ROW_TILE == 0 and lc % ROW_TILE == 0 and lc & (lc - 1) == 0 and b <= 8
    ly = 0

    cc = jnp.zeros((16, d), F32).at[0:b].set(c).at[b].set(c_ctx)
    mod = _modulation(cc, w_mod[ly], b_mod[ly])
    mod3 = mod[0:b + 1].reshape((b + 1) * 6, 1, d)

    wi = w_in[ly]
    o_u, o_qkv, o_gate, o_ba, o_br = 0, 512, 2048, 2560, 2576
    w_ba = wi[:, o_ba:o_br]
    ba_hi, ba_lo = _split_hi_lo(w_ba)
    pad = lambda t: jnp.pad(t, ((0, 0), (0, 128 - t.shape[1])))
    w_main = jnp.concatenate([wi[:, o_u:o_ba].astype(BF16), wi[:, o_br:].astype(BF16), pad(ba_hi), pad(ba_lo)], axis=1)
    wt = jnp.concatenate([ba_hi.T, ba_lo.T], axis=0)
    ut, qkv, gate, br, ba, bat = _in_projection(x, ctx, mod3, norm1[ly], w_main, pad(ba_hi), wt, ba_hi.T)

    mats = _s5_matrices(s5_lam_re[ly], s5_lam_im[ly], s5_log_step[ly], s5_b_re[ly], s5_b_im[ly],
                        s5_c_re[ly], s5_c_im[ly], s5_d[ly])
    z = _s5_mixer(ut, l, mats)

    lt = l + lc
    nch = lt // GDN_CHUNK
    bat_chunks = jnp.transpose(bat.reshape(b, 16, nch, GDN_CHUNK), (0, 2, 1, 3))
    ea = jnp.exp(gdn_a_log[ly].astype(F32)).reshape(-1)
    dtb = gdn_dt_bias[ly].astype(F32).reshape(-1)
    prow = jnp.zeros((2, 128), F32).at[0, 8:16].set(ea).at[1, 8:16].set(dtb)
    pcol = jnp.stack([ea, dtb], axis=1)
    conv_w = jnp.zeros((8, qkv.shape[2]), F32).at[0:CONV_K].set(gdn_conv[ly].astype(F32))
    prep = _gdn_prepare(qkv, conv_w, ba, bat_chunks, prow, pcol, l)
    o_f, o_b = _gdn_scan(prep, l)

    wr = jnp.pad(w_router[ly].astype(F32), ((0, 0), (0, 128 - N_EXPERTS)))
    brt = jnp.pad(b_router[ly].astype(F32), (0, 128 - N_EXPERTS)).reshape(1, 128)
    x1, h2, logits = _merge(x, z, o_f, o_b, gate, br, mod3,
                            s5_w_glu[ly].astype(BF16), s5_b_glu[ly].astype(F32).reshape(1, -1),
                            gdn_norm[ly].astype(F32).reshape(1, -1),
                            w_branch_a[ly].astype(BF16), w_branch_b[ly].astype(BF16), w_out[ly].astype(BF16),
                            norm2[ly].astype(F32).reshape(1, d), wr, brt)

    n_tok = b * l
    weights, src_token, pos, tile_expert, n_used = _route(logits.reshape(n_tok, 128)[:, 0:N_EXPERTS], n_tok)
    h2f = h2.reshape(2 * n_tok, d)
    xs = [h2f[piece] for piece in jnp.split(src_token, src_token.shape[0] // n_tok)]
    ys = _moe_experts(xs, tile_expert, n_used, w_gate_up[ly], b_gate_up[ly], w_down[ly], b_down[ly])
    yg = ys[pos.T.reshape(-1)].reshape(TOP_K, b, l, d)
    wt4 = jnp.pad(weights, ((0, 0), (0, 128 - TOP_K))).reshape(b, l, 128)
    return _final(x1, yg, wt4, mod3, norm_f)
```

```python
import functools

import numpy as np
import jax
import jax.numpy as jnp
from jax import lax
from jax.experimental import pallas as pl
from jax.experimental.pallas import tpu as pltpu

F32 = jnp.float32
BF16 = jnp.bfloat16
HIGHEST = lax.Precision.HIGHEST

EPS = 1e-6
GRID_W = 64

S5_GROUP = 16
S5_STATE = 64
S5_Q = 16

GDN_HEADS = 4
GDN_DK = 128
GDN_CHUNK = 64
CONV_K = 5

N_EXPERTS = 32
TOP_K = 4
SWIGLU_LIMIT = 7.0
SWIGLU_ALPHA = 1.702

ROW_TILE = 256
MERGE_TILE = 512
MOE_TILE = 512
VMEM_LIMIT = 56 * 1024 * 1024


def _cparams(sem):
    return pltpu.CompilerParams(dimension_semantics=sem, vmem_limit_bytes=VMEM_LIMIT)


def _sigmoid(v):
    return 1.0 / (1.0 + jnp.exp(-v))


def _softplus(v):
    return jnp.maximum(v, 0.0) + jnp.log(1.0 + jnp.exp(-jnp.abs(v)))


def _dot(a, b):
    return jnp.dot(a, b, preferred_element_type=F32)


def _dot_nt(a, b):
    return lax.dot_general(a, b, (((1,), (1,)), ((), ())), preferred_element_type=F32)


def _dot_tn(a, b):
    return lax.dot_general(a, b, (((0,), (0,)), ((), ())), preferred_element_type=F32)


def _mod_kernel(c_ref, w_ref, b_ref, o_ref):
    cc = c_ref[...]
    s = cc * _sigmoid(cc)
    o_ref[...] = jnp.dot(s, w_ref[...], precision=HIGHEST, preferred_element_type=F32) + b_ref[...]


def _modulation(cc, w_mod, b_mod):
    rows, d = cc.shape
    n = w_mod.shape[1]
    bn = d
    return pl.pallas_call(
        _mod_kernel,
        grid=(n // bn,),
        in_specs=[pl.BlockSpec((rows, d), lambda j: (0, 0)),
                  pl.BlockSpec((d, bn), lambda j: (0, j)),
                  pl.BlockSpec((1, bn), lambda j: (0, j))],
        out_specs=pl.BlockSpec((rows, bn), lambda j: (0, j)),
        out_shape=jax.ShapeDtypeStruct((rows, n), F32),
        compiler_params=_cparams(("arbitrary",)),
        name="adaln_mod",
    )(cc, w_mod, b_mod.reshape(1, n))


def _inproj_kernel(x_ref, c_ref, sc_ref, sh_ref, g_ref, w_ref, wlo_ref, wt_ref, wtlo_ref,
                   ut_ref, qkv_ref, gate_ref, br_ref, ba_ref, bat_ref, us_ref, *, n_lat_blocks):
    j = pl.program_id(1)
    is_ctx = j >= n_lat_blocks
    x = jnp.where(is_ctx, c_ref[0], x_ref[0])
    ms = jnp.mean(x * x, axis=-1, keepdims=True)
    y = x * lax.rsqrt(ms + EPS) * g_ref[...]
    h = y * (1.0 + sc_ref[0]) + sh_ref[0]
    h_hi = h.astype(BF16)
    h_lo = (h - h_hi.astype(F32)).astype(BF16)
    z = _dot(h_hi, w_ref[...])
    nchunk = us_ref.shape[1] // S5_Q
    for c in range(us_ref.shape[0]):
        us_ref[c] = z[:, c * 128:(c + 1) * 128]
        for t in range(S5_Q):
            ut_ref[t, 0, :, c * 128:(c + 1) * 128] = us_ref[c, pl.ds(t, nchunk, stride=S5_Q), :].astype(BF16)
    qkv_ref[0] = z[:, 512:2048].astype(BF16)
    gate_ref[0] = z[:, 2048:2560].astype(BF16)
    br_ref[0] = z[:, 2560:4608].astype(BF16)
    ba_ref[0] = z[:, 4608:4736] + z[:, 4736:4864] + _dot(h_lo, wlo_ref[...])
    rt = _dot_nt(wt_ref[...], h_hi)
    bat_ref[0] = rt[0:16] + rt[16:32] + _dot_nt(wtlo_ref[...], h_lo)


def _in_projection(x, ctx, mod3, norm1, w_main, w_lo, wt, wt_lo):
    b, l, d = x.shape
    lc = ctx.shape[1]
    tm = ROW_TILE
    nl, nc = l // tm, lc // tm
    nb = mod3.shape[0] // 6 - 1
    lt = l + lc
    ncols = w_main.shape[1]

    def x_map(i, j):
        return (i, jnp.minimum(j, nl - 1), 0)

    def c_map(i, j):
        return (i, jnp.maximum(j - nl, 0), 0)

    def mod_map(k):
        return lambda i, j: (jnp.where(j >= nl, nb, i) * 6 + k, 0, 0)

    def tok(width):
        return pl.BlockSpec((1, tm, width), lambda i, j: (i, j, 0))

    const2 = lambda i, j: (0, 0)
    outs = pl.pallas_call(
        functools.partial(_inproj_kernel, n_lat_blocks=nl),
        grid=(b, nl + nc),
        in_specs=[pl.BlockSpec((1, tm, d), x_map),
                  pl.BlockSpec((1, tm, d), c_map),
                  pl.BlockSpec((1, 1, d), mod_map(1)),
                  pl.BlockSpec((1, 1, d), mod_map(0)),
                  pl.BlockSpec((1, d), const2),
                  pl.BlockSpec((d, ncols), const2),
                  pl.BlockSpec((d, 128), const2),
                  pl.BlockSpec((32, d), const2),
                  pl.BlockSpec((16, d), const2)],
        out_specs=[pl.BlockSpec((S5_Q, 1, tm // S5_Q, 512), lambda i, j: (0, i, j, 0)),
                   tok(1536), tok(512), tok(2048), tok(128),
                   pl.BlockSpec((1, 16, tm), lambda i, j: (i, 0, j))],
        out_shape=[jax.ShapeDtypeStruct((S5_Q, b, lt // S5_Q, 512), BF16),
                   jax.ShapeDtypeStruct((b, lt, 1536), BF16),
                   jax.ShapeDtypeStruct((b, lt, 512), BF16),
                   jax.ShapeDtypeStruct((b, lt, 2048), BF16),
                   jax.ShapeDtypeStruct((b, lt, 128), F32),
                   jax.ShapeDtypeStruct((b, 16, lt), F32)],
        scratch_shapes=[pltpu.VMEM((4, tm, 128), F32)],
        compiler_params=_cparams(("parallel", "arbitrary")),
        name="in_projection",
    )(x, ctx, mod3, mod3, norm1.reshape(1, d), w_main, w_lo, wt, wt_lo)
    return outs


def _s5_matrices(lam_re, lam_im, log_step, b_re, b_im, c_re, c_im, d_skip):
    q = S5_Q
    lr = jnp.minimum(lam_re.astype(F32), -1e-4)
    li = lam_im.astype(F32)
    dt = jnp.exp(log_step.astype(F32))[..., None]
    pw = jnp.arange(q + 1, dtype=F32)[:, None, None, None]
    mag = jnp.exp(pw * (lr * dt))
    ang = pw * (li * dt)
    pr, pi = mag * jnp.cos(ang), mag * jnp.sin(ang)
    ar, ai = pr[1], pi[1]
    den = lr * lr + li * li
    fr = ((ar - 1.0) * lr + ai * li) / den
    fi = (ai * lr - (ar - 1.0) * li) / den
    br, bi = b_re.astype(F32), b_im.astype(F32)
    bbr = fr[..., None] * br - fi[..., None] * bi
    bbi = fr[..., None] * bi + fi[..., None] * br
    cr, ci = c_re.astype(F32), c_im.astype(F32)
    car = cr[None] * pr[:, :, :, None, :] - ci[None] * pi[:, :, :, None, :]
    cai = cr[None] * pi[:, :, :, None, :] + ci[None] * pr[:, :, :, None, :]
    bbr_t = jnp.swapaxes(bbr, -1, -2)[None, :, :, None]
    bbi_t = jnp.swapaxes(bbi, -1, -2)[None, :, :, None]
    kern = jnp.sum(car[..., None, :] * bbr_t - cai[..., None, :] * bbi_t, axis=-1)
    g = lr.shape[1]
    jj = np.arange(q)[:, None]
    ii = np.arange(q)[None, :]
    df = ii - jj
    mf = jnp.where((df >= 0)[:, :, None, None, None], kern[:q, 0][np.clip(df, 0, q - 1)], 0.0)
    mb = jnp.where((df <= 0)[:, :, None, None, None], kern[:q, 1][np.clip(-df, 0, q - 1)], 0.0)
    m = jnp.transpose(mf + mb, (2, 0, 4, 1, 3))
    skip = (jnp.eye(q, dtype=F32)[None, :, None, :, None]
            * jnp.eye(S5_GROUP, dtype=F32)[None, None, :, None, :]
            * d_skip.astype(F32).reshape(g, 1, S5_GROUP, 1, 1))
    m = (m + skip).reshape(g, q * S5_GROUP, q * S5_GROUP)

    def p_block(powers, d):
        ppr, ppi = pr[powers, d], pi[powers, d]
        re = ppr[..., None] * bbr[d][None] - ppi[..., None] * bbi[d][None]
        im = ppr[..., None] * bbi[d][None] + ppi[..., None] * bbr[d][None]
        to = lambda t: jnp.transpose(t, (1, 0, 3, 2)).reshape(g, q * S5_GROUP, S5_STATE)
        return to(re), to(im)

    pf_re, pf_im = p_block(np.arange(q - 1, -1, -1), 0)
    pb_re, pb_im = p_block(np.arange(q), 1)
    p = jnp.concatenate([pf_re, pb_re, pf_im, pb_im], axis=-1)

    def r_block(powers, d):
        to = lambda t: jnp.transpose(t, (1, 3, 0, 2)).reshape(g, S5_STATE, q * S5_GROUP)
        return to(car[powers, d]), to(-cai[powers, d])

    rf_re, rf_im = r_block(np.arange(1, q + 1), 0)
    rb_re, rb_im = r_block(np.arange(q, 0, -1), 1)
    r = jnp.concatenate([rf_re, rb_re, rf_im, rb_im], axis=1)
    a16 = jnp.stack([jnp.concatenate([pr[q, 0], pr[q, 1]], axis=-1),
                     jnp.concatenate([pi[q, 0], pi[q, 1]], axis=-1)], axis=1)
    return m.astype(BF16), p.astype(BF16), r.astype(BF16), a16


S5_LANE_GROUPS = 8
S5_BATCH = 4


def _s5_kernel(ut_ref, m_ref, p_ref, r_ref, a_ref, zt_ref,
               pi_ref, x_ref, xp_ref, v_ref, sk_ref, srb_ref, y_ref, yp_ref, *, n_lat, n_ctx, bb):
    q, ng = S5_Q, S5_LANE_GROUPS
    qc = q * S5_GROUP
    width = ng * qc
    n_all = n_lat + n_ctx
    rows_lat = n_lat * bb

    @pl.when(jnp.logical_and(pl.program_id(0) == 0, pl.program_id(1) == 0))
    def _():
        src = lax.broadcasted_iota(jnp.int32, (width, qc), 0)
        dst = lax.broadcasted_iota(jnp.int32, (width, qc), 1)
        dest_in_group = ((src >> 7) << 4) + (src & 15)
        grp = (src >> 4) & (ng - 1)
        for g in range(ng):
            hit = jnp.logical_and(grp == g, dest_in_group == dst)
            pi_ref[:, g * qc:(g + 1) * qc] = jnp.where(hit, 1.0, 0.0).astype(BF16)

    for b in range(bb):
        for t in range(q):
            x_ref[t, pl.ds(b, n_all, stride=bb), :] = ut_ref[t, b].astype(F32)
    xb = jnp.concatenate([x_ref[t] for t in range(q)], axis=1).astype(BF16)
    for g in range(ng):
        cols = slice(g * qc, (g + 1) * qc)
        xg = _dot(xb, pi_ref[:, cols]).astype(BF16)
        xp_ref[:, cols] = xg
        v_ref[:, cols] = _dot(xg, p_ref[g])

    a = a_ref[...]
    a1 = jnp.concatenate([a[g, 0:1] for g in range(ng) for _ in range(2)], axis=1)
    a2 = jnp.concatenate([s * a[g, 1:2] for g in range(ng) for s in (-1.0, 1.0)], axis=1)
    fwd_lane = (lax.broadcasted_iota(jnp.int32, (bb, width), 1) & 127) < S5_STATE

    def partner(s):
        tiles = [s[:, i * 128:(i + 1) * 128] for i in range(2 * ng)]
        return jnp.concatenate([tiles[i ^ 1] for i in range(2 * ng)], axis=1)

    def advance(s, vf, vb):
        return a1 * s + a2 * partner(s) + jnp.where(fwd_lane, vf, vb)

    def scan(base, n, s0, store):
        def body(t, s):
            rf = pl.multiple_of((base + 2 * t) * bb, 2 * bb)
            rb = pl.multiple_of((base + n - 2 - 2 * t) * bb, 2 * bb)
            vf = v_ref[pl.ds(rf, 2 * bb), :]
            vb = v_ref[pl.ds(rb, 2 * bb), :]
            s1 = advance(s, vf[0:bb], vb[bb:2 * bb])
            s2 = advance(s1, vf[bb:2 * bb], vb[0:bb])
            if store:
                sk_ref[pl.ds(rf, 2 * bb), :] = jnp.concatenate([s, s1], axis=0)
                srb_ref[pl.ds(rb, 2 * bb), :] = jnp.concatenate([s1, s], axis=0)
            return s2
        return lax.fori_loop(0, n // 2, body, s0)

    s_ctx = scan(n_lat, n_ctx, jnp.zeros((bb, width), F32), False)
    scan(0, n_lat, s_ctx, True)
    lane = (lax.broadcasted_iota(jnp.int32, (rows_lat, width), 1) & 127) < S5_STATE
    st = jnp.where(lane, sk_ref[...], srb_ref[...]).astype(BF16)
    for g in range(ng):
        cols = slice(g * qc, (g + 1) * qc)
        y = _dot(xp_ref[0:rows_lat, cols], m_ref[g]) + _dot(st[:, cols], r_ref[g])
        y_ref[:, cols] = jax.nn.gelu(y).astype(BF16)
    yb = y_ref[...]
    for t in range(q):
        yp_ref[...] = _dot_nt(yb, pi_ref[t * 128:(t + 1) * 128, :])
        for b in range(bb):
            zt_ref[t, b] = yp_ref[pl.ds(b, n_lat, stride=bb), :].astype(BF16)


def _s5_mixer(ut, l, mats):
    m, p, r, a16 = mats
    q, b, n_all, width = ut.shape
    n_lat = l // q
    ng = S5_LANE_GROUPS
    qc = q * S5_GROUP
    bb = min(S5_BATCH, b)
    nj = width // 128
    wspec = pl.BlockSpec((ng, qc, qc), lambda j, i: (j, 0, 0))
    return pl.pallas_call(
        functools.partial(_s5_kernel, n_lat=n_lat, n_ctx=n_all - n_lat, bb=bb),
        grid=(nj, b // bb),
        in_specs=[pl.BlockSpec((q, bb, n_all, 128), lambda j, i: (0, i, 0, j)),
                  wspec, wspec, wspec,
                  pl.BlockSpec((ng, 2, 2 * S5_STATE), lambda j, i: (j, 0, 0))],
        out_specs=pl.BlockSpec((q, bb, n_lat, 128), lambda j, i: (0, i, 0, j)),
        out_shape=jax.ShapeDtypeStruct((q, b, n_lat, width), BF16),
        scratch_shapes=[pltpu.VMEM((ng * qc, ng * qc), BF16),
                        pltpu.VMEM((q, n_all * bb, 128), F32),
                        pltpu.VMEM((n_all * bb, ng * qc), BF16),
                        pltpu.VMEM((n_all * bb, ng * qc), F32),
                        pltpu.VMEM((n_lat * bb, ng * qc), F32),
                        pltpu.VMEM((n_lat * bb, ng * qc), F32),
                        pltpu.VMEM((n_lat * bb, ng * qc), BF16),
                        pltpu.VMEM((n_lat * bb, 128), F32)],
        compiler_params=_cparams(("arbitrary", "arbitrary")),
        name="s5_scan",
    )(ut, m, p, r, a16)


def _inv_unit_triangular_many(mats):
    c = mats[0].shape[0]
    eye = (lax.broadcasted_iota(jnp.int32, (c, c), 0) == lax.broadcasted_iota(jnp.int32, (c, c), 1)).astype(F32)
    prods = [eye - a for a in mats]
    pows = [a.astype(BF16) for a in mats]
    for _ in range(int(np.log2(c)) - 1):
        pows = [_dot(a, a).astype(BF16) for a in pows]
        prods = [p + _dot(p.astype(BF16), a) for p, a in zip(prods, pows)]
    return prods


def _gdn_prep_kernel(qkv_ref, cw_ref, ba_ref, bat_ref, prow_ref, pcol_ref,
                     u_ref, wq_ref, qk_ref, kdt_ref, dl_ref, *, n_lat_blocks, ctx_len):
    j = pl.program_id(1)
    rows = qkv_ref.shape[1]
    c = GDN_CHUNK
    nchunk = rows // c
    seg = jnp.where(j >= n_lat_blocks, ctx_len, GRID_W)
    x = qkv_ref[0].astype(F32)
    pos = lax.broadcasted_iota(jnp.int32, (rows, 1), 0) & (seg - 1)
    acc = x * cw_ref[2:3, :]
    for s in (-2, -1, 1, 2):
        shifted = pltpu.roll(x, (-s) % rows, 0)
        ok = jnp.logical_and(pos + s >= 0, pos + s < seg)
        acc = acc + jnp.where(ok, shifted, 0.0) * cw_ref[2 + s:3 + s, :]
    act = acc * _sigmoid(acc)

    ii = lax.broadcasted_iota(jnp.int32, (c, c), 0)
    jj = lax.broadcasted_iota(jnp.int32, (c, c), 1)
    incl = (ii >= jj, ii <= jj)
    strict = (ii > jj, ii < jj)
    tri_low = incl[0].astype(F32)
    tri_up = incl[1].astype(F32)
    eye_bf = (ii == jj).astype(BF16)
    ones = jnp.ones((c, c), F32)
    lane = lax.broadcasted_iota(jnp.int32, (c, 128), 1)
    prow = prow_ref[...]
    pcol = pcol_ref[...]
    hp = dict(precision=HIGHEST, preferred_element_type=F32)

    gates = []
    for ci in range(nchunk):
        ba = ba_ref[0, ci * c:(ci + 1) * c, :]
        g_all = -prow[0:1, :] * _softplus(ba + prow[1:2, :])
        g_all = jnp.where(jnp.logical_and(lane >= 8, lane < 16), g_all, 0.0)
        bat = bat_ref[0, ci]
        g_row = -pcol[:, 0:1] * _softplus(bat[8:16, :] + pcol[:, 1:2])
        gates.append(dict(beta=_sigmoid(ba),
                          gc=(jnp.dot(tri_low, g_all, **hp), jnp.dot(tri_up, g_all, **hp)),
                          gtot=jnp.dot(ones, g_all, **hp),
                          gr=(jnp.dot(g_row, tri_up, **hp), jnp.dot(g_row, tri_low, **hp))))

    qn, kn, vv = [], [], []
    for h in range(GDN_HEADS):
        qh = act[:, h * 128:(h + 1) * 128]
        kh = act[:, 512 + h * 128:512 + (h + 1) * 128]
        qn.append(qh * lax.rsqrt(jnp.sum(qh * qh, axis=-1, keepdims=True) + EPS) * (GDN_DK ** -0.5))
        kn.append(kh * lax.rsqrt(jnp.sum(kh * kh, axis=-1, keepdims=True) + EPS))
        vv.append(act[:, 1024 + h * 128:1024 + (h + 1) * 128])

    pairs = [(ci, h) for ci in range(nchunk) for h in range(GDN_HEADS)]
    sl = lambda t, ci: t[ci * c:(ci + 1) * c]
    kb = {p: sl(kn[p[1]], p[0]).astype(BF16) for p in pairs}
    kk = {p: _dot_nt(kb[p], kb[p]) for p in pairs}
    qkm = {p: _dot_nt(sl(qn[p[1]], p[0]).astype(BF16), kb[p]) for p in pairs}

    probs = [(ci, h, d) for ci in range(nchunk) for h in range(GDN_HEADS) for d in range(2)]
    beta, gc, gt, decay, a_mats = {}, {}, {}, {}, []
    for (ci, h, d) in probs:
        col = d * GDN_HEADS + h
        gi = gates[ci]
        beta[ci, h, d] = gi["beta"][:, col:col + 1]
        gc[ci, h, d] = gi["gc"][d][:, 8 + col:9 + col]
        gt[ci, h, d] = gi["gtot"][:, 8 + col:9 + col]
        grow = gi["gr"][d][col:col + 1, :]
        decay[ci, h, d] = jnp.exp(jnp.where(incl[d], gc[ci, h, d] - grow, -jnp.inf))
        a_mats.append(jnp.where(strict[d], beta[ci, h, d] * kk[ci, h] * decay[ci, h, d], 0.0))
    tinv = _inv_unit_triangular_many(a_mats)

    eg, sols, kdts = {}, {}, {}
    for n, (ci, h, d) in enumerate(probs):
        p = (ci, h, d)
        eg[p] = jnp.exp(gc[p])
        kh = sl(kn[h], ci)
        rhs = jnp.concatenate([sl(vv[h], ci) * beta[p], kh * (beta[p] * eg[p])], axis=-1).astype(BF16)
        sols[p] = _dot(tinv[n].astype(BF16), rhs)
        kdts[p] = _dot_tn((kh * jnp.exp(gt[p] - gc[p])).astype(BF16), eye_bf)
    for (ci, h, d) in probs:
        p = (ci, h, d)
        u_ref[0, d, h, ci] = sols[p][:, 0:128]
        wq_ref[0, d, h, ci, 0:c, :] = sols[p][:, 128:256].astype(BF16)
        wq_ref[0, d, h, ci, c:2 * c, :] = (sl(qn[h], ci) * eg[p]).astype(BF16)
        qk_ref[0, d, h, ci] = (qkm[ci, h] * decay[p]).astype(BF16)
        kdt_ref[0, d, h, ci] = kdts[p].astype(BF16)
        dl_ref[0, d, h, ci] = jnp.broadcast_to(jnp.exp(gt[p][0:8, :]), (8, 128))


def _gdn_prepare(qkv, conv_w, ba, bat_chunks, prow, pcol, l):
    b, lt, width = qkv.shape
    tm = ROW_TILE
    c = GDN_CHUNK
    cpb = tm // c
    nblk = lt // tm
    nch = lt // c
    hd = (b, 2, GDN_HEADS, nch)

    def blk(shape_tail, dtype):
        return (pl.BlockSpec((1, 2, GDN_HEADS, cpb) + shape_tail, lambda i, j: (i, 0, 0, j, 0, 0)),
                jax.ShapeDtypeStruct(hd + shape_tail, dtype))

    specs = [blk((c, 128), F32), blk((2 * c, 128), BF16), blk((c, c), BF16), blk((GDN_DK, c), BF16),
             blk((8, 128), F32)]
    return pl.pallas_call(
        functools.partial(_gdn_prep_kernel, n_lat_blocks=l // tm, ctx_len=lt - l),
        grid=(b, nblk),
        in_specs=[pl.BlockSpec((1, tm, width), lambda i, j: (i, j, 0)),
                  pl.BlockSpec((8, width), lambda i, j: (0, 0)),
                  pl.BlockSpec((1, tm, 128), lambda i, j: (i, j, 0)),
                  pl.BlockSpec((1, cpb, 16, c), lambda i, j: (i, j, 0, 0)),
                  pl.BlockSpec((2, 128), lambda i, j: (0, 0)),
                  pl.BlockSpec((8, 2), lambda i, j: (0, 0))],
        out_specs=[s for s, _ in specs],
        out_shape=[o for _, o in specs],
        compiler_params=_cparams(("parallel", "parallel")),
        name="gdn_prepare",
    )(qkv, conv_w, ba, bat_chunks, prow, pcol)


SCAN_GROUP = 2


def _gdn_scan_kernel(uf, wqf, qkf, kdtf, dlf, ub, wqb, qkb, kdtb, dlb, of_ref, ob_ref, s_ref):
    t = pl.program_id(0)
    nb = s_ref.shape[0]
    c = GDN_CHUNK

    @pl.when(t == 0)
    def _():
        s_ref[...] = jnp.zeros_like(s_ref)

    refs = ((uf, wqf, qkf, kdtf, dlf, of_ref), (ub, wqb, qkb, kdtb, dlb, ob_ref))
    for b0 in range(0, nb, SCAN_GROUP):
        chains = [(bi, d, h) for bi in range(b0, min(b0 + SCAN_GROUP, nb)) for d in range(2)
                  for h in range(GDN_HEADS)]
        s = {k: s_ref[k[0], k[1], k[2]] for k in chains}
        sb = {k: s[k].astype(BF16) for k in chains}
        r = {k: _dot(refs[k[1]][1][k[0], 0, k[2], 0], sb[k]) for k in chains}
        vb = {k: (refs[k[1]][0][k[0], 0, k[2], 0] - r[k][0:c]).astype(BF16) for k in chains}
        o = {k: r[k][c:2 * c] + _dot(refs[k[1]][2][k[0], 0, k[2], 0], vb[k]) for k in chains}
        sn = {k: s[k] * refs[k[1]][4][k[0], 0, k[2], 0][0:1, :] + _dot(refs[k[1]][3][k[0], 0, k[2], 0], vb[k])
              for k in chains}
        for k in chains:
            s_ref[k[0], k[1], k[2]] = sn[k]
            refs[k[1]][5][k[0], :, k[2] * 128:(k[2] + 1) * 128] = o[k]


def _gdn_scan(prep, l):
    u = prep[0]
    b, _, heads, nch, c, _ = u.shape
    n_lat = l // c
    n_ctx = nch - n_lat
    lt = nch * c

    def fwd_chunk(t):
        return jnp.where(t < n_ctx, n_lat + t, t - n_ctx)

    def bwd_chunk(t):
        return nch - 1 - t

    def spec(arr, chunk_of, d):
        tail = arr.shape[4:]
        return pl.BlockSpec((b, 1, heads, 1) + tail, lambda t: (0, d, 0, chunk_of(t), 0, 0))

    ins = [spec(a, fwd_chunk, 0) for a in prep] + [spec(a, bwd_chunk, 1) for a in prep]
    width = heads * 128
    return pl.pallas_call(
        _gdn_scan_kernel,
        grid=(nch,),
        in_specs=ins,
        out_specs=[pl.BlockSpec((b, c, width), lambda t: (0, fwd_chunk(t), 0)),
                   pl.BlockSpec((b, c, width), lambda t: (0, bwd_chunk(t), 0))],
        out_shape=[jax.ShapeDtypeStruct((b, lt, width), F32)] * 2,
        scratch_shapes=[pltpu.VMEM((b, 2, heads, GDN_DK, 128), F32)],
        compiler_params=_cparams(("arbitrary",)),
        name="gdn_scan",
    )(*prep, *prep)


def _merge_kernel(x_ref, z_ref, of_ref, ob_ref, gate_ref, br_ref, gt1_ref, sc2_ref, sh2_ref,
                  wglu_ref, bglu_ref, gnorm_ref, wa_ref, wb_ref, wo_ref, n2_ref, wr_ref, wrhi_ref, brt_ref,
                  x1_ref, h2_ref, lg_ref, zs_ref):
    nchunk = zs_ref.shape[1] // S5_Q
    for t in range(S5_Q):
        zt = z_ref[t, 0].astype(F32)
        for c in range(zs_ref.shape[0]):
            zs_ref[c, pl.ds(t, nchunk, stride=S5_Q), :] = zt[:, c * 128:(c + 1) * 128]
    zf = jnp.concatenate([zs_ref[c] for c in range(zs_ref.shape[0])], axis=1)
    z = zf.astype(BF16)
    ya = zf * _sigmoid(_dot(z, wglu_ref[...]) + bglu_ref[...])
    o = of_ref[0] + ob_ref[0]
    gate = gate_ref[0].astype(F32)
    parts = []
    for h in range(GDN_HEADS):
        oh = o[:, h * 128:(h + 1) * 128]
        parts.append(oh * lax.rsqrt(jnp.mean(oh * oh, axis=-1, keepdims=True) + EPS) * gnorm_ref[...])
    yb = jnp.concatenate(parts, axis=-1) * (gate * _sigmoid(gate))
    br = br_ref[0].astype(F32)
    d = x_ref.shape[2]
    ga = _sigmoid(br[:, 0:d])
    gb = _sigmoid(br[:, d:2 * d])
    m = ga * _dot(ya.astype(BF16), wa_ref[...]) + gb * _dot(yb.astype(BF16), wb_ref[...])
    mix = _dot(m.astype(BF16), wo_ref[...])
    x1 = x_ref[0] + gt1_ref[0] * mix
    x1_ref[0] = x1
    y2 = x1 * lax.rsqrt(jnp.mean(x1 * x1, axis=-1, keepdims=True) + EPS) * n2_ref[...]
    h2 = y2 * (1.0 + sc2_ref[0]) + sh2_ref[0]
    h2b = h2.astype(BF16)
    h2_ref[0, 0] = h2b
    h2_ref[1, 0] = h2b
    h2_lo = (h2 - h2b.astype(F32)).astype(BF16)
    rl = _dot(h2b, wr_ref[...])
    lg_ref[0] = rl[:, 0:128] + rl[:, 128:256] + _dot(h2_lo, wrhi_ref[...]) + brt_ref[...]


def _merge(x, z, o_f, o_b, gate, br, mod3, wglu, bglu, gnorm, wa, wb, wo, norm2, wr, wr_hi, brt):
    b, l, d = x.shape
    tm = min(MERGE_TILE, l)
    tok = lambda width: pl.BlockSpec((1, tm, width), lambda i, j: (i, j, 0))
    modspec = lambda k: pl.BlockSpec((1, 1, d), lambda i, j: (i * 6 + k, 0, 0))
    full = lambda arr: pl.BlockSpec(arr.shape, lambda i, j: (0,) * arr.ndim)
    consts = [wglu, bglu, gnorm, wa, wb, wo, norm2, wr, wr_hi, brt]
    return pl.pallas_call(
        _merge_kernel,
        grid=(b, l // tm),
        in_specs=[tok(d), pl.BlockSpec((S5_Q, 1, tm // S5_Q, 512), lambda i, j: (0, i, j, 0)),
                  tok(512), tok(512), tok(512), tok(2 * d),
                  modspec(2), modspec(4), modspec(3)] + [full(a) for a in consts],
        out_specs=[tok(d), pl.BlockSpec((2, 1, tm, d), lambda i, j: (0, i, j, 0)), tok(128)],
        out_shape=[jax.ShapeDtypeStruct((b, l, d), F32),
                   jax.ShapeDtypeStruct((2, b, l, d), BF16),
                   jax.ShapeDtypeStruct((b, l, 128), F32)],
        scratch_shapes=[pltpu.VMEM((4, tm, 128), F32)],
        compiler_params=_cparams(("parallel", "parallel")),
        name="branch_merge",
    )(x, z, o_f, o_b, gate, br, mod3, mod3, mod3, *consts)


def _moe_kernel(te_ref, nu_ref, x_ref, wgu_ref, bgu_ref, wd_ref, bd_ref, y_ref, wgu_bf, wd_bf):
    i = pl.program_id(0)
    changed = jnp.logical_or(i == 0, te_ref[i] != te_ref[jnp.maximum(i - 1, 0)])

    @pl.when(jnp.logical_and(changed, i < nu_ref[0]))
    def _():
        wgu_bf[...] = wgu_ref[0].astype(BF16)
        wd_bf[...] = wd_ref[0].astype(BF16)

    @pl.when(i < nu_ref[0])
    def _():
        de = wd_ref.shape[1]
        gu = _dot(x_ref[...], wgu_bf[...]) + bgu_ref[0]
        gate = jnp.minimum(gu[:, 0:de], SWIGLU_LIMIT)
        up = jnp.clip(gu[:, de:2 * de], -SWIGLU_LIMIT, SWIGLU_LIMIT)
        act = (up + 1.0) * gate * _sigmoid(gate * SWIGLU_ALPHA)
        y_ref[...] = (_dot(act.astype(BF16), wd_bf[...]) + bd_ref[0]).astype(y_ref.dtype)

    @pl.when(i >= nu_ref[0])
    def _():
        y_ref[...] = jnp.zeros_like(y_ref)


def _moe_experts(xs, tile_expert, n_used, w_gate_up, b_gate_up, w_down, b_down):
    p, d = xs.shape
    tm = MOE_TILE
    e, _, n2 = w_gate_up.shape
    de = w_down.shape[1]
    grid_spec = pltpu.PrefetchScalarGridSpec(
        num_scalar_prefetch=2,
        grid=(p // tm,),
        in_specs=[pl.BlockSpec((tm, d), lambda i, te, nu: (jnp.minimum(i, jnp.maximum(nu[0] - 1, 0)), 0)),
                  pl.BlockSpec((1, d, n2), lambda i, te, nu: (te[i], 0, 0)),
                  pl.BlockSpec((1, 1, n2), lambda i, te, nu: (te[i], 0, 0)),
                  pl.BlockSpec((1, de, d), lambda i, te, nu: (te[i], 0, 0)),
                  pl.BlockSpec((1, 1, d), lambda i, te, nu: (te[i], 0, 0))],
        out_specs=pl.BlockSpec((tm, d), lambda i, te, nu: (i, 0)),
        scratch_shapes=[pltpu.VMEM((d, n2), BF16), pltpu.VMEM((de, d), BF16)],
    )
    return pl.pallas_call(
        _moe_kernel,
        grid_spec=grid_spec,
        out_shape=jax.ShapeDtypeStruct((p, d), BF16),
        compiler_params=_cparams(("arbitrary",)),
        name="moe_experts",
    )(tile_expert, n_used, xs, w_gate_up, b_gate_up.reshape(e, 1, n2), w_down, b_down.reshape(e, 1, d))


def _route(logits, n_tokens):
    tm = MOE_TILE
    top_val, top_idx = lax.top_k(logits, TOP_K)
    weights = jax.nn.softmax(top_val, axis=-1)
    flat_e = top_idx.reshape(-1).astype(jnp.int32)
    n_assign = flat_e.shape[0]
    iota = jnp.arange(n_assign, dtype=jnp.int32)
    sorted_e, order = lax.sort((flat_e, iota), num_keys=1, is_stable=True)
    counts = jnp.sum(jax.nn.one_hot(flat_e, N_EXPERTS, dtype=jnp.int32), axis=0)
    padded = ((counts + tm - 1) // tm) * tm
    pad_end = jnp.cumsum(padded)
    pad_start = pad_end - padded
    raw_start = jnp.cumsum(counts) - counts
    dest = pad_start[sorted_e] + (iota - raw_start[sorted_e])
    _, pos = lax.sort((order, dest), num_keys=1)
    pos = pos.reshape(n_tokens, TOP_K)
    n_rows = n_assign + N_EXPERTS * tm
    n_tiles = n_rows // tm
    n_used = (pad_end[-1] // tm).astype(jnp.int32)
    tile_start = jnp.arange(n_tiles, dtype=jnp.int32) * tm
    tile_expert = jnp.sum((tile_start[:, None] >= pad_end[None, :]).astype(jnp.int32), axis=1)
    last_e = jnp.sum((pad_end[-1] - 1 >= pad_end).astype(jnp.int32))
    tile_expert = jnp.minimum(tile_expert, last_e).astype(jnp.int32)
    lane = jnp.arange(tm, dtype=jnp.int32)[None, :]
    off = lane + (tile_start - pad_start[tile_expert])[:, None]
    valid = off < counts[tile_expert][:, None]
    sidx = jnp.clip(raw_start[tile_expert][:, None] + off, 0, n_assign - 1)
    filler = (tile_start[:, None] + lane) % n_tokens
    src_token = jnp.where(valid, (order // TOP_K)[sidx], filler).reshape(-1)
    return weights, src_token, pos, tile_expert, n_used.reshape(1)


def _final_kernel(x1_ref, y0_ref, y1_ref, y2_ref, y3_ref, wt_ref, gt2_ref, nf_ref, o_ref):
    wt = wt_ref[0]
    moe = (wt[:, 0:1] * y0_ref[0, 0].astype(F32) + wt[:, 1:2] * y1_ref[0, 0].astype(F32)
           + wt[:, 2:3] * y2_ref[0, 0].astype(F32) + wt[:, 3:4] * y3_ref[0, 0].astype(F32))
    x2 = x1_ref[0] + gt2_ref[0] * moe
    o_ref[0] = x2 * lax.rsqrt(jnp.mean(x2 * x2, axis=-1, keepdims=True) + EPS) * nf_ref[...]


def _final(x1, yg, wt, mod3, norm_f):
    b, l, d = x1.shape
    tm = ROW_TILE
    tok = lambda width: pl.BlockSpec((1, tm, width), lambda i, j: (i, j, 0))
    ysel = lambda k: pl.BlockSpec((1, 1, tm, d), lambda i, j: (k, i, j, 0))
    return pl.pallas_call(
        _final_kernel,
        grid=(b, l // tm),
        in_specs=[tok(d)] + [ysel(k) for k in range(TOP_K)] + [tok(128), pl.BlockSpec((1, 1, d), lambda i, j: (i * 6 + 5, 0, 0)),
                                 pl.BlockSpec((1, d), lambda i, j: (0, 0))],
        out_specs=tok(d),
        out_shape=jax.ShapeDtypeStruct((b, l, d), F32),
        compiler_params=_cparams(("parallel", "parallel")),
        name="combine_final_norm",
    )(x1, *([yg] * TOP_K), wt, mod3, norm_f.reshape(1, d))


def _split_hi_lo(w):
    hi = w.astype(BF16)
    lo = (w - hi.astype(F32)).astype(BF16)
    return hi, lo


def kernel(x, c, ctx, c_ctx, w_mod, b_mod, norm1, w_in, s5_lam_re, s5_lam_im, s5_log_step, s5_b_re, s5_b_im, s5_c_re, s5_c_im, s5_d, s5_w_glu, s5_b_glu, gdn_conv, gdn_a_log, gdn_dt_bias, gdn_norm, w_branch_a, w_branch_b, w_out, norm2, w_router, b_router, w_gate_up, b_gate_up, w_down, b_down, norm_f):
    b, l, d = x.shape
    lc = ctx.shape[1]
    depth = w_mod.shape[0]
    assert depth == 1, "single-layer block: the context stream has no consumer after the token mixer"
    assert l % ROW_TILE == 0 and lc % ROW_TILE == 0 and lc & (lc - 1) == 0 and b <= 8
    ly = 0

    cc = jnp.zeros((16, d), F32).at[0:b].set(c).at[b].set(c_ctx)
    mod = _modulation(cc, w_mod[ly], b_mod[ly])
    mod3 = mod[0:b + 1].reshape((b + 1) * 6, 1, d)

    wi = w_in[ly]
    o_u, o_qkv, o_gate, o_ba, o_br = 0, 512, 2048, 2560, 2576
    w_ba = wi[:, o_ba:o_br]
    ba_hi, ba_lo = _split_hi_lo(w_ba)
    pad = lambda t: jnp.pad(t, ((0, 0), (0, 128 - t.shape[1])))
    w_main = jnp.concatenate([wi[:, o_u:o_ba].astype(BF16), wi[:, o_br:].astype(BF16), pad(ba_hi), pad(ba_lo)], axis=1)
    wt = jnp.concatenate([ba_hi.T, ba_lo.T], axis=0)
    ut, qkv, gate, br, ba, bat = _in_projection(x, ctx, mod3, norm1[ly], w_main, pad(ba_hi), wt, ba_hi.T)

    mats = _s5_matrices(s5_lam_re[ly], s5_lam_im[ly], s5_log_step[ly], s5_b_re[ly], s5_b_im[ly],
                        s5_c_re[ly], s5_c_im[ly], s5_d[ly])
    z = _s5_mixer(ut, l, mats)

    lt = l + lc
    nch = lt // GDN_CHUNK
    bat_chunks = jnp.transpose(bat.reshape(b, 16, nch, GDN_CHUNK), (0, 2, 1, 3))
    ea = jnp.exp(gdn_a_log[ly].astype(F32)).reshape(-1)
    dtb = gdn_dt_bias[ly].astype(F32).reshape(-1)
    prow = jnp.zeros((2, 128), F32).at[0, 8:16].set(ea).at[1, 8:16].set(dtb)
    pcol = jnp.stack([ea, dtb], axis=1)
    conv_w = jnp.zeros((8, qkv.shape[2]), F32).at[0:CONV_K].set(gdn_conv[ly].astype(F32))
    prep = _gdn_prepare(qkv, conv_w, ba, bat_chunks, prow, pcol, l)
    o_f, o_b = _gdn_scan(prep, l)

    wr_hi, wr_lo = _split_hi_lo(jnp.pad(w_router[ly].astype(F32), ((0, 0), (0, 128 - N_EXPERTS))))
    wr = jnp.concatenate([wr_hi, wr_lo], axis=1)
    brt = jnp.pad(b_router[ly].astype(F32), (0, 128 - N_EXPERTS)).reshape(1, 128)
    x1, h2, logits = _merge(x, z, o_f, o_b, gate, br, mod3,
                            s5_w_glu[ly].astype(BF16), s5_b_glu[ly].astype(F32).reshape(1, -1),
                            gdn_norm[ly].astype(F32).reshape(1, -1),
                            w_branch_a[ly].astype(BF16), w_branch_b[ly].astype(BF16), w_out[ly].astype(BF16),
                            norm2[ly].astype(F32).reshape(1, d), wr, wr_hi, brt)

    n_tok = b * l
    weights, src_token, pos, tile_expert, n_used = _route(logits.reshape(n_tok, 128)[:, 0:N_EXPERTS], n_tok)
    h2f = h2.reshape(2 * n_tok, d)
    xs = h2f[src_token]
    ys = _moe_experts(xs, tile_expert, n_used, w_gate_up[ly], b_gate_up[ly], w_down[ly], b_down[ly])
    yg = ys[pos.T.reshape(-1)].reshape(TOP_K, b, l, d)
    wt4 = jnp.pad(weights, ((0, 0), (0, 128 - TOP_K))).reshape(b, l, 128)
    return _final(x1, yg, wt4, mod3, norm_f)
```

```python
import functools

import numpy as np
import jax
import jax.numpy as jnp
from jax import lax
from jax.experimental import pallas as pl
from jax.experimental.pallas import tpu as pltpu

F32 = jnp.float32
BF16 = jnp.bfloat16
HIGHEST = lax.Precision.HIGHEST

EPS = 1e-6
GRID_W = 64

S5_GROUP = 16
S5_STATE = 64
S5_Q = 16

GDN_HEADS = 4
GDN_DK = 128
GDN_CHUNK = 64
CONV_K = 5

N_EXPERTS = 32
TOP_K = 4
SWIGLU_LIMIT = 7.0
SWIGLU_ALPHA = 1.702

ROW_TILE = 256
MERGE_TILE = 512
IN_TILE = 512
MOE_TILE = 512
VMEM_LIMIT = 56 * 1024 * 1024


def _cparams(sem):
    return pltpu.CompilerParams(dimension_semantics=sem, vmem_limit_bytes=VMEM_LIMIT)


def _sigmoid(v):
    return 1.0 / (1.0 + jnp.exp(-v))


def _softplus(v):
    return jnp.maximum(v, 0.0) + jnp.log(1.0 + jnp.exp(-jnp.abs(v)))


def _dot(a, b):
    return jnp.dot(a, b, preferred_element_type=F32)


def _dot_nt(a, b):
    return lax.dot_general(a, b, (((1,), (1,)), ((), ())), preferred_element_type=F32)


def _dot_tn(a, b):
    return lax.dot_general(a, b, (((0,), (0,)), ((), ())), preferred_element_type=F32)


def _mod_kernel(c_ref, w_ref, b_ref, o_ref):
    cc = c_ref[...]
    s = cc * _sigmoid(cc)
    o_ref[...] = jnp.dot(s, w_ref[...], precision=HIGHEST, preferred_element_type=F32) + b_ref[...]


def _modulation(cc, w_mod, b_mod):
    rows, d = cc.shape
    n = w_mod.shape[1]
    bn = d
    return pl.pallas_call(
        _mod_kernel,
        grid=(n // bn,),
        in_specs=[pl.BlockSpec((rows, d), lambda j: (0, 0)),
                  pl.BlockSpec((d, bn), lambda j: (0, j)),
                  pl.BlockSpec((1, bn), lambda j: (0, j))],
        out_specs=pl.BlockSpec((rows, bn), lambda j: (0, j)),
        out_shape=jax.ShapeDtypeStruct((rows, n), F32),
        compiler_params=_cparams(("arbitrary",)),
        name="adaln_mod",
    )(cc, w_mod, b_mod.reshape(1, n))


def _inproj_kernel(x_ref, c_ref, sc_ref, sh_ref, g_ref, w_ref, wlo_ref, wt_ref, wtlo_ref,
                   ut_ref, qkv_ref, gate_ref, br_ref, ba_ref, bat_ref, us_ref, *, n_lat_blocks):
    j = pl.program_id(1)
    is_ctx = j >= n_lat_blocks
    x = x_ref[0]
    lc = c_ref.shape[1]
    x = jnp.where(is_ctx, jnp.concatenate([c_ref[0], x[lc:]], axis=0) if lc < x.shape[0] else c_ref[0], x)
    ms = jnp.mean(x * x, axis=-1, keepdims=True)
    y = x * lax.rsqrt(ms + EPS) * g_ref[...]
    h = y * (1.0 + sc_ref[0]) + sh_ref[0]
    h_hi = h.astype(BF16)
    h_lo = (h - h_hi.astype(F32)).astype(BF16)
    z = _dot(h_hi, w_ref[...])
    nchunk = us_ref.shape[1] // S5_Q
    for c in range(us_ref.shape[0]):
        us_ref[c] = z[:, c * 128:(c + 1) * 128]
        for t in range(S5_Q):
            ut_ref[t, 0, :, c * 128:(c + 1) * 128] = us_ref[c, pl.ds(t, nchunk, stride=S5_Q), :].astype(BF16)
    qkv_ref[0] = z[:, 512:2048].astype(BF16)
    gate_ref[0] = z[:, 2048:2560].astype(BF16)
    br_ref[0] = z[:, 2560:4608].astype(BF16)
    ba_ref[0] = z[:, 4608:4736] + z[:, 4736:4864] + _dot(h_lo, wlo_ref[...])
    rt = _dot_nt(wt_ref[...], h_hi)
    bat_ref[0] = rt[0:16] + rt[16:32] + _dot_nt(wtlo_ref[...], h_lo)


def _in_projection(x, ctx, mod3, norm1, w_main, w_lo, wt, wt_lo):
    b, l, d = x.shape
    lc = ctx.shape[1]
    tm = min(IN_TILE, l)
    assert l % tm == 0 and lc <= tm
    nl = l // tm
    nb = mod3.shape[0] // 6 - 1
    lt = (nl + 1) * tm
    ncols = w_main.shape[1]

    def x_map(i, j):
        return (i, jnp.minimum(j, nl - 1), 0)

    def mod_map(k):
        return lambda i, j: (jnp.where(j >= nl, nb, i) * 6 + k, 0, 0)

    def tok(width):
        return pl.BlockSpec((1, tm, width), lambda i, j: (i, j, 0))

    const2 = lambda i, j: (0, 0)
    outs = pl.pallas_call(
        functools.partial(_inproj_kernel, n_lat_blocks=nl),
        grid=(b, nl + 1),
        in_specs=[pl.BlockSpec((1, tm, d), x_map),
                  pl.BlockSpec((1, lc, d), lambda i, j: (i, 0, 0)),
                  pl.BlockSpec((1, 1, d), mod_map(1)),
                  pl.BlockSpec((1, 1, d), mod_map(0)),
                  pl.BlockSpec((1, d), const2),
                  pl.BlockSpec((d, ncols), const2),
                  pl.BlockSpec((d, 128), const2),
                  pl.BlockSpec((32, d), const2),
                  pl.BlockSpec((16, d), const2)],
        out_specs=[pl.BlockSpec((S5_Q, 1, tm // S5_Q, 512), lambda i, j: (0, i, j, 0)),
                   tok(1536), tok(512), tok(2048), tok(128),
                   pl.BlockSpec((1, 16, tm), lambda i, j: (i, 0, j))],
        out_shape=[jax.ShapeDtypeStruct((S5_Q, b, lt // S5_Q, 512), BF16),
                   jax.ShapeDtypeStruct((b, lt, 1536), BF16),
                   jax.ShapeDtypeStruct((b, lt, 512), BF16),
                   jax.ShapeDtypeStruct((b, lt, 2048), BF16),
                   jax.ShapeDtypeStruct((b, lt, 128), F32),
                   jax.ShapeDtypeStruct((b, 16, lt), F32)],
        scratch_shapes=[pltpu.VMEM((4, tm, 128), F32)],
        compiler_params=_cparams(("parallel", "arbitrary")),
        name="in_projection",
    )(x, ctx, mod3, mod3, norm1.reshape(1, d), w_main, w_lo, wt, wt_lo)
    return outs


def _s5_matrices(lam_re, lam_im, log_step, b_re, b_im, c_re, c_im, d_skip):
    q, c16 = S5_Q, S5_GROUP
    qc = q * c16
    lr = jnp.minimum(lam_re.astype(F32), -1e-4)
    li = lam_im.astype(F32)
    dt = jnp.exp(log_step.astype(F32))[..., None]
    g, n = lr.shape[1], lr.shape[2]
    pw = jnp.arange(q + 1, dtype=F32)
    mag = jnp.exp(pw * (lr * dt)[..., None])
    ang = pw * (li * dt)[..., None]
    pr, pi = mag * jnp.cos(ang), mag * jnp.sin(ang)
    ar, ai = pr[..., 1], pi[..., 1]
    den = lr * lr + li * li
    fr = ((ar - 1.0) * lr + ai * li) / den
    fi = (ai * lr - (ar - 1.0) * li) / den
    br, bi = b_re.astype(F32), b_im.astype(F32)
    bbr = fr[..., None] * br - fi[..., None] * bi
    bbi = fr[..., None] * bi + fi[..., None] * br
    crt = jnp.swapaxes(c_re.astype(F32), -1, -2)
    cit = jnp.swapaxes(c_im.astype(F32), -1, -2)
    car = (crt[:, :, :, None, :] * pr[..., None] - cit[:, :, :, None, :] * pi[..., None]).reshape(2, g, n, (q + 1) * c16)
    cai = (crt[:, :, :, None, :] * pi[..., None] + cit[:, :, :, None, :] * pr[..., None]).reshape(2, g, n, (q + 1) * c16)
    kern = jnp.sum(bbr[..., None] * car[:, :, :, None, :] - bbi[..., None] * cai[:, :, :, None, :], axis=2)
    kf = kern[0, :, :, 0:qc]
    kb = kern[1].reshape(g, c16, q + 1, c16)[:, :, q - 1::-1].reshape(g, c16, qc)
    onehot = np.zeros((q, c16, qc), np.float32)
    for j in range(q):
        onehot[j, np.arange(c16), j * c16 + np.arange(c16)] = 1.0
    skip = d_skip.astype(F32).reshape(g, 1, c16, 1) * onehot[None]
    rows = []
    for j in range(q):
        mf = jnp.pad(kf, ((0, 0), (0, 0), (j * c16, 0)))[:, :, 0:qc]
        mb = jnp.pad(kb[:, :, (q - 1 - j) * c16:], ((0, 0), (0, 0), (0, (q - 1 - j) * c16)))
        rows.append(mf + mb)
    m = (jnp.stack(rows, axis=1) + skip).reshape(g, qc, qc)

    bbr_t, bbi_t = jnp.swapaxes(bbr, -1, -2), jnp.swapaxes(bbi, -1, -2)
    prt, pit = jnp.swapaxes(pr, -1, -2), jnp.swapaxes(pi, -1, -2)

    def p_block(d, reverse):
        ppr = prt[d, :, q - 1::-1] if reverse else prt[d, :, 0:q]
        ppi = pit[d, :, q - 1::-1] if reverse else pit[d, :, 0:q]
        re = ppr[:, :, None, :] * bbr_t[d][:, None] - ppi[:, :, None, :] * bbi_t[d][:, None]
        im = ppr[:, :, None, :] * bbi_t[d][:, None] + ppi[:, :, None, :] * bbr_t[d][:, None]
        return re.reshape(g, qc, n), im.reshape(g, qc, n)

    pf_re, pf_im = p_block(0, True)
    pb_re, pb_im = p_block(1, False)
    p = jnp.concatenate([pf_re, pb_re, pf_im, pb_im], axis=-1)

    def r_block(d, reverse):
        if reverse:
            sel = lambda t: t[d].reshape(g, n, q + 1, c16)[:, :, q:0:-1].reshape(g, n, qc)
        else:
            sel = lambda t: t[d][:, :, c16:(q + 1) * c16]
        return sel(car), -sel(cai)

    rf_re, rf_im = r_block(0, False)
    rb_re, rb_im = r_block(1, True)
    r = jnp.concatenate([rf_re, rb_re, rf_im, rb_im], axis=1)
    a16 = jnp.stack([jnp.concatenate([pr[0, :, :, q], pr[1, :, :, q]], axis=-1),
                     jnp.concatenate([pi[0, :, :, q], pi[1, :, :, q]], axis=-1)], axis=1)
    return m.astype(BF16), p.astype(BF16), r.astype(BF16), a16


S5_LANE_GROUPS = 8
S5_BATCH = 4


def _s5_kernel(ut_ref, m_ref, p_ref, r_ref, a_ref, zt_ref,
               pi_ref, x_ref, xp_ref, v_ref, sk_ref, srb_ref, y_ref, yp_ref, *, n_lat, n_ctx, bb):
    q, ng = S5_Q, S5_LANE_GROUPS
    qc = q * S5_GROUP
    width = ng * qc
    n_all = n_lat + n_ctx
    rows_lat = n_lat * bb

    @pl.when(jnp.logical_and(pl.program_id(0) == 0, pl.program_id(1) == 0))
    def _():
        src = lax.broadcasted_iota(jnp.int32, (width, qc), 0)
        dst = lax.broadcasted_iota(jnp.int32, (width, qc), 1)
        dest_in_group = ((src >> 7) << 4) + (src & 15)
        grp = (src >> 4) & (ng - 1)
        for g in range(ng):
            hit = jnp.logical_and(grp == g, dest_in_group == dst)
            pi_ref[:, g * qc:(g + 1) * qc] = jnp.where(hit, 1.0, 0.0).astype(BF16)

    for b in range(bb):
        for t in range(q):
            x_ref[t, pl.ds(b, n_all, stride=bb), :] = ut_ref[t, b].astype(F32)
    xb = jnp.concatenate([x_ref[t] for t in range(q)], axis=1).astype(BF16)
    for g in range(ng):
        cols = slice(g * qc, (g + 1) * qc)
        xg = _dot(xb, pi_ref[:, cols]).astype(BF16)
        xp_ref[:, cols] = xg
        v_ref[:, cols] = _dot(xg, p_ref[g])

    a = a_ref[...]
    a1 = jnp.concatenate([a[g, 0:1] for g in range(ng) for _ in range(2)], axis=1)
    a2 = jnp.concatenate([s * a[g, 1:2] for g in range(ng) for s in (-1.0, 1.0)], axis=1)
    fwd_lane = (lax.broadcasted_iota(jnp.int32, (bb, width), 1) & 127) < S5_STATE

    def partner(s):
        tiles = [s[:, i * 128:(i + 1) * 128] for i in range(2 * ng)]
        return jnp.concatenate([tiles[i ^ 1] for i in range(2 * ng)], axis=1)

    def advance(s, vf, vb):
        return a1 * s + a2 * partner(s) + jnp.where(fwd_lane, vf, vb)

    def scan(base, n, s0, store):
        def body(t, s):
            rf = pl.multiple_of((base + 2 * t) * bb, 2 * bb)
            rb = pl.multiple_of((base + n - 2 - 2 * t) * bb, 2 * bb)
            vf = v_ref[pl.ds(rf, 2 * bb), :]
            vb = v_ref[pl.ds(rb, 2 * bb), :]
            s1 = advance(s, vf[0:bb], vb[bb:2 * bb])
            s2 = advance(s1, vf[bb:2 * bb], vb[0:bb])
            if store:
                sk_ref[pl.ds(rf, 2 * bb), :] = jnp.concatenate([s, s1], axis=0)
                srb_ref[pl.ds(rb, 2 * bb), :] = jnp.concatenate([s1, s], axis=0)
            return s2
        return lax.fori_loop(0, n // 2, body, s0)

    s_ctx = scan(n_lat, n_ctx, jnp.zeros((bb, width), F32), False)
    scan(0, n_lat, s_ctx, True)
    lane = (lax.broadcasted_iota(jnp.int32, (rows_lat, width), 1) & 127) < S5_STATE
    st = jnp.where(lane, sk_ref[...], srb_ref[...]).astype(BF16)
    for g in range(ng):
        cols = slice(g * qc, (g + 1) * qc)
        y = _dot(xp_ref[0:rows_lat, cols], m_ref[g]) + _dot(st[:, cols], r_ref[g])
        y_ref[:, cols] = jax.nn.gelu(y).astype(BF16)
    yb = y_ref[...]
    for t in range(q):
        yp_ref[...] = _dot_nt(yb, pi_ref[t * 128:(t + 1) * 128, :])
        for b in range(bb):
            zt_ref[t, b] = yp_ref[pl.ds(b, n_lat, stride=bb), :].astype(BF16)


def _s5_mixer(ut, l, lt, mats):
    m, p, r, a16 = mats
    q, b, _, width = ut.shape
    n_all, n_lat = lt // q, l // q
    ng = S5_LANE_GROUPS
    qc = q * S5_GROUP
    bb = min(S5_BATCH, b)
    nj = width // 128
    wspec = pl.BlockSpec((ng, qc, qc), lambda j, i: (j, 0, 0))
    return pl.pallas_call(
        functools.partial(_s5_kernel, n_lat=n_lat, n_ctx=n_all - n_lat, bb=bb),
        grid=(nj, b // bb),
        in_specs=[pl.BlockSpec((q, bb, n_all, 128), lambda j, i: (0, i, 0, j)),
                  wspec, wspec, wspec,
                  pl.BlockSpec((ng, 2, 2 * S5_STATE), lambda j, i: (j, 0, 0))],
        out_specs=pl.BlockSpec((q, bb, n_lat, 128), lambda j, i: (0, i, 0, j)),
        out_shape=jax.ShapeDtypeStruct((q, b, n_lat, width), BF16),
        scratch_shapes=[pltpu.VMEM((ng * qc, ng * qc), BF16),
                        pltpu.VMEM((q, n_all * bb, 128), F32),
                        pltpu.VMEM((n_all * bb, ng * qc), BF16),
                        pltpu.VMEM((n_all * bb, ng * qc), F32),
                        pltpu.VMEM((n_lat * bb, ng * qc), F32),
                        pltpu.VMEM((n_lat * bb, ng * qc), F32),
                        pltpu.VMEM((n_lat * bb, ng * qc), BF16),
                        pltpu.VMEM((n_lat * bb, 128), F32)],
        compiler_params=_cparams(("arbitrary", "arbitrary")),
        name="s5_scan",
    )(ut, m, p, r, a16)


def _inv_unit_triangular_many(mats):
    c = mats[0].shape[0]
    eye = (lax.broadcasted_iota(jnp.int32, (c, c), 0) == lax.broadcasted_iota(jnp.int32, (c, c), 1)).astype(F32)
    prods = [eye - a for a in mats]
    pows = [a.astype(BF16) for a in mats]
    for _ in range(int(np.log2(c)) - 1):
        pows = [_dot(a, a).astype(BF16) for a in pows]
        prods = [p + _dot(p.astype(BF16), a) for p, a in zip(prods, pows)]
    return prods


def _gdn_prep_kernel(qkv_ref, cw_ref, ba_ref, bat_ref, prow_ref, pcol_ref,
                     u_ref, wq_ref, qk_ref, kdt_ref, dl_ref, *, n_lat_blocks, ctx_len):
    j = pl.program_id(1)
    rows = qkv_ref.shape[1]
    c = GDN_CHUNK
    nchunk = rows // c
    seg = jnp.where(j >= n_lat_blocks, ctx_len, GRID_W)
    x = qkv_ref[0].astype(F32)
    pos = lax.broadcasted_iota(jnp.int32, (rows, 1), 0) & (seg - 1)
    acc = x * cw_ref[2:3, :]
    for s in (-2, -1, 1, 2):
        shifted = pltpu.roll(x, (-s) % rows, 0)
        ok = jnp.logical_and(pos + s >= 0, pos + s < seg)
        acc = acc + jnp.where(ok, shifted, 0.0) * cw_ref[2 + s:3 + s, :]
    act = acc * _sigmoid(acc)

    ii = lax.broadcasted_iota(jnp.int32, (c, c), 0)
    jj = lax.broadcasted_iota(jnp.int32, (c, c), 1)
    incl = (ii >= jj, ii <= jj)
    strict = (ii > jj, ii < jj)
    tri_low = incl[0].astype(F32)
    tri_up = incl[1].astype(F32)
    eye_bf = (ii == jj).astype(BF16)
    ones = jnp.ones((c, c), F32)
    lane = lax.broadcasted_iota(jnp.int32, (c, 128), 1)
    prow = prow_ref[...]
    pcol = pcol_ref[...]
    hp = dict(precision=HIGHEST, preferred_element_type=F32)

    gates = []
    for ci in range(nchunk):
        ba = ba_ref[0, ci * c:(ci + 1) * c, :]
        g_all = -prow[0:1, :] * _softplus(ba + prow[1:2, :])
        g_all = jnp.where(jnp.logical_and(lane >= 8, lane < 16), g_all, 0.0)
        bat = bat_ref[0, ci]
        g_row = -pcol[:, 0:1] * _softplus(bat[8:16, :] + pcol[:, 1:2])
        gates.append(dict(beta=_sigmoid(ba),
                          gc=(jnp.dot(tri_low, g_all, **hp), jnp.dot(tri_up, g_all, **hp)),
                          gtot=jnp.dot(ones, g_all, **hp),
                          gr=(jnp.dot(g_row, tri_up, **hp), jnp.dot(g_row, tri_low, **hp))))

    qn, kn, vv = [], [], []
    for h in range(GDN_HEADS):
        qh = act[:, h * 128:(h + 1) * 128]
        kh = act[:, 512 + h * 128:512 + (h + 1) * 128]
        qn.append(qh * lax.rsqrt(jnp.sum(qh * qh, axis=-1, keepdims=True) + EPS) * (GDN_DK ** -0.5))
        kn.append(kh * lax.rsqrt(jnp.sum(kh * kh, axis=-1, keepdims=True) + EPS))
        vv.append(act[:, 1024 + h * 128:1024 + (h + 1) * 128])

    pairs = [(ci, h) for ci in range(nchunk) for h in range(GDN_HEADS)]
    sl = lambda t, ci: t[ci * c:(ci + 1) * c]
    kb = {p: sl(kn[p[1]], p[0]).astype(BF16) for p in pairs}
    kk = {p: _dot_nt(kb[p], kb[p]) for p in pairs}
    qkm = {p: _dot_nt(sl(qn[p[1]], p[0]).astype(BF16), kb[p]) for p in pairs}

    probs = [(ci, h, d) for ci in range(nchunk) for h in range(GDN_HEADS) for d in range(2)]
    beta, gc, gt, decay, a_mats = {}, {}, {}, {}, []
    for (ci, h, d) in probs:
        col = d * GDN_HEADS + h
        gi = gates[ci]
        beta[ci, h, d] = gi["beta"][:, col:col + 1]
        gc[ci, h, d] = gi["gc"][d][:, 8 + col:9 + col]
        gt[ci, h, d] = gi["gtot"][:, 8 + col:9 + col]
        grow = gi["gr"][d][col:col + 1, :]
        decay[ci, h, d] = jnp.exp(jnp.where(incl[d], gc[ci, h, d] - grow, -jnp.inf))
        a_mats.append(jnp.where(strict[d], beta[ci, h, d] * kk[ci, h] * decay[ci, h, d], 0.0))
    tinv = _inv_unit_triangular_many(a_mats)

    eg, sols, kdts = {}, {}, {}
    for n, (ci, h, d) in enumerate(probs):
        p = (ci, h, d)
        eg[p] = jnp.exp(gc[p])
        kh = sl(kn[h], ci)
        rhs = jnp.concatenate([sl(vv[h], ci) * beta[p], kh * (beta[p] * eg[p])], axis=-1).astype(BF16)
        sols[p] = _dot(tinv[n].astype(BF16), rhs)
        kdts[p] = _dot_tn((kh * jnp.exp(gt[p] - gc[p])).astype(BF16), eye_bf)
    for (ci, h, d) in probs:
        p = (ci, h, d)
        u_ref[0, d, h, ci] = sols[p][:, 0:128]
        wq_ref[0, d, h, ci, 0:c, :] = sols[p][:, 128:256].astype(BF16)
        wq_ref[0, d, h, ci, c:2 * c, :] = (sl(qn[h], ci) * eg[p]).astype(BF16)
        qk_ref[0, d, h, ci] = (qkm[ci, h] * decay[p]).astype(BF16)
        kdt_ref[0, d, h, ci] = kdts[p].astype(BF16)
        dl_ref[0, d, h, ci] = jnp.broadcast_to(jnp.exp(gt[p][0:8, :]), (8, 128))


def _gdn_prepare(qkv, conv_w, ba, bat_chunks, prow, pcol, l, lt):
    b, _, width = qkv.shape
    tm = ROW_TILE
    c = GDN_CHUNK
    cpb = tm // c
    nblk = lt // tm
    nch = lt // c
    hd = (b, 2, GDN_HEADS, nch)

    def blk(shape_tail, dtype):
        return (pl.BlockSpec((1, 2, GDN_HEADS, cpb) + shape_tail, lambda i, j: (i, 0, 0, j, 0, 0)),
                jax.ShapeDtypeStruct(hd + shape_tail, dtype))

    specs = [blk((c, 128), F32), blk((2 * c, 128), BF16), blk((c, c), BF16), blk((GDN_DK, c), BF16),
             blk((8, 128), F32)]
    return pl.pallas_call(
        functools.partial(_gdn_prep_kernel, n_lat_blocks=l // tm, ctx_len=lt - l),
        grid=(b, nblk),
        in_specs=[pl.BlockSpec((1, tm, width), lambda i, j: (i, j, 0)),
                  pl.BlockSpec((8, width), lambda i, j: (0, 0)),
                  pl.BlockSpec((1, tm, 128), lambda i, j: (i, j, 0)),
                  pl.BlockSpec((1, cpb, 16, c), lambda i, j: (i, j, 0, 0)),
                  pl.BlockSpec((2, 128), lambda i, j: (0, 0)),
                  pl.BlockSpec((8, 2), lambda i, j: (0, 0))],
        out_specs=[s for s, _ in specs],
        out_shape=[o for _, o in specs],
        compiler_params=_cparams(("parallel", "parallel")),
        name="gdn_prepare",
    )(qkv, conv_w, ba, bat_chunks, prow, pcol)


SCAN_GROUP = 2


def _gdn_scan_kernel(uf, wqf, qkf, kdtf, dlf, ub, wqb, qkb, kdtb, dlb, of_ref, ob_ref, s_ref):
    t = pl.program_id(0)
    nb = s_ref.shape[0]
    c = GDN_CHUNK

    @pl.when(t == 0)
    def _():
        s_ref[...] = jnp.zeros_like(s_ref)

    refs = ((uf, wqf, qkf, kdtf, dlf, of_ref), (ub, wqb, qkb, kdtb, dlb, ob_ref))
    for b0 in range(0, nb, SCAN_GROUP):
        chains = [(bi, d, h) for bi in range(b0, min(b0 + SCAN_GROUP, nb)) for d in range(2)
                  for h in range(GDN_HEADS)]
        s = {k: s_ref[k[0], k[1], k[2]] for k in chains}
        sb = {k: s[k].astype(BF16) for k in chains}
        r = {k: _dot(refs[k[1]][1][k[0], 0, k[2], 0], sb[k]) for k in chains}
        vb = {k: (refs[k[1]][0][k[0], 0, k[2], 0] - r[k][0:c]).astype(BF16) for k in chains}
        o = {k: r[k][c:2 * c] + _dot(refs[k[1]][2][k[0], 0, k[2], 0], vb[k]) for k in chains}
        sn = {k: s[k] * refs[k[1]][4][k[0], 0, k[2], 0][0:1, :] + _dot(refs[k[1]][3][k[0], 0, k[2], 0], vb[k])
              for k in chains}
        for k in chains:
            s_ref[k[0], k[1], k[2]] = sn[k]
            refs[k[1]][5][k[0], :, k[2] * 128:(k[2] + 1) * 128] = o[k]


def _gdn_scan(prep, l):
    u = prep[0]
    b, _, heads, nch, c, _ = u.shape
    n_lat = l // c
    n_ctx = nch - n_lat
    lt = nch * c

    def fwd_chunk(t):
        return jnp.where(t < n_ctx, n_lat + t, t - n_ctx)

    def bwd_chunk(t):
        return nch - 1 - t

    def spec(arr, chunk_of, d):
        tail = arr.shape[4:]
        return pl.BlockSpec((b, 1, heads, 1) + tail, lambda t: (0, d, 0, chunk_of(t), 0, 0))

    ins = [spec(a, fwd_chunk, 0) for a in prep] + [spec(a, bwd_chunk, 1) for a in prep]
    width = heads * 128
    return pl.pallas_call(
        _gdn_scan_kernel,
        grid=(nch,),
        in_specs=ins,
        out_specs=[pl.BlockSpec((b, c, width), lambda t: (0, fwd_chunk(t), 0)),
                   pl.BlockSpec((b, c, width), lambda t: (0, bwd_chunk(t), 0))],
        out_shape=[jax.ShapeDtypeStruct((b, lt, width), F32)] * 2,
        scratch_shapes=[pltpu.VMEM((b, 2, heads, GDN_DK, 128), F32)],
        compiler_params=_cparams(("arbitrary",)),
        name="gdn_scan",
    )(*prep, *prep)


def _merge_kernel(x_ref, z_ref, of_ref, ob_ref, gate_ref, br_ref, gt1_ref, sc2_ref, sh2_ref,
                  wglu_ref, bglu_ref, gnorm_ref, wa_ref, wb_ref, wo_ref, n2_ref, wr_ref, wrhi_ref, brt_ref,
                  x1_ref, h2_ref, lg_ref, zs_ref):
    nchunk = zs_ref.shape[1] // S5_Q
    for t in range(S5_Q):
        zt = z_ref[t, 0].astype(F32)
        for c in range(zs_ref.shape[0]):
            zs_ref[c, pl.ds(t, nchunk, stride=S5_Q), :] = zt[:, c * 128:(c + 1) * 128]
    zf = jnp.concatenate([zs_ref[c] for c in range(zs_ref.shape[0])], axis=1)
    z = zf.astype(BF16)
    ya = zf * _sigmoid(_dot(z, wglu_ref[...]) + bglu_ref[...])
    o = of_ref[0] + ob_ref[0]
    gate = gate_ref[0].astype(F32)
    parts = []
    for h in range(GDN_HEADS):
        oh = o[:, h * 128:(h + 1) * 128]
        parts.append(oh * lax.rsqrt(jnp.mean(oh * oh, axis=-1, keepdims=True) + EPS) * gnorm_ref[...])
    yb = jnp.concatenate(parts, axis=-1) * (gate * _sigmoid(gate))
    br = br_ref[0].astype(F32)
    d = x_ref.shape[2]
    ga = _sigmoid(br[:, 0:d])
    gb = _sigmoid(br[:, d:2 * d])
    m = ga * _dot(ya.astype(BF16), wa_ref[...]) + gb * _dot(yb.astype(BF16), wb_ref[...])
    mix = _dot(m.astype(BF16), wo_ref[...])
    x1 = x_ref[0] + gt1_ref[0] * mix
    x1_ref[0] = x1
    y2 = x1 * lax.rsqrt(jnp.mean(x1 * x1, axis=-1, keepdims=True) + EPS) * n2_ref[...]
    h2 = y2 * (1.0 + sc2_ref[0]) + sh2_ref[0]
    h2b = h2.astype(BF16)
    h2_ref[0, 0] = h2b
    h2_ref[1, 0] = h2b
    h2_lo = (h2 - h2b.astype(F32)).astype(BF16)
    rl = _dot(h2b, wr_ref[...])
    lg_ref[0] = rl[:, 0:128] + rl[:, 128:256] + _dot(h2_lo, wrhi_ref[...]) + brt_ref[...]


def _merge(x, z, o_f, o_b, gate, br, mod3, wglu, bglu, gnorm, wa, wb, wo, norm2, wr, wr_hi, brt):
    b, l, d = x.shape
    tm = min(MERGE_TILE, l)
    tok = lambda width: pl.BlockSpec((1, tm, width), lambda i, j: (i, j, 0))
    modspec = lambda k: pl.BlockSpec((1, 1, d), lambda i, j: (i * 6 + k, 0, 0))
    full = lambda arr: pl.BlockSpec(arr.shape, lambda i, j: (0,) * arr.ndim)
    consts = [wglu, bglu, gnorm, wa, wb, wo, norm2, wr, wr_hi, brt]
    return pl.pallas_call(
        _merge_kernel,
        grid=(b, l // tm),
        in_specs=[tok(d), pl.BlockSpec((S5_Q, 1, tm // S5_Q, 512), lambda i, j: (0, i, j, 0)),
                  tok(512), tok(512), tok(512), tok(2 * d),
                  modspec(2), modspec(4), modspec(3)] + [full(a) for a in consts],
        out_specs=[tok(d), pl.BlockSpec((2, 1, tm, d), lambda i, j: (0, i, j, 0)), tok(128)],
        out_shape=[jax.ShapeDtypeStruct((b, l, d), F32),
                   jax.ShapeDtypeStruct((2, b, l, d), BF16),
                   jax.ShapeDtypeStruct((b, l, 128), F32)],
        scratch_shapes=[pltpu.VMEM((4, tm, 128), F32)],
        compiler_params=_cparams(("parallel", "parallel")),
        name="branch_merge",
    )(x, z, o_f, o_b, gate, br, mod3, mod3, mod3, *consts)


def _moe_kernel(te_ref, nu_ref, x_ref, wgu_ref, bgu_ref, wd_ref, bd_ref, y_ref, wgu_bf, wd_bf):
    i = pl.program_id(0)
    changed = jnp.logical_or(i == 0, te_ref[i] != te_ref[jnp.maximum(i - 1, 0)])

    @pl.when(jnp.logical_and(changed, i < nu_ref[0]))
    def _():
        wgu_bf[...] = wgu_ref[0].astype(BF16)
        wd_bf[...] = wd_ref[0].astype(BF16)

    @pl.when(i < nu_ref[0])
    def _():
        de = wd_ref.shape[1]
        gu = _dot(x_ref[...], wgu_bf[...]) + bgu_ref[0]
        gate = jnp.minimum(gu[:, 0:de], SWIGLU_LIMIT)
        up = jnp.clip(gu[:, de:2 * de], -SWIGLU_LIMIT, SWIGLU_LIMIT)
        act = (up + 1.0) * gate * _sigmoid(gate * SWIGLU_ALPHA)
        y_ref[...] = (_dot(act.astype(BF16), wd_bf[...]) + bd_ref[0]).astype(y_ref.dtype)

    @pl.when(i >= nu_ref[0])
    def _():
        y_ref[...] = jnp.zeros_like(y_ref)


def _moe_experts(xs, tile_expert, n_used, w_gate_up, b_gate_up, w_down, b_down):
    p, d = xs.shape
    tm = MOE_TILE
    e, _, n2 = w_gate_up.shape
    de = w_down.shape[1]
    grid_spec = pltpu.PrefetchScalarGridSpec(
        num_scalar_prefetch=2,
        grid=(p // tm,),
        in_specs=[pl.BlockSpec((tm, d), lambda i, te, nu: (jnp.minimum(i, jnp.maximum(nu[0] - 1, 0)), 0)),
                  pl.BlockSpec((1, d, n2), lambda i, te, nu: (te[i], 0, 0)),
                  pl.BlockSpec((1, 1, n2), lambda i, te, nu: (te[i], 0, 0)),
                  pl.BlockSpec((1, de, d), lambda i, te, nu: (te[i], 0, 0)),
                  pl.BlockSpec((1, 1, d), lambda i, te, nu: (te[i], 0, 0))],
        out_specs=pl.BlockSpec((tm, d), lambda i, te, nu: (i, 0)),
        scratch_shapes=[pltpu.VMEM((d, n2), BF16), pltpu.VMEM((de, d), BF16)],
    )
    return pl.pallas_call(
        _moe_kernel,
        grid_spec=grid_spec,
        out_shape=jax.ShapeDtypeStruct((p, d), BF16),
        compiler_params=_cparams(("arbitrary",)),
        name="moe_experts",
    )(tile_expert, n_used, xs, w_gate_up, b_gate_up.reshape(e, 1, n2), w_down, b_down.reshape(e, 1, d))


def _route(logits, n_tokens):
    tm = MOE_TILE
    top_val, top_idx = lax.top_k(logits, TOP_K)
    weights = jax.nn.softmax(top_val, axis=-1)
    flat_e = top_idx.reshape(-1).astype(jnp.int32)
    n_assign = flat_e.shape[0]
    iota = jnp.arange(n_assign, dtype=jnp.int32)
    sorted_e, order = lax.sort((flat_e, iota), num_keys=1, is_stable=True)
    counts = jnp.sum(jax.nn.one_hot(flat_e, N_EXPERTS, dtype=jnp.int32), axis=0)
    padded = ((counts + tm - 1) // tm) * tm
    pad_end = jnp.cumsum(padded)
    pad_start = pad_end - padded
    raw_start = jnp.cumsum(counts) - counts
    dest = pad_start[sorted_e] + (iota - raw_start[sorted_e])
    _, pos = lax.sort((order, dest), num_keys=1)
    pos = pos.reshape(n_tokens, TOP_K)
    n_rows = n_assign + N_EXPERTS * tm
    n_tiles = n_rows // tm
    n_used = (pad_end[-1] // tm).astype(jnp.int32)
    tile_start = jnp.arange(n_tiles, dtype=jnp.int32) * tm
    tile_expert = jnp.sum((tile_start[:, None] >= pad_end[None, :]).astype(jnp.int32), axis=1)
    last_e = jnp.sum((pad_end[-1] - 1 >= pad_end).astype(jnp.int32))
    tile_expert = jnp.minimum(tile_expert, last_e).astype(jnp.int32)
    lane = jnp.arange(tm, dtype=jnp.int32)[None, :]
    off = lane + (tile_start - pad_start[tile_expert])[:, None]
    valid = off < counts[tile_expert][:, None]
    sidx = jnp.clip(raw_start[tile_expert][:, None] + off, 0, n_assign - 1)
    filler = (tile_start[:, None] + lane) % n_tokens
    src_token = jnp.where(valid, (order // TOP_K)[sidx], filler).reshape(-1)
    return weights, src_token, pos, tile_expert, n_used.reshape(1)


def _final_kernel(x1_ref, y0_ref, y1_ref, y2_ref, y3_ref, wt_ref, gt2_ref, nf_ref, o_ref):
    wt = wt_ref[0]
    moe = (wt[:, 0:1] * y0_ref[0, 0].astype(F32) + wt[:, 1:2] * y1_ref[0, 0].astype(F32)
           + wt[:, 2:3] * y2_ref[0, 0].astype(F32) + wt[:, 3:4] * y3_ref[0, 0].astype(F32))
    x2 = x1_ref[0] + gt2_ref[0] * moe
    o_ref[0] = x2 * lax.rsqrt(jnp.mean(x2 * x2, axis=-1, keepdims=True) + EPS) * nf_ref[...]


def _final(x1, yg, wt, mod3, norm_f):
    b, l, d = x1.shape
    tm = ROW_TILE
    tok = lambda width: pl.BlockSpec((1, tm, width), lambda i, j: (i, j, 0))
    ysel = lambda k: pl.BlockSpec((1, 1, tm, d), lambda i, j: (k, i, j, 0))
    return pl.pallas_call(
        _final_kernel,
        grid=(b, l // tm),
        in_specs=[tok(d)] + [ysel(k) for k in range(TOP_K)] + [tok(128), pl.BlockSpec((1, 1, d), lambda i, j: (i * 6 + 5, 0, 0)),
                                 pl.BlockSpec((1, d), lambda i, j: (0, 0))],
        out_specs=tok(d),
        out_shape=jax.ShapeDtypeStruct((b, l, d), F32),
        compiler_params=_cparams(("parallel", "parallel")),
        name="combine_final_norm",
    )(x1, *([yg] * TOP_K), wt, mod3, norm_f.reshape(1, d))


def _split_hi_lo(w):
    hi = w.astype(BF16)
    lo = (w - hi.astype(F32)).astype(BF16)
    return hi, lo


def kernel(x, c, ctx, c_ctx, w_mod, b_mod, norm1, w_in, s5_lam_re, s5_lam_im, s5_log_step, s5_b_re, s5_b_im, s5_c_re, s5_c_im, s5_d, s5_w_glu, s5_b_glu, gdn_conv, gdn_a_log, gdn_dt_bias, gdn_norm, w_branch_a, w_branch_b, w_out, norm2, w_router, b_router, w_gate_up, b_gate_up, w_down, b_down, norm_f):
    b, l, d = x.shape
    lc = ctx.shape[1]
    depth = w_mod.shape[0]
    assert depth == 1, "single-layer block: the context stream has no consumer after the token mixer"
    assert l % ROW_TILE == 0 and lc == ROW_TILE and lc & (lc - 1) == 0 and b <= 8
    ly = 0

    cc = jnp.zeros((16, d), F32).at[0:b].set(c).at[b].set(c_ctx)
    mod = _modulation(cc, w_mod[ly], b_mod[ly])
    mod3 = mod[0:b + 1].reshape((b + 1) * 6, 1, d)

    wi = w_in[ly]
    o_u, o_qkv, o_gate, o_ba, o_br = 0, 512, 2048, 2560, 2576
    w_ba = wi[:, o_ba:o_br]
    ba_hi, ba_lo = _split_hi_lo(w_ba)
    pad = lambda t: jnp.pad(t, ((0, 0), (0, 128 - t.shape[1])))
    w_main = jnp.concatenate([wi[:, o_u:o_ba].astype(BF16), wi[:, o_br:].astype(BF16), pad(ba_hi), pad(ba_lo)], axis=1)
    wt = jnp.concatenate([ba_hi.T, ba_lo.T], axis=0)
    ut, qkv, gate, br, ba, bat = _in_projection(x, ctx, mod3, norm1[ly], w_main, pad(ba_hi), wt, ba_hi.T)

    mats = _s5_matrices(s5_lam_re[ly], s5_lam_im[ly], s5_log_step[ly], s5_b_re[ly], s5_b_im[ly],
                        s5_c_re[ly], s5_c_im[ly], s5_d[ly])
    lt = l + lc
    z = _s5_mixer(ut, l, lt, mats)

    nch = lt // GDN_CHUNK
    bat_chunks = jnp.transpose(bat[:, :, 0:lt].reshape(b, 16, nch, GDN_CHUNK), (0, 2, 1, 3))
    ea = jnp.exp(gdn_a_log[ly].astype(F32)).reshape(-1)
    dtb = gdn_dt_bias[ly].astype(F32).reshape(-1)
    prow = jnp.zeros((2, 128), F32).at[0, 8:16].set(ea).at[1, 8:16].set(dtb)
    pcol = jnp.stack([ea, dtb], axis=1)
    conv_w = jnp.zeros((8, qkv.shape[2]), F32).at[0:CONV_K].set(gdn_conv[ly].astype(F32))
    prep = _gdn_prepare(qkv, conv_w, ba, bat_chunks, prow, pcol, l, lt)
    o_f, o_b = _gdn_scan(prep, l)

    wr_hi, wr_lo = _split_hi_lo(jnp.pad(w_router[ly].astype(F32), ((0, 0), (0, 128 - N_EXPERTS))))
    wr = jnp.concatenate([wr_hi, wr_lo], axis=1)
    brt = jnp.pad(b_router[ly].astype(F32), (0, 128 - N_EXPERTS)).reshape(1, 128)
    x1, h2, logits = _merge(x, z, o_f, o_b, gate, br, mod3,
                            s5_w_glu[ly].astype(BF16), s5_b_glu[ly].astype(F32).reshape(1, -1),
                            gdn_norm[ly].astype(F32).reshape(1, -1),
                            w_branch_a[ly].astype(BF16), w_branch_b[ly].astype(BF16), w_out[ly].astype(BF16),
                            norm2[ly].astype(F32).reshape(1, d), wr, wr_hi, brt)

    n_tok = b * l
    weights, src_token, pos, tile_expert, n_used = _route(logits.reshape(n_tok, 128)[:, 0:N_EXPERTS], n_tok)
    h2f = h2.reshape(2 * n_tok, d)
    xs = h2f[src_token]
    ys = _moe_experts(xs, tile_expert, n_used, w_gate_up[ly], b_gate_up[ly], w_down[ly], b_down[ly])
    yg = ys[pos.T.reshape(-1)].reshape(TOP_K, b, l, d)
    wt4 = jnp.pad(weights, ((0, 0), (0, 128 - TOP_K))).reshape(b, l, 128)
    return _final(x1, yg, wt4, mod3, norm_f)
```

```python
import functools

import numpy as np
import jax
import jax.numpy as jnp
from jax import lax
from jax.experimental import pallas as pl
from jax.experimental.pallas import tpu as pltpu

F32 = jnp.float32
BF16 = jnp.bfloat16
HIGHEST = lax.Precision.HIGHEST

EPS = 1e-6
GRID_W = 64

S5_GROUP = 16
S5_STATE = 64
S5_Q = 16

GDN_HEADS = 4
GDN_DK = 128
GDN_CHUNK = 64
CONV_K = 5

N_EXPERTS = 32
TOP_K = 4
SWIGLU_LIMIT = 7.0
SWIGLU_ALPHA = 1.702

ROW_TILE = 256
MERGE_TILE = 512
IN_TILE = 256
MOE_TILE = 512
VMEM_LIMIT = 56 * 1024 * 1024


def _cparams(sem):
    return pltpu.CompilerParams(dimension_semantics=sem, vmem_limit_bytes=VMEM_LIMIT)


def _sigmoid(v):
    return 1.0 / (1.0 + jnp.exp(-v))


def _softplus(v):
    return jnp.maximum(v, 0.0) + jnp.log(1.0 + jnp.exp(-jnp.abs(v)))


def _dot(a, b):
    return jnp.dot(a, b, preferred_element_type=F32)


def _dot_nt(a, b):
    return lax.dot_general(a, b, (((1,), (1,)), ((), ())), preferred_element_type=F32)


def _dot_tn(a, b):
    return lax.dot_general(a, b, (((0,), (0,)), ((), ())), preferred_element_type=F32)


def _mod_kernel(c_ref, w_ref, b_ref, o_ref):
    cc = c_ref[...]
    s = cc * _sigmoid(cc)
    o_ref[...] = jnp.dot(s, w_ref[...], precision=HIGHEST, preferred_element_type=F32) + b_ref[...]


def _modulation(cc, w_mod, b_mod):
    rows, d = cc.shape
    n = w_mod.shape[1]
    bn = d
    return pl.pallas_call(
        _mod_kernel,
        grid=(n // bn,),
        in_specs=[pl.BlockSpec((rows, d), lambda j: (0, 0)),
                  pl.BlockSpec((d, bn), lambda j: (0, j)),
                  pl.BlockSpec((1, bn), lambda j: (0, j))],
        out_specs=pl.BlockSpec((rows, bn), lambda j: (0, j)),
        out_shape=jax.ShapeDtypeStruct((rows, n), F32),
        compiler_params=_cparams(("arbitrary",)),
        name="adaln_mod",
    )(cc, w_mod, b_mod.reshape(1, n))


def _inproj_kernel(x_ref, c_ref, sc_ref, sh_ref, g_ref, w_ref, wlo_ref, wt_ref, wtlo_ref,
                   ut_ref, qkv_ref, gate_ref, br_ref, ba_ref, bat_ref, us_ref, *, n_lat_blocks):
    j = pl.program_id(1)
    is_ctx = j >= n_lat_blocks
    x = x_ref[0]
    lc = c_ref.shape[1]
    x = jnp.where(is_ctx, jnp.concatenate([c_ref[0], x[lc:]], axis=0) if lc < x.shape[0] else c_ref[0], x)
    ms = jnp.mean(x * x, axis=-1, keepdims=True)
    y = x * lax.rsqrt(ms + EPS) * g_ref[...]
    h = y * (1.0 + sc_ref[0]) + sh_ref[0]
    h_hi = h.astype(BF16)
    h_lo = (h - h_hi.astype(F32)).astype(BF16)
    z = _dot(h_hi, w_ref[...])
    nchunk = us_ref.shape[1] // S5_Q
    for c in range(us_ref.shape[0]):
        us_ref[c] = z[:, c * 128:(c + 1) * 128]
        for t in range(S5_Q):
            ut_ref[t, 0, :, c * 128:(c + 1) * 128] = us_ref[c, pl.ds(t, nchunk, stride=S5_Q), :].astype(BF16)
    qkv_ref[0] = z[:, 512:2048].astype(BF16)
    gate_ref[0] = z[:, 2048:2560].astype(BF16)
    br_ref[0] = z[:, 2560:4608].astype(BF16)
    ba_ref[0] = z[:, 4608:4736] + z[:, 4736:4864] + _dot(h_lo, wlo_ref[...])
    rt = _dot_nt(wt_ref[...], h_hi)
    bat_ref[0] = rt[0:16] + rt[16:32] + _dot_nt(wtlo_ref[...], h_lo)


def _in_projection(x, ctx, mod3, norm1, w_main, w_lo, wt, wt_lo):
    b, l, d = x.shape
    lc = ctx.shape[1]
    tm = min(IN_TILE, l)
    assert l % tm == 0 and lc <= tm
    nl = l // tm
    nb = mod3.shape[0] // 6 - 1
    lt = (nl + 1) * tm
    ncols = w_main.shape[1]

    def x_map(i, j):
        return (i, jnp.minimum(j, nl - 1), 0)

    def mod_map(k):
        return lambda i, j: (jnp.where(j >= nl, nb, i) * 6 + k, 0, 0)

    def tok(width):
        return pl.BlockSpec((1, tm, width), lambda i, j: (i, j, 0))

    const2 = lambda i, j: (0, 0)
    outs = pl.pallas_call(
        functools.partial(_inproj_kernel, n_lat_blocks=nl),
        grid=(b, nl + 1),
        in_specs=[pl.BlockSpec((1, tm, d), x_map),
                  pl.BlockSpec((1, lc, d), lambda i, j: (i, 0, 0)),
                  pl.BlockSpec((1, 1, d), mod_map(1)),
                  pl.BlockSpec((1, 1, d), mod_map(0)),
                  pl.BlockSpec((1, d), const2),
                  pl.BlockSpec((d, ncols), const2),
                  pl.BlockSpec((d, 128), const2),
                  pl.BlockSpec((32, d), const2),
                  pl.BlockSpec((16, d), const2)],
        out_specs=[pl.BlockSpec((S5_Q, 1, tm // S5_Q, 512), lambda i, j: (0, i, j, 0)),
                   tok(1536), tok(512), tok(2048), tok(128),
                   pl.BlockSpec((1, 16, tm), lambda i, j: (i, 0, j))],
        out_shape=[jax.ShapeDtypeStruct((S5_Q, b, lt // S5_Q, 512), BF16),
                   jax.ShapeDtypeStruct((b, lt, 1536), BF16),
                   jax.ShapeDtypeStruct((b, lt, 512), BF16),
                   jax.ShapeDtypeStruct((b, lt, 2048), BF16),
                   jax.ShapeDtypeStruct((b, lt, 128), F32),
                   jax.ShapeDtypeStruct((b, 16, lt), F32)],
        scratch_shapes=[pltpu.VMEM((4, tm, 128), F32)],
        compiler_params=_cparams(("parallel", "arbitrary")),
        name="in_projection",
    )(x, ctx, mod3, mod3, norm1.reshape(1, d), w_main, w_lo, wt, wt_lo)
    return outs


def _s5_matrices(lam_re, lam_im, log_step, b_re, b_im, c_re, c_im, d_skip):
    q, c16 = S5_Q, S5_GROUP
    qc = q * c16
    lr = jnp.minimum(lam_re.astype(F32), -1e-4)
    li = lam_im.astype(F32)
    dt = jnp.exp(log_step.astype(F32))[..., None]
    g, n = lr.shape[1], lr.shape[2]
    pw = jnp.arange(q + 1, dtype=F32)
    mag = jnp.exp(pw * (lr * dt)[..., None])
    ang = pw * (li * dt)[..., None]
    pr, pi = mag * jnp.cos(ang), mag * jnp.sin(ang)
    ar, ai = pr[..., 1], pi[..., 1]
    den = lr * lr + li * li
    fr = ((ar - 1.0) * lr + ai * li) / den
    fi = (ai * lr - (ar - 1.0) * li) / den
    br, bi = b_re.astype(F32), b_im.astype(F32)
    bbr = fr[..., None] * br - fi[..., None] * bi
    bbi = fr[..., None] * bi + fi[..., None] * br
    crt = jnp.swapaxes(c_re.astype(F32), -1, -2)
    cit = jnp.swapaxes(c_im.astype(F32), -1, -2)
    car = (crt[:, :, :, None, :] * pr[..., None] - cit[:, :, :, None, :] * pi[..., None]).reshape(2, g, n, (q + 1) * c16)
    cai = (crt[:, :, :, None, :] * pi[..., None] + cit[:, :, :, None, :] * pr[..., None]).reshape(2, g, n, (q + 1) * c16)
    kern = jnp.sum(bbr[..., None] * car[:, :, :, None, :] - bbi[..., None] * cai[:, :, :, None, :], axis=2)
    kf = kern[0, :, :, 0:qc]
    kb = kern[1].reshape(g, c16, q + 1, c16)[:, :, q - 1::-1].reshape(g, c16, qc)
    taps = (kf, kb, d_skip.astype(F32).reshape(g, c16, 1))

    bbr_t, bbi_t = jnp.swapaxes(bbr, -1, -2), jnp.swapaxes(bbi, -1, -2)
    prt, pit = jnp.swapaxes(pr, -1, -2), jnp.swapaxes(pi, -1, -2)

    def p_block(d, reverse):
        ppr = prt[d, :, q - 1::-1] if reverse else prt[d, :, 0:q]
        ppi = pit[d, :, q - 1::-1] if reverse else pit[d, :, 0:q]
        re = ppr[:, :, None, :] * bbr_t[d][:, None] - ppi[:, :, None, :] * bbi_t[d][:, None]
        im = ppr[:, :, None, :] * bbi_t[d][:, None] + ppi[:, :, None, :] * bbr_t[d][:, None]
        return re.reshape(g, qc, n), im.reshape(g, qc, n)

    pf_re, pf_im = p_block(0, True)
    pb_re, pb_im = p_block(1, False)
    p = jnp.concatenate([pf_re, pb_re, pf_im, pb_im], axis=-1)

    def r_block(d, reverse):
        if reverse:
            sel = lambda t: t[d].reshape(g, n, q + 1, c16)[:, :, q:0:-1].reshape(g, n, qc)
        else:
            sel = lambda t: t[d][:, :, c16:(q + 1) * c16]
        return sel(car), -sel(cai)

    rf_re, rf_im = r_block(0, False)
    rb_re, rb_im = r_block(1, True)
    r = jnp.concatenate([rf_re, rb_re, rf_im, rb_im], axis=1)
    a16 = jnp.stack([jnp.concatenate([pr[0, :, :, q], pr[1, :, :, q]], axis=-1),
                     jnp.concatenate([pi[0, :, :, q], pi[1, :, :, q]], axis=-1)], axis=1)
    return taps, p.astype(BF16), r.astype(BF16), a16


S5_LANE_GROUPS = 8
S5_BATCH = 4


def _s5_kernel(ut_ref, kf_ref, kb_ref, d_ref, p_ref, r_ref, a_ref, zt_ref,
               pi_ref, m_ref, x_ref, xp_ref, v_ref, sk_ref, srb_ref, y_ref, yp_ref, *, n_lat, n_ctx, bb):
    q, ng = S5_Q, S5_LANE_GROUPS
    qc = q * S5_GROUP
    width = ng * qc
    n_all = n_lat + n_ctx
    rows_lat = n_lat * bb

    @pl.when(jnp.logical_and(pl.program_id(0) == 0, pl.program_id(1) == 0))
    def _():
        src = lax.broadcasted_iota(jnp.int32, (width, qc), 0)
        dst = lax.broadcasted_iota(jnp.int32, (width, qc), 1)
        dest_in_group = ((src >> 7) << 4) + (src & 15)
        grp = (src >> 4) & (ng - 1)
        for g in range(ng):
            hit = jnp.logical_and(grp == g, dest_in_group == dst)
            pi_ref[:, g * qc:(g + 1) * qc] = jnp.where(hit, 1.0, 0.0).astype(BF16)

    @pl.when(pl.program_id(1) == 0)
    def _():
        lane = lax.broadcasted_iota(jnp.int32, (S5_GROUP, qc), 1)
        chan = lax.broadcasted_iota(jnp.int32, (S5_GROUP, qc), 0)
        for g in range(ng):
            kf, kb, dg = kf_ref[g], kb_ref[g], d_ref[g]
            for j in range(q):
                lo, hi = j * S5_GROUP, qc - (q - 1 - j) * S5_GROUP
                mf = jnp.where(lane >= lo, pltpu.roll(kf, lo, 1) if lo else kf, 0.0)
                mb = jnp.where(lane < hi, pltpu.roll(kb, hi % qc, 1) if hi % qc else kb, 0.0)
                skip = jnp.where(lane == lo + chan, dg, 0.0)
                m_ref[g, j * S5_GROUP:(j + 1) * S5_GROUP, :] = (mf + mb + skip).astype(BF16)

    for b in range(bb):
        for t in range(q):
            x_ref[t, pl.ds(b, n_all, stride=bb), :] = ut_ref[t, b].astype(F32)
    xb = jnp.concatenate([x_ref[t] for t in range(q)], axis=1).astype(BF16)
    for g in range(ng):
        cols = slice(g * qc, (g + 1) * qc)
        xg = _dot(xb, pi_ref[:, cols]).astype(BF16)
        xp_ref[:, cols] = xg
        v_ref[:, cols] = _dot(xg, p_ref[g])

    a = a_ref[...]
    a1 = jnp.concatenate([a[g, 0:1] for g in range(ng) for _ in range(2)], axis=1)
    a2 = jnp.concatenate([s * a[g, 1:2] for g in range(ng) for s in (-1.0, 1.0)], axis=1)
    fwd_lane = (lax.broadcasted_iota(jnp.int32, (bb, width), 1) & 127) < S5_STATE

    def partner(s):
        tiles = [s[:, i * 128:(i + 1) * 128] for i in range(2 * ng)]
        return jnp.concatenate([tiles[i ^ 1] for i in range(2 * ng)], axis=1)

    def advance(s, vf, vb):
        return a1 * s + a2 * partner(s) + jnp.where(fwd_lane, vf, vb)

    def scan(base, n, s0, store):
        def body(t, s):
            rf = pl.multiple_of((base + 2 * t) * bb, 2 * bb)
            rb = pl.multiple_of((base + n - 2 - 2 * t) * bb, 2 * bb)
            vf = v_ref[pl.ds(rf, 2 * bb), :]
            vb = v_ref[pl.ds(rb, 2 * bb), :]
            s1 = advance(s, vf[0:bb], vb[bb:2 * bb])
            s2 = advance(s1, vf[bb:2 * bb], vb[0:bb])
            if store:
                sk_ref[pl.ds(rf, 2 * bb), :] = jnp.concatenate([s, s1], axis=0)
                srb_ref[pl.ds(rb, 2 * bb), :] = jnp.concatenate([s1, s], axis=0)
            return s2
        return lax.fori_loop(0, n // 2, body, s0)

    s_ctx = scan(n_lat, n_ctx, jnp.zeros((bb, width), F32), False)
    scan(0, n_lat, s_ctx, True)
    lane = (lax.broadcasted_iota(jnp.int32, (rows_lat, width), 1) & 127) < S5_STATE
    st = jnp.where(lane, sk_ref[...], srb_ref[...]).astype(BF16)
    for g in range(ng):
        cols = slice(g * qc, (g + 1) * qc)
        y = _dot(xp_ref[0:rows_lat, cols], m_ref[g]) + _dot(st[:, cols], r_ref[g])
        y_ref[:, cols] = jax.nn.gelu(y).astype(BF16)
    yb = y_ref[...]
    for t in range(q):
        yp_ref[...] = _dot_nt(yb, pi_ref[t * 128:(t + 1) * 128, :])
        for b in range(bb):
            zt_ref[t, b] = yp_ref[pl.ds(b, n_lat, stride=bb), :].astype(BF16)


def _s5_mixer(ut, l, lt, mats):
    (kf, kb, dsk), p, r, a16 = mats
    q, b, _, width = ut.shape
    n_all, n_lat = lt // q, l // q
    ng = S5_LANE_GROUPS
    qc = q * S5_GROUP
    bb = min(S5_BATCH, b)
    nj = width // 128
    wspec = pl.BlockSpec((ng, qc, qc), lambda j, i: (j, 0, 0))
    tspec = pl.BlockSpec((ng, S5_GROUP, qc), lambda j, i: (j, 0, 0))
    return pl.pallas_call(
        functools.partial(_s5_kernel, n_lat=n_lat, n_ctx=n_all - n_lat, bb=bb),
        grid=(nj, b // bb),
        in_specs=[pl.BlockSpec((q, bb, n_all, 128), lambda j, i: (0, i, 0, j)),
                  tspec, tspec, pl.BlockSpec((ng, S5_GROUP, 1), lambda j, i: (j, 0, 0)), wspec, wspec,
                  pl.BlockSpec((ng, 2, 2 * S5_STATE), lambda j, i: (j, 0, 0))],
        out_specs=pl.BlockSpec((q, bb, n_lat, 128), lambda j, i: (0, i, 0, j)),
        out_shape=jax.ShapeDtypeStruct((q, b, n_lat, width), BF16),
        scratch_shapes=[pltpu.VMEM((ng * qc, ng * qc), BF16),
                        pltpu.VMEM((ng, qc, qc), BF16),
                        pltpu.VMEM((q, n_all * bb, 128), F32),
                        pltpu.VMEM((n_all * bb, ng * qc), BF16),
                        pltpu.VMEM((n_all * bb, ng * qc), F32),
                        pltpu.VMEM((n_lat * bb, ng * qc), F32),
                        pltpu.VMEM((n_lat * bb, ng * qc), F32),
                        pltpu.VMEM((n_lat * bb, ng * qc), BF16),
                        pltpu.VMEM((n_lat * bb, 128), F32)],
        compiler_params=_cparams(("arbitrary", "arbitrary")),
        name="s5_scan",
    )(ut, kf, kb, dsk, p, r, a16)


def _inv_unit_triangular_many(mats):
    c = mats[0].shape[0]
    eye = (lax.broadcasted_iota(jnp.int32, (c, c), 0) == lax.broadcasted_iota(jnp.int32, (c, c), 1)).astype(F32)
    prods = [eye - a for a in mats]
    pows = [a.astype(BF16) for a in mats]
    for _ in range(int(np.log2(c)) - 1):
        pows = [_dot(a, a).astype(BF16) for a in pows]
        prods = [p + _dot(p.astype(BF16), a) for p, a in zip(prods, pows)]
    return prods


def _gdn_prep_kernel(qkv_ref, cw_ref, ba_ref, bat_ref, prow_ref, pcol_ref,
                     u_ref, wq_ref, qk_ref, kdt_ref, dl_ref, *, n_lat_blocks, ctx_len):
    j = pl.program_id(1)
    rows = qkv_ref.shape[1]
    c = GDN_CHUNK
    nchunk = rows // c
    seg = jnp.where(j >= n_lat_blocks, ctx_len, GRID_W)
    x = qkv_ref[0].astype(F32)
    pos = lax.broadcasted_iota(jnp.int32, (rows, 1), 0) & (seg - 1)
    acc = x * cw_ref[2:3, :]
    for s in (-2, -1, 1, 2):
        shifted = pltpu.roll(x, (-s) % rows, 0)
        ok = jnp.logical_and(pos + s >= 0, pos + s < seg)
        acc = acc + jnp.where(ok, shifted, 0.0) * cw_ref[2 + s:3 + s, :]
    act = acc * _sigmoid(acc)

    ii = lax.broadcasted_iota(jnp.int32, (c, c), 0)
    jj = lax.broadcasted_iota(jnp.int32, (c, c), 1)
    incl = (ii >= jj, ii <= jj)
    strict = (ii > jj, ii < jj)
    tri_low = incl[0].astype(F32)
    tri_up = incl[1].astype(F32)
    eye_bf = (ii == jj).astype(BF16)
    ones = jnp.ones((c, c), F32)
    lane = lax.broadcasted_iota(jnp.int32, (c, 128), 1)
    prow = prow_ref[...]
    pcol = pcol_ref[...]
    hp = dict(precision=HIGHEST, preferred_element_type=F32)

    gates = []
    for ci in range(nchunk):
        ba = ba_ref[0, ci * c:(ci + 1) * c, :]
        g_all = -prow[0:1, :] * _softplus(ba + prow[1:2, :])
        g_all = jnp.where(jnp.logical_and(lane >= 8, lane < 16), g_all, 0.0)
        bat = bat_ref[0, ci]
        g_row = -pcol[:, 0:1] * _softplus(bat[8:16, :] + pcol[:, 1:2])
        gates.append(dict(beta=_sigmoid(ba),
                          gc=(jnp.dot(tri_low, g_all, **hp), jnp.dot(tri_up, g_all, **hp)),
                          gtot=jnp.dot(ones, g_all, **hp),
                          gr=(jnp.dot(g_row, tri_up, **hp), jnp.dot(g_row, tri_low, **hp))))

    qn, kn, vv = [], [], []
    for h in range(GDN_HEADS):
        qh = act[:, h * 128:(h + 1) * 128]
        kh = act[:, 512 + h * 128:512 + (h + 1) * 128]
        qn.append(qh * lax.rsqrt(jnp.sum(qh * qh, axis=-1, keepdims=True) + EPS) * (GDN_DK ** -0.5))
        kn.append(kh * lax.rsqrt(jnp.sum(kh * kh, axis=-1, keepdims=True) + EPS))
        vv.append(act[:, 1024 + h * 128:1024 + (h + 1) * 128])

    pairs = [(ci, h) for ci in range(nchunk) for h in range(GDN_HEADS)]
    sl = lambda t, ci: t[ci * c:(ci + 1) * c]
    kb = {p: sl(kn[p[1]], p[0]).astype(BF16) for p in pairs}
    kk = {p: _dot_nt(kb[p], kb[p]) for p in pairs}
    qkm = {p: _dot_nt(sl(qn[p[1]], p[0]).astype(BF16), kb[p]) for p in pairs}

    probs = [(ci, h, d) for ci in range(nchunk) for h in range(GDN_HEADS) for d in range(2)]
    beta, gc, gt, decay, a_mats = {}, {}, {}, {}, []
    for (ci, h, d) in probs:
        col = d * GDN_HEADS + h
        gi = gates[ci]
        beta[ci, h, d] = gi["beta"][:, col:col + 1]
        gc[ci, h, d] = gi["gc"][d][:, 8 + col:9 + col]
        gt[ci, h, d] = gi["gtot"][:, 8 + col:9 + col]
        grow = gi["gr"][d][col:col + 1, :]
        decay[ci, h, d] = jnp.exp(jnp.where(incl[d], gc[ci, h, d] - grow, -jnp.inf))
        a_mats.append(jnp.where(strict[d], beta[ci, h, d] * kk[ci, h] * decay[ci, h, d], 0.0))
    tinv = _inv_unit_triangular_many(a_mats)

    eg, sols, kdts = {}, {}, {}
    for n, (ci, h, d) in enumerate(probs):
        p = (ci, h, d)
        eg[p] = jnp.exp(gc[p])
        kh = sl(kn[h], ci)
        rhs = jnp.concatenate([sl(vv[h], ci) * beta[p], kh * (beta[p] * eg[p])], axis=-1).astype(BF16)
        sols[p] = _dot(tinv[n].astype(BF16), rhs)
        kdts[p] = _dot_tn((kh * jnp.exp(gt[p] - gc[p])).astype(BF16), eye_bf)
    for (ci, h, d) in probs:
        p = (ci, h, d)
        u_ref[0, d, h, ci] = sols[p][:, 0:128]
        wq_ref[0, d, h, ci, 0:c, :] = sols[p][:, 128:256].astype(BF16)
        wq_ref[0, d, h, ci, c:2 * c, :] = (sl(qn[h], ci) * eg[p]).astype(BF16)
        qk_ref[0, d, h, ci] = (qkm[ci, h] * decay[p]).astype(BF16)
        kdt_ref[0, d, h, ci] = kdts[p].astype(BF16)
        dl_ref[0, d, h, ci] = jnp.broadcast_to(jnp.exp(gt[p][0:8, :]), (8, 128))


def _gdn_prepare(qkv, conv_w, ba, bat_chunks, prow, pcol, l, lt):
    b, _, width = qkv.shape
    tm = ROW_TILE
    c = GDN_CHUNK
    cpb = tm // c
    nblk = lt // tm
    nch = lt // c
    hd = (b, 2, GDN_HEADS, nch)

    def blk(shape_tail, dtype):
        return (pl.BlockSpec((1, 2, GDN_HEADS, cpb) + shape_tail, lambda i, j: (i, 0, 0, j, 0, 0)),
                jax.ShapeDtypeStruct(hd + shape_tail, dtype))

    specs = [blk((c, 128), F32), blk((2 * c, 128), BF16), blk((c, c), BF16), blk((GDN_DK, c), BF16),
             blk((8, 128), F32)]
    return pl.pallas_call(
        functools.partial(_gdn_prep_kernel, n_lat_blocks=l // tm, ctx_len=lt - l),
        grid=(b, nblk),
        in_specs=[pl.BlockSpec((1, tm, width), lambda i, j: (i, j, 0)),
                  pl.BlockSpec((8, width), lambda i, j: (0, 0)),
                  pl.BlockSpec((1, tm, 128), lambda i, j: (i, j, 0)),
                  pl.BlockSpec((1, cpb, 16, c), lambda i, j: (i, j, 0, 0)),
                  pl.BlockSpec((2, 128), lambda i, j: (0, 0)),
                  pl.BlockSpec((8, 2), lambda i, j: (0, 0))],
        out_specs=[s for s, _ in specs],
        out_shape=[o for _, o in specs],
        compiler_params=_cparams(("parallel", "parallel")),
        name="gdn_prepare",
    )(qkv, conv_w, ba, bat_chunks, prow, pcol)


SCAN_GROUP = 2


def _gdn_scan_kernel(uf, wqf, qkf, kdtf, dlf, ub, wqb, qkb, kdtb, dlb, of_ref, ob_ref, s_ref):
    t = pl.program_id(0)
    nb = s_ref.shape[0]
    c = GDN_CHUNK

    @pl.when(t == 0)
    def _():
        s_ref[...] = jnp.zeros_like(s_ref)

    refs = ((uf, wqf, qkf, kdtf, dlf, of_ref), (ub, wqb, qkb, kdtb, dlb, ob_ref))
    for b0 in range(0, nb, SCAN_GROUP):
        chains = [(bi, d, h) for bi in range(b0, min(b0 + SCAN_GROUP, nb)) for d in range(2)
                  for h in range(GDN_HEADS)]
        s = {k: s_ref[k[0], k[1], k[2]] for k in chains}
        sb = {k: s[k].astype(BF16) for k in chains}
        r = {k: _dot(refs[k[1]][1][k[0], 0, k[2], 0], sb[k]) for k in chains}
        vb = {k: (refs[k[1]][0][k[0], 0, k[2], 0] - r[k][0:c]).astype(BF16) for k in chains}
        o = {k: r[k][c:2 * c] + _dot(refs[k[1]][2][k[0], 0, k[2], 0], vb[k]) for k in chains}
        sn = {k: s[k] * refs[k[1]][4][k[0], 0, k[2], 0][0:1, :] + _dot(refs[k[1]][3][k[0], 0, k[2], 0], vb[k])
              for k in chains}
        for k in chains:
            s_ref[k[0], k[1], k[2]] = sn[k]
            refs[k[1]][5][k[0], :, k[2] * 128:(k[2] + 1) * 128] = o[k]


def _gdn_scan(prep, l):
    u = prep[0]
    b, _, heads, nch, c, _ = u.shape
    n_lat = l // c
    n_ctx = nch - n_lat
    lt = nch * c

    def fwd_chunk(t):
        return jnp.where(t < n_ctx, n_lat + t, t - n_ctx)

    def bwd_chunk(t):
        return nch - 1 - t

    def spec(arr, chunk_of, d):
        tail = arr.shape[4:]
        return pl.BlockSpec((b, 1, heads, 1) + tail, lambda t: (0, d, 0, chunk_of(t), 0, 0))

    ins = [spec(a, fwd_chunk, 0) for a in prep] + [spec(a, bwd_chunk, 1) for a in prep]
    width = heads * 128
    return pl.pallas_call(
        _gdn_scan_kernel,
        grid=(nch,),
        in_specs=ins,
        out_specs=[pl.BlockSpec((b, c, width), lambda t: (0, fwd_chunk(t), 0)),
                   pl.BlockSpec((b, c, width), lambda t: (0, bwd_chunk(t), 0))],
        out_shape=[jax.ShapeDtypeStruct((b, lt, width), F32)] * 2,
        scratch_shapes=[pltpu.VMEM((b, 2, heads, GDN_DK, 128), F32)],
        compiler_params=_cparams(("arbitrary",)),
        name="gdn_scan",
    )(*prep, *prep)


def _merge_kernel(x_ref, z_ref, of_ref, ob_ref, gate_ref, br_ref, gt1_ref, sc2_ref, sh2_ref,
                  wglu_ref, bglu_ref, gnorm_ref, wa_ref, wb_ref, wo_ref, n2_ref, wr_ref, wrhi_ref, brt_ref,
                  x1_ref, h2_ref, lg_ref, zs_ref):
    nchunk = zs_ref.shape[1] // S5_Q
    for t in range(S5_Q):
        zt = z_ref[t, 0].astype(F32)
        for c in range(zs_ref.shape[0]):
            zs_ref[c, pl.ds(t, nchunk, stride=S5_Q), :] = zt[:, c * 128:(c + 1) * 128]
    zf = jnp.concatenate([zs_ref[c] for c in range(zs_ref.shape[0])], axis=1)
    z = zf.astype(BF16)
    ya = zf * _sigmoid(_dot(z, wglu_ref[...]) + bglu_ref[...])
    o = of_ref[0] + ob_ref[0]
    gate = gate_ref[0].astype(F32)
    parts = []
    for h in range(GDN_HEADS):
        oh = o[:, h * 128:(h + 1) * 128]
        parts.append(oh * lax.rsqrt(jnp.mean(oh * oh, axis=-1, keepdims=True) + EPS) * gnorm_ref[...])
    yb = jnp.concatenate(parts, axis=-1) * (gate * _sigmoid(gate))
    br = br_ref[0].astype(F32)
    d = x_ref.shape[2]
    ga = _sigmoid(br[:, 0:d])
    gb = _sigmoid(br[:, d:2 * d])
    m = ga * _dot(ya.astype(BF16), wa_ref[...]) + gb * _dot(yb.astype(BF16), wb_ref[...])
    mix = _dot(m.astype(BF16), wo_ref[...])
    x1 = x_ref[0] + gt1_ref[0] * mix
    x1_ref[0] = x1
    y2 = x1 * lax.rsqrt(jnp.mean(x1 * x1, axis=-1, keepdims=True) + EPS) * n2_ref[...]
    h2 = y2 * (1.0 + sc2_ref[0]) + sh2_ref[0]
    h2b = h2.astype(BF16)
    h2_ref[0, 0] = h2b
    h2_ref[1, 0] = h2b
    h2_lo = (h2 - h2b.astype(F32)).astype(BF16)
    rl = _dot(h2b, wr_ref[...])
    lg_ref[0] = rl[:, 0:128] + rl[:, 128:256] + _dot(h2_lo, wrhi_ref[...]) + brt_ref[...]


def _merge(x, z, o_f, o_b, gate, br, mod3, wglu, bglu, gnorm, wa, wb, wo, norm2, wr, wr_hi, brt):
    b, l, d = x.shape
    tm = min(MERGE_TILE, l)
    tok = lambda width: pl.BlockSpec((1, tm, width), lambda i, j: (i, j, 0))
    modspec = lambda k: pl.BlockSpec((1, 1, d), lambda i, j: (i * 6 + k, 0, 0))
    full = lambda arr: pl.BlockSpec(arr.shape, lambda i, j: (0,) * arr.ndim)
    consts = [wglu, bglu, gnorm, wa, wb, wo, norm2, wr, wr_hi, brt]
    return pl.pallas_call(
        _merge_kernel,
        grid=(b, l // tm),
        in_specs=[tok(d), pl.BlockSpec((S5_Q, 1, tm // S5_Q, 512), lambda i, j: (0, i, j, 0)),
                  tok(512), tok(512), tok(512), tok(2 * d),
                  modspec(2), modspec(4), modspec(3)] + [full(a) for a in consts],
        out_specs=[tok(d), pl.BlockSpec((2, 1, tm, d), lambda i, j: (0, i, j, 0)), tok(128)],
        out_shape=[jax.ShapeDtypeStruct((b, l, d), F32),
                   jax.ShapeDtypeStruct((2, b, l, d), BF16),
                   jax.ShapeDtypeStruct((b, l, 128), F32)],
        scratch_shapes=[pltpu.VMEM((4, tm, 128), F32)],
        compiler_params=_cparams(("parallel", "parallel")),
        name="branch_merge",
    )(x, z, o_f, o_b, gate, br, mod3, mod3, mod3, *consts)


def _moe_kernel(te_ref, nu_ref, x_ref, wgu_ref, bgu_ref, wd_ref, bd_ref, y_ref, wgu_bf, wd_bf):
    i = pl.program_id(0)
    changed = jnp.logical_or(i == 0, te_ref[i] != te_ref[jnp.maximum(i - 1, 0)])

    @pl.when(jnp.logical_and(changed, i < nu_ref[0]))
    def _():
        wgu_bf[...] = wgu_ref[0].astype(BF16)
        wd_bf[...] = wd_ref[0].astype(BF16)

    @pl.when(i < nu_ref[0])
    def _():
        de = wd_ref.shape[1]
        half = x_ref.shape[0] // 2
        parts = (slice(0, half), slice(half, 2 * half))
        gus = [_dot(x_ref[rows, :], wgu_bf[...]) + bgu_ref[0] for rows in parts]
        acts = []
        for gu in gus:
            gate = jnp.minimum(gu[:, 0:de], SWIGLU_LIMIT)
            up = jnp.clip(gu[:, de:2 * de], -SWIGLU_LIMIT, SWIGLU_LIMIT)
            acts.append(((up + 1.0) * gate * _sigmoid(gate * SWIGLU_ALPHA)).astype(BF16))
        for rows, act in zip(parts, acts):
            y_ref[rows, :] = (_dot(act, wd_bf[...]) + bd_ref[0]).astype(y_ref.dtype)

    @pl.when(i >= nu_ref[0])
    def _():
        y_ref[...] = jnp.zeros_like(y_ref)


def _moe_experts(xs, tile_expert, n_used, w_gate_up, b_gate_up, w_down, b_down):
    p, d = xs.shape
    tm = MOE_TILE
    e, _, n2 = w_gate_up.shape
    de = w_down.shape[1]
    grid_spec = pltpu.PrefetchScalarGridSpec(
        num_scalar_prefetch=2,
        grid=(p // tm,),
        in_specs=[pl.BlockSpec((tm, d), lambda i, te, nu: (jnp.minimum(i, jnp.maximum(nu[0] - 1, 0)), 0)),
                  pl.BlockSpec((1, d, n2), lambda i, te, nu: (te[i], 0, 0)),
                  pl.BlockSpec((1, 1, n2), lambda i, te, nu: (te[i], 0, 0)),
                  pl.BlockSpec((1, de, d), lambda i, te, nu: (te[i], 0, 0)),
                  pl.BlockSpec((1, 1, d), lambda i, te, nu: (te[i], 0, 0))],
        out_specs=pl.BlockSpec((tm, d), lambda i, te, nu: (i, 0)),
        scratch_shapes=[pltpu.VMEM((d, n2), BF16), pltpu.VMEM((de, d), BF16)],
    )
    return pl.pallas_call(
        _moe_kernel,
        grid_spec=grid_spec,
        out_shape=jax.ShapeDtypeStruct((p, d), BF16),
        compiler_params=_cparams(("arbitrary",)),
        name="moe_experts",
    )(tile_expert, n_used, xs, w_gate_up, b_gate_up.reshape(e, 1, n2), w_down, b_down.reshape(e, 1, d))


def _route(logits, n_tokens):
    tm = MOE_TILE
    top_val, top_idx = lax.top_k(logits, TOP_K)
    weights = jax.nn.softmax(top_val, axis=-1)
    flat_e = top_idx.reshape(-1).astype(jnp.int32)
    n_assign = flat_e.shape[0]
    iota = jnp.arange(n_assign, dtype=jnp.int32)
    sorted_e, order = lax.sort((flat_e, iota), num_keys=1, is_stable=True)
    counts = jnp.sum(jax.nn.one_hot(flat_e, N_EXPERTS, dtype=jnp.int32), axis=0)
    padded = ((counts + tm - 1) // tm) * tm
    pad_end = jnp.cumsum(padded)
    pad_start = pad_end - padded
    raw_start = jnp.cumsum(counts) - counts
    dest = pad_start[sorted_e] + (iota - raw_start[sorted_e])
    _, pos = lax.sort((order, dest), num_keys=1)
    pos = pos.reshape(n_tokens, TOP_K)
    n_rows = n_assign + N_EXPERTS * tm
    n_tiles = n_rows // tm
    n_used = (pad_end[-1] // tm).astype(jnp.int32)
    tile_start = jnp.arange(n_tiles, dtype=jnp.int32) * tm
    tile_expert = jnp.sum((tile_start[:, None] >= pad_end[None, :]).astype(jnp.int32), axis=1)
    last_e = jnp.sum((pad_end[-1] - 1 >= pad_end).astype(jnp.int32))
    tile_expert = jnp.minimum(tile_expert, last_e).astype(jnp.int32)
    lane = jnp.arange(tm, dtype=jnp.int32)[None, :]
    off = lane + (tile_start - pad_start[tile_expert])[:, None]
    valid = off < counts[tile_expert][:, None]
    sidx = jnp.clip(raw_start[tile_expert][:, None] + off, 0, n_assign - 1)
    filler = (tile_start[:, None] + lane) % n_tokens
    src_token = jnp.where(valid, (order // TOP_K)[sidx], filler).reshape(-1)
    return weights, src_token, pos, tile_expert, n_used.reshape(1)


def _final_kernel(x1_ref, y0_ref, y1_ref, y2_ref, y3_ref, wt_ref, gt2_ref, nf_ref, o_ref):
    wt = wt_ref[0]
    moe = (wt[:, 0:1] * y0_ref[0, 0].astype(F32) + wt[:, 1:2] * y1_ref[0, 0].astype(F32)
           + wt[:, 2:3] * y2_ref[0, 0].astype(F32) + wt[:, 3:4] * y3_ref[0, 0].astype(F32))
    x2 = x1_ref[0] + gt2_ref[0] * moe
    o_ref[0] = x2 * lax.rsqrt(jnp.mean(x2 * x2, axis=-1, keepdims=True) + EPS) * nf_ref[...]


def _final(x1, yg, wt, mod3, norm_f):
    b, l, d = x1.shape
    tm = ROW_TILE
    tok = lambda width: pl.BlockSpec((1, tm, width), lambda i, j: (i, j, 0))
    ysel = lambda k: pl.BlockSpec((1, 1, tm, d), lambda i, j: (k, i, j, 0))
    return pl.pallas_call(
        _final_kernel,
        grid=(b, l // tm),
        in_specs=[tok(d)] + [ysel(k) for k in range(TOP_K)] + [tok(128), pl.BlockSpec((1, 1, d), lambda i, j: (i * 6 + 5, 0, 0)),
                                 pl.BlockSpec((1, d), lambda i, j: (0, 0))],
        out_specs=tok(d),
        out_shape=jax.ShapeDtypeStruct((b, l, d), F32),
        compiler_params=_cparams(("parallel", "parallel")),
        name="combine_final_norm",
    )(x1, *([yg] * TOP_K), wt, mod3, norm_f.reshape(1, d))


def _split_hi_lo(w):
    hi = w.astype(BF16)
    lo = (w - hi.astype(F32)).astype(BF16)
    return hi, lo


def kernel(x, c, ctx, c_ctx, w_mod, b_mod, norm1, w_in, s5_lam_re, s5_lam_im, s5_log_step, s5_b_re, s5_b_im, s5_c_re, s5_c_im, s5_d, s5_w_glu, s5_b_glu, gdn_conv, gdn_a_log, gdn_dt_bias, gdn_norm, w_branch_a, w_branch_b, w_out, norm2, w_router, b_router, w_gate_up, b_gate_up, w_down, b_down, norm_f):
    b, l, d = x.shape
    lc = ctx.shape[1]
    depth = w_mod.shape[0]
    assert depth == 1, "single-layer block: the context stream has no consumer after the token mixer"
    assert l % ROW_TILE == 0 and lc == ROW_TILE and lc & (lc - 1) == 0 and b <= 8
    ly = 0

    cc = jnp.zeros((16, d), F32).at[0:b].set(c).at[b].set(c_ctx)
    mod = _modulation(cc, w_mod[ly], b_mod[ly])
    mod3 = mod[0:b + 1].reshape((b + 1) * 6, 1, d)

    wi = w_in[ly]
    o_u, o_qkv, o_gate, o_ba, o_br = 0, 512, 2048, 2560, 2576
    w_ba = wi[:, o_ba:o_br]
    ba_hi, ba_lo = _split_hi_lo(w_ba)
    pad = lambda t: jnp.pad(t, ((0, 0), (0, 128 - t.shape[1])))
    w_main = jnp.concatenate([wi[:, o_u:o_ba].astype(BF16), wi[:, o_br:].astype(BF16), pad(ba_hi), pad(ba_lo)], axis=1)
    wt = jnp.concatenate([ba_hi.T, ba_lo.T], axis=0)
    ut, qkv, gate, br, ba, bat = _in_projection(x, ctx, mod3, norm1[ly], w_main, pad(ba_hi), wt, ba_hi.T)

    mats = _s5_matrices(s5_lam_re[ly], s5_lam_im[ly], s5_log_step[ly], s5_b_re[ly], s5_b_im[ly],
                        s5_c_re[ly], s5_c_im[ly], s5_d[ly])
    lt = l + lc
    z = _s5_mixer(ut, l, lt, mats)

    nch = lt // GDN_CHUNK
    bat_chunks = jnp.transpose(bat[:, :, 0:lt].reshape(b, 16, nch, GDN_CHUNK), (0, 2, 1, 3))
    ea = jnp.exp(gdn_a_log[ly].astype(F32)).reshape(-1)
    dtb = gdn_dt_bias[ly].astype(F32).reshape(-1)
    prow = jnp.zeros((2, 128), F32).at[0, 8:16].set(ea).at[1, 8:16].set(dtb)
    pcol = jnp.stack([ea, dtb], axis=1)
    conv_w = jnp.zeros((8, qkv.shape[2]), F32).at[0:CONV_K].set(gdn_conv[ly].astype(F32))
    prep = _gdn_prepare(qkv, conv_w, ba, bat_chunks, prow, pcol, l, lt)
    o_f, o_b = _gdn_scan(prep, l)

    wr_hi, wr_lo = _split_hi_lo(jnp.pad(w_router[ly].astype(F32), ((0, 0), (0, 128 - N_EXPERTS))))
    wr = jnp.concatenate([wr_hi, wr_lo], axis=1)
    brt = jnp.pad(b_router[ly].astype(F32), (0, 128 - N_EXPERTS)).reshape(1, 128)
    x1, h2, logits = _merge(x, z, o_f, o_b, gate, br, mod3,
                            s5_w_glu[ly].astype(BF16), s5_b_glu[ly].astype(F32).reshape(1, -1),
                            gdn_norm[ly].astype(F32).reshape(1, -1),
                            w_branch_a[ly].astype(BF16), w_branch_b[ly].astype(BF16), w_out[ly].astype(BF16),
                            norm2[ly].astype(F32).reshape(1, d), wr, wr_hi, brt)

    n_tok = b * l
    weights, src_token, pos, tile_expert, n_used = _route(logits.reshape(n_tok, 128)[:, 0:N_EXPERTS], n_tok)
    h2f = h2.reshape(2 * n_tok, d)
    xs = h2f[src_token]
    ys = _moe_experts(xs, tile_expert, n_used, w_gate_up[ly], b_gate_up[ly], w_down[ly], b_down[ly])
    yg = ys[pos.T.reshape(-1)].reshape(TOP_K, b, l, d)
    wt4 = jnp.pad(weights, ((0, 0), (0, 128 - TOP_K))).reshape(b, l, 128)
    return _final(x1, yg, wt4, mod3, norm_f)
```

```python
import functools

import numpy as np
import jax
import jax.numpy as jnp
from jax import lax
from jax.experimental import pallas as pl
from jax.experimental.pallas import tpu as pltpu

F32 = jnp.float32
BF16 = jnp.bfloat16
HIGHEST = lax.Precision.HIGHEST

EPS = 1e-6
GRID_W = 64

S5_GROUP = 16
S5_STATE = 64
S5_Q = 16

GDN_HEADS = 4
GDN_DK = 128
GDN_CHUNK = 64
CONV_K = 5

N_EXPERTS = 32
TOP_K = 4
SWIGLU_LIMIT = 7.0
SWIGLU_ALPHA = 1.702

ROW_TILE = 256
MERGE_TILE = 512
IN_TILE = 256
MOE_TILE = 512
VMEM_LIMIT = 56 * 1024 * 1024


def _cparams(sem):
    return pltpu.CompilerParams(dimension_semantics=sem, vmem_limit_bytes=VMEM_LIMIT)


def _sigmoid(v):
    return 1.0 / (1.0 + jnp.exp(-v))


def _softplus(v):
    return jnp.maximum(v, 0.0) + jnp.log(1.0 + jnp.exp(-jnp.abs(v)))


def _dot(a, b):
    return jnp.dot(a, b, preferred_element_type=F32)


def _dot_nt(a, b):
    return lax.dot_general(a, b, (((1,), (1,)), ((), ())), preferred_element_type=F32)


def _dot_tn(a, b):
    return lax.dot_general(a, b, (((0,), (0,)), ((), ())), preferred_element_type=F32)


def _mod_kernel(c_ref, w_ref, b_ref, o_ref):
    cc = c_ref[...]
    s = cc * _sigmoid(cc)
    o_ref[...] = jnp.dot(s, w_ref[...], precision=HIGHEST, preferred_element_type=F32) + b_ref[...]


def _modulation(cc, w_mod, b_mod):
    rows, d = cc.shape
    n = w_mod.shape[1]
    bn = d
    return pl.pallas_call(
        _mod_kernel,
        grid=(n // bn,),
        in_specs=[pl.BlockSpec((rows, d), lambda j: (0, 0)),
                  pl.BlockSpec((d, bn), lambda j: (0, j)),
                  pl.BlockSpec((1, bn), lambda j: (0, j))],
        out_specs=pl.BlockSpec((rows, bn), lambda j: (0, j)),
        out_shape=jax.ShapeDtypeStruct((rows, n), F32),
        compiler_params=_cparams(("arbitrary",)),
        name="adaln_mod",
    )(cc, w_mod, b_mod.reshape(1, n))


def _inproj_kernel(x_ref, c_ref, sc_ref, sh_ref, g_ref, w_ref, wlo_ref, wt_ref, wtlo_ref,
                   ut_ref, qkv_ref, gate_ref, br_ref, ba_ref, bat_ref, us_ref, *, n_lat_blocks):
    j = pl.program_id(1)
    is_ctx = j >= n_lat_blocks
    x = x_ref[0]
    lc = c_ref.shape[1]
    x = jnp.where(is_ctx, jnp.concatenate([c_ref[0], x[lc:]], axis=0) if lc < x.shape[0] else c_ref[0], x)
    ms = jnp.mean(x * x, axis=-1, keepdims=True)
    y = x * lax.rsqrt(ms + EPS) * g_ref[...]
    h = y * (1.0 + sc_ref[0]) + sh_ref[0]
    h_hi = h.astype(BF16)
    h_lo = (h - h_hi.astype(F32)).astype(BF16)
    z = _dot(h_hi, w_ref[...])
    nchunk = us_ref.shape[1] // S5_Q
    for c in range(us_ref.shape[0]):
        us_ref[c] = z[:, c * 128:(c + 1) * 128]
        for t in range(S5_Q):
            ut_ref[t, 0, :, c * 128:(c + 1) * 128] = us_ref[c, pl.ds(t, nchunk, stride=S5_Q), :].astype(BF16)
    qkv_ref[0] = z[:, 512:2048].astype(BF16)
    gate_ref[0] = z[:, 2048:2560].astype(BF16)
    br_ref[0] = z[:, 2560:4608].astype(BF16)
    ba_ref[0] = z[:, 4608:4736] + z[:, 4736:4864] + _dot(h_lo, wlo_ref[...])
    rt = _dot_nt(wt_ref[...], h_hi)
    bat_ref[0] = rt[0:16] + rt[16:32] + _dot_nt(wtlo_ref[...], h_lo)


def _in_projection(x, ctx, mod3, norm1, w_main, w_lo, wt, wt_lo):
    b, l, d = x.shape
    lc = ctx.shape[1]
    tm = min(IN_TILE, l)
    assert l % tm == 0 and lc <= tm
    nl = l // tm
    nb = mod3.shape[0] // 6 - 1
    lt = (nl + 1) * tm
    ncols = w_main.shape[1]

    def x_map(i, j):
        return (i, jnp.minimum(j, nl - 1), 0)

    def mod_map(k):
        return lambda i, j: (jnp.where(j >= nl, nb, i) * 6 + k, 0, 0)

    def tok(width):
        return pl.BlockSpec((1, tm, width), lambda i, j: (i, j, 0))

    const2 = lambda i, j: (0, 0)
    outs = pl.pallas_call(
        functools.partial(_inproj_kernel, n_lat_blocks=nl),
        grid=(b, nl + 1),
        in_specs=[pl.BlockSpec((1, tm, d), x_map),
                  pl.BlockSpec((1, lc, d), lambda i, j: (i, 0, 0)),
                  pl.BlockSpec((1, 1, d), mod_map(1)),
                  pl.BlockSpec((1, 1, d), mod_map(0)),
                  pl.BlockSpec((1, d), const2),
                  pl.BlockSpec((d, ncols), const2),
                  pl.BlockSpec((d, 128), const2),
                  pl.BlockSpec((32, d), const2),
                  pl.BlockSpec((16, d), const2)],
        out_specs=[pl.BlockSpec((S5_Q, 1, tm // S5_Q, 512), lambda i, j: (0, i, j, 0)),
                   tok(1536), tok(512), tok(2048), tok(128),
                   pl.BlockSpec((1, 16, tm), lambda i, j: (i, 0, j))],
        out_shape=[jax.ShapeDtypeStruct((S5_Q, b, lt // S5_Q, 512), BF16),
                   jax.ShapeDtypeStruct((b, lt, 1536), BF16),
                   jax.ShapeDtypeStruct((b, lt, 512), BF16),
                   jax.ShapeDtypeStruct((b, lt, 2048), BF16),
                   jax.ShapeDtypeStruct((b, lt, 128), F32),
                   jax.ShapeDtypeStruct((b, 16, lt), F32)],
        scratch_shapes=[pltpu.VMEM((4, tm, 128), F32)],
        compiler_params=_cparams(("parallel", "arbitrary")),
        name="in_projection",
    )(x, ctx, mod3, mod3, norm1.reshape(1, d), w_main, w_lo, wt, wt_lo)
    return outs


def _s5_matrices(lam_re, lam_im, log_step, b_re, b_im, c_re, c_im, d_skip):
    q, c16 = S5_Q, S5_GROUP
    qc = q * c16
    lr = jnp.minimum(lam_re.astype(F32), -1e-4)
    li = lam_im.astype(F32)
    dt = jnp.exp(log_step.astype(F32))[..., None]
    g, n = lr.shape[1], lr.shape[2]
    pw = jnp.arange(q + 1, dtype=F32)
    mag = jnp.exp(pw * (lr * dt)[..., None])
    ang = pw * (li * dt)[..., None]
    pr, pi = mag * jnp.cos(ang), mag * jnp.sin(ang)
    ar, ai = pr[..., 1], pi[..., 1]
    den = lr * lr + li * li
    fr = ((ar - 1.0) * lr + ai * li) / den
    fi = (ai * lr - (ar - 1.0) * li) / den
    br, bi = b_re.astype(F32), b_im.astype(F32)
    bbr = fr[..., None] * br - fi[..., None] * bi
    bbi = fr[..., None] * bi + fi[..., None] * br
    crt = jnp.swapaxes(c_re.astype(F32), -1, -2)
    cit = jnp.swapaxes(c_im.astype(F32), -1, -2)
    car = (crt[:, :, :, None, :] * pr[..., None] - cit[:, :, :, None, :] * pi[..., None]).reshape(2, g, n, (q + 1) * c16)
    cai = (crt[:, :, :, None, :] * pi[..., None] + cit[:, :, :, None, :] * pr[..., None]).reshape(2, g, n, (q + 1) * c16)
    kern = jnp.sum(bbr[..., None] * car[:, :, :, None, :] - bbi[..., None] * cai[:, :, :, None, :], axis=2)
    kf = kern[0, :, :, 0:qc]
    kb = kern[1].reshape(g, c16, q + 1, c16)[:, :, q - 1::-1].reshape(g, c16, qc)
    taps = (kf, kb, d_skip.astype(F32).reshape(g, c16, 1))

    bbr_t, bbi_t = jnp.swapaxes(bbr, -1, -2), jnp.swapaxes(bbi, -1, -2)
    prt, pit = jnp.swapaxes(pr, -1, -2), jnp.swapaxes(pi, -1, -2)

    def p_block(d, reverse):
        ppr = prt[d, :, q - 1::-1] if reverse else prt[d, :, 0:q]
        ppi = pit[d, :, q - 1::-1] if reverse else pit[d, :, 0:q]
        re = ppr[:, :, None, :] * bbr_t[d][:, None] - ppi[:, :, None, :] * bbi_t[d][:, None]
        im = ppr[:, :, None, :] * bbi_t[d][:, None] + ppi[:, :, None, :] * bbr_t[d][:, None]
        return re.reshape(g, qc, n), im.reshape(g, qc, n)

    pf_re, pf_im = p_block(0, True)
    pb_re, pb_im = p_block(1, False)
    p = jnp.concatenate([pf_re, pb_re, pf_im, pb_im], axis=-1)

    def r_block(d, reverse):
        if reverse:
            sel = lambda t: t[d].reshape(g, n, q + 1, c16)[:, :, q:0:-1].reshape(g, n, qc)
        else:
            sel = lambda t: t[d][:, :, c16:(q + 1) * c16]
        return sel(car), -sel(cai)

    rf_re, rf_im = r_block(0, False)
    rb_re, rb_im = r_block(1, True)
    r = jnp.concatenate([rf_re, rb_re, rf_im, rb_im], axis=1)
    a16 = jnp.stack([jnp.concatenate([pr[0, :, :, q], pr[1, :, :, q]], axis=-1),
                     jnp.concatenate([pi[0, :, :, q], pi[1, :, :, q]], axis=-1)], axis=1)
    return taps, p.astype(BF16), r.astype(BF16), a16


S5_LANE_GROUPS = 8
S5_BATCH = 4


def _s5_kernel(ut_ref, kf_ref, kb_ref, d_ref, p_ref, r_ref, a_ref, zt_ref,
               pi_ref, m_ref, x_ref, xp_ref, v_ref, sk_ref, srb_ref, y_ref, yp_ref, *, n_lat, n_ctx, bb):
    q, ng = S5_Q, S5_LANE_GROUPS
    qc = q * S5_GROUP
    width = ng * qc
    n_all = n_lat + n_ctx
    rows_lat = n_lat * bb

    @pl.when(jnp.logical_and(pl.program_id(0) == 0, pl.program_id(1) == 0))
    def _():
        src = lax.broadcasted_iota(jnp.int32, (width, qc), 0)
        dst = lax.broadcasted_iota(jnp.int32, (width, qc), 1)
        dest_in_group = ((src >> 7) << 4) + (src & 15)
        grp = (src >> 4) & (ng - 1)
        for g in range(ng):
            hit = jnp.logical_and(grp == g, dest_in_group == dst)
            pi_ref[:, g * qc:(g + 1) * qc] = jnp.where(hit, 1.0, 0.0).astype(BF16)

    @pl.when(pl.program_id(1) == 0)
    def _():
        lane = lax.broadcasted_iota(jnp.int32, (S5_GROUP, qc), 1)
        chan = lax.broadcasted_iota(jnp.int32, (S5_GROUP, qc), 0)
        for g in range(ng):
            kf, kb, dg = kf_ref[g], kb_ref[g], d_ref[g]
            for j in range(q):
                lo, hi = j * S5_GROUP, qc - (q - 1 - j) * S5_GROUP
                mf = jnp.where(lane >= lo, pltpu.roll(kf, lo, 1) if lo else kf, 0.0)
                mb = jnp.where(lane < hi, pltpu.roll(kb, hi % qc, 1) if hi % qc else kb, 0.0)
                skip = jnp.where(lane == lo + chan, dg, 0.0)
                m_ref[g, j * S5_GROUP:(j + 1) * S5_GROUP, :] = (mf + mb + skip).astype(BF16)

    for b in range(bb):
        for t in range(q):
            x_ref[t, pl.ds(b, n_all, stride=bb), :] = ut_ref[t, b].astype(F32)
    xb = jnp.concatenate([x_ref[t] for t in range(q)], axis=1).astype(BF16)
    for g in range(ng):
        cols = slice(g * qc, (g + 1) * qc)
        xg = _dot(xb, pi_ref[:, cols]).astype(BF16)
        xp_ref[:, cols] = xg
        v_ref[:, cols] = _dot(xg, p_ref[g])

    a = a_ref[...]
    a1 = jnp.concatenate([a[g, 0:1] for g in range(ng) for _ in range(2)], axis=1)
    a2 = jnp.concatenate([s * a[g, 1:2] for g in range(ng) for s in (-1.0, 1.0)], axis=1)
    fwd_lane = (lax.broadcasted_iota(jnp.int32, (bb, width), 1) & 127) < S5_STATE

    def partner(s):
        tiles = [s[:, i * 128:(i + 1) * 128] for i in range(2 * ng)]
        return jnp.concatenate([tiles[i ^ 1] for i in range(2 * ng)], axis=1)

    def advance(s, vf, vb):
        return a1 * s + a2 * partner(s) + jnp.where(fwd_lane, vf, vb)

    def scan(base, n, s0, store):
        def body(t, s):
            rf = pl.multiple_of((base + 2 * t) * bb, 2 * bb)
            rb = pl.multiple_of((base + n - 2 - 2 * t) * bb, 2 * bb)
            vf = v_ref[pl.ds(rf, 2 * bb), :]
            vb = v_ref[pl.ds(rb, 2 * bb), :]
            s1 = advance(s, vf[0:bb], vb[bb:2 * bb])
            s2 = advance(s1, vf[bb:2 * bb], vb[0:bb])
            if store:
                sk_ref[pl.ds(rf, 2 * bb), :] = jnp.concatenate([s, s1], axis=0)
                srb_ref[pl.ds(rb, 2 * bb), :] = jnp.concatenate([s1, s], axis=0)
            return s2
        return lax.fori_loop(0, n // 2, body, s0)

    s_ctx = scan(n_lat, n_ctx, jnp.zeros((bb, width), F32), False)
    scan(0, n_lat, s_ctx, True)
    lane = (lax.broadcasted_iota(jnp.int32, (rows_lat, width), 1) & 127) < S5_STATE
    st = jnp.where(lane, sk_ref[...], srb_ref[...]).astype(BF16)
    for g in range(ng):
        cols = slice(g * qc, (g + 1) * qc)
        y = _dot(xp_ref[0:rows_lat, cols], m_ref[g]) + _dot(st[:, cols], r_ref[g])
        y_ref[:, cols] = jax.nn.gelu(y).astype(BF16)
    yb = y_ref[...]
    for t in range(q):
        yp_ref[...] = _dot_nt(yb, pi_ref[t * 128:(t + 1) * 128, :])
        for b in range(bb):
            zt_ref[t, b] = yp_ref[pl.ds(b, n_lat, stride=bb), :].astype(BF16)


def _s5_mixer(ut, l, lt, mats):
    (kf, kb, dsk), p, r, a16 = mats
    q, b, _, width = ut.shape
    n_all, n_lat = lt // q, l // q
    ng = S5_LANE_GROUPS
    qc = q * S5_GROUP
    bb = min(S5_BATCH, b)
    nj = width // 128
    wspec = pl.BlockSpec((ng, qc, qc), lambda j, i: (j, 0, 0))
    tspec = pl.BlockSpec((ng, S5_GROUP, qc), lambda j, i: (j, 0, 0))
    return pl.pallas_call(
        functools.partial(_s5_kernel, n_lat=n_lat, n_ctx=n_all - n_lat, bb=bb),
        grid=(nj, b // bb),
        in_specs=[pl.BlockSpec((q, bb, n_all, 128), lambda j, i: (0, i, 0, j)),
                  tspec, tspec, pl.BlockSpec((ng, S5_GROUP, 1), lambda j, i: (j, 0, 0)), wspec, wspec,
                  pl.BlockSpec((ng, 2, 2 * S5_STATE), lambda j, i: (j, 0, 0))],
        out_specs=pl.BlockSpec((q, bb, n_lat, 128), lambda j, i: (0, i, 0, j)),
        out_shape=jax.ShapeDtypeStruct((q, b, n_lat, width), BF16),
        scratch_shapes=[pltpu.VMEM((ng * qc, ng * qc), BF16),
                        pltpu.VMEM((ng, qc, qc), BF16),
                        pltpu.VMEM((q, n_all * bb, 128), F32),
                        pltpu.VMEM((n_all * bb, ng * qc), BF16),
                        pltpu.VMEM((n_all * bb, ng * qc), F32),
                        pltpu.VMEM((n_lat * bb, ng * qc), F32),
                        pltpu.VMEM((n_lat * bb, ng * qc), F32),
                        pltpu.VMEM((n_lat * bb, ng * qc), BF16),
                        pltpu.VMEM((n_lat * bb, 128), F32)],
        compiler_params=_cparams(("arbitrary", "arbitrary")),
        name="s5_scan",
    )(ut, kf, kb, dsk, p, r, a16)


def _inv_unit_triangular_many(mats):
    c = mats[0].shape[0]
    eye = (lax.broadcasted_iota(jnp.int32, (c, c), 0) == lax.broadcasted_iota(jnp.int32, (c, c), 1)).astype(F32)
    prods = [eye - a for a in mats]
    pows = [a.astype(BF16) for a in mats]
    for _ in range(int(np.log2(c)) - 1):
        pows = [_dot(a, a).astype(BF16) for a in pows]
        prods = [p + _dot(p.astype(BF16), a) for p, a in zip(prods, pows)]
    return prods


def _gdn_prep_kernel(qkv_ref, cw_ref, ba_ref, bat_ref, prow_ref, pcol_ref,
                     u_ref, wq_ref, qk_ref, kdt_ref, dl_ref, *, n_lat_blocks, ctx_len):
    j = pl.program_id(1)
    rows = qkv_ref.shape[1]
    c = GDN_CHUNK
    nchunk = rows // c
    seg = jnp.where(j >= n_lat_blocks, ctx_len, GRID_W)
    x = qkv_ref[0].astype(F32)
    pos = lax.broadcasted_iota(jnp.int32, (rows, 1), 0) & (seg - 1)
    acc = x * cw_ref[2:3, :]
    for s in (-2, -1, 1, 2):
        shifted = pltpu.roll(x, (-s) % rows, 0)
        ok = jnp.logical_and(pos + s >= 0, pos + s < seg)
        acc = acc + jnp.where(ok, shifted, 0.0) * cw_ref[2 + s:3 + s, :]
    act = acc * _sigmoid(acc)

    ii = lax.broadcasted_iota(jnp.int32, (c, c), 0)
    jj = lax.broadcasted_iota(jnp.int32, (c, c), 1)
    incl = (ii >= jj, ii <= jj)
    strict = (ii > jj, ii < jj)
    tri_low = incl[0].astype(F32)
    tri_up = incl[1].astype(F32)
    eye_bf = (ii == jj).astype(BF16)
    ones = jnp.ones((c, c), F32)
    lane = lax.broadcasted_iota(jnp.int32, (c, 128), 1)
    prow = prow_ref[...]
    pcol = pcol_ref[...]
    hp = dict(precision=HIGHEST, preferred_element_type=F32)

    gates = []
    for ci in range(nchunk):
        ba = ba_ref[0, ci * c:(ci + 1) * c, :]
        g_all = -prow[0:1, :] * _softplus(ba + prow[1:2, :])
        g_all = jnp.where(jnp.logical_and(lane >= 8, lane < 16), g_all, 0.0)
        bat = bat_ref[0, ci]
        g_row = -pcol[:, 0:1] * _softplus(bat[8:16, :] + pcol[:, 1:2])
        gates.append(dict(beta=_sigmoid(ba),
                          gc=(jnp.dot(tri_low, g_all, **hp), jnp.dot(tri_up, g_all, **hp)),
                          gtot=jnp.dot(ones, g_all, **hp),
                          gr=(jnp.dot(g_row, tri_up, **hp), jnp.dot(g_row, tri_low, **hp))))

    qn, kn, vv = [], [], []
    for h in range(GDN_HEADS):
        qh = act[:, h * 128:(h + 1) * 128]
        kh = act[:, 512 + h * 128:512 + (h + 1) * 128]
        qn.append(qh * lax.rsqrt(jnp.sum(qh * qh, axis=-1, keepdims=True) + EPS) * (GDN_DK ** -0.5))
        kn.append(kh * lax.rsqrt(jnp.sum(kh * kh, axis=-1, keepdims=True) + EPS))
        vv.append(act[:, 1024 + h * 128:1024 + (h + 1) * 128])

    pairs = [(ci, h) for ci in range(nchunk) for h in range(GDN_HEADS)]
    sl = lambda t, ci: t[ci * c:(ci + 1) * c]
    kb = {p: sl(kn[p[1]], p[0]).astype(BF16) for p in pairs}
    kk = {p: _dot_nt(kb[p], kb[p]) for p in pairs}
    qkm = {p: _dot_nt(sl(qn[p[1]], p[0]).astype(BF16), kb[p]) for p in pairs}

    probs = [(ci, h, d) for ci in range(nchunk) for h in range(GDN_HEADS) for d in range(2)]
    beta, gc, gt, decay, a_mats = {}, {}, {}, {}, []
    for (ci, h, d) in probs:
        col = d * GDN_HEADS + h
        gi = gates[ci]
        beta[ci, h, d] = gi["beta"][:, col:col + 1]
        gc[ci, h, d] = gi["gc"][d][:, 8 + col:9 + col]
        gt[ci, h, d] = gi["gtot"][:, 8 + col:9 + col]
        grow = gi["gr"][d][col:col + 1, :]
        decay[ci, h, d] = jnp.exp(jnp.where(incl[d], gc[ci, h, d] - grow, -jnp.inf))
        a_mats.append(jnp.where(strict[d], beta[ci, h, d] * kk[ci, h] * decay[ci, h, d], 0.0))
    tinv = _inv_unit_triangular_many(a_mats)

    eg, sols, kdts = {}, {}, {}
    for n, (ci, h, d) in enumerate(probs):
        p = (ci, h, d)
        eg[p] = jnp.exp(gc[p])
        kh = sl(kn[h], ci)
        rhs = jnp.concatenate([sl(vv[h], ci) * beta[p], kh * (beta[p] * eg[p])], axis=-1).astype(BF16)
        sols[p] = _dot(tinv[n].astype(BF16), rhs)
        kdts[p] = _dot_tn((kh * jnp.exp(gt[p] - gc[p])).astype(BF16), eye_bf)
    for (ci, h, d) in probs:
        p = (ci, h, d)
        u_ref[0, d, h, ci] = sols[p][:, 0:128]
        wq_ref[0, d, h, ci, 0:c, :] = sols[p][:, 128:256].astype(BF16)
        wq_ref[0, d, h, ci, c:2 * c, :] = (sl(qn[h], ci) * eg[p]).astype(BF16)
        qk_ref[0, d, h, ci] = (qkm[ci, h] * decay[p]).astype(BF16)
        kdt_ref[0, d, h, ci] = kdts[p].astype(BF16)
        dl_ref[0, d, h, ci] = jnp.broadcast_to(jnp.exp(gt[p][0:8, :]), (8, 128))


def _gdn_prepare(qkv, conv_w, ba, bat_chunks, prow, pcol, l, lt):
    b, _, width = qkv.shape
    tm = ROW_TILE
    c = GDN_CHUNK
    cpb = tm // c
    nblk = lt // tm
    nch = lt // c
    hd = (b, 2, GDN_HEADS, nch)

    def blk(shape_tail, dtype):
        return (pl.BlockSpec((1, 2, GDN_HEADS, cpb) + shape_tail, lambda i, j: (i, 0, 0, j, 0, 0)),
                jax.ShapeDtypeStruct(hd + shape_tail, dtype))

    specs = [blk((c, 128), F32), blk((2 * c, 128), BF16), blk((c, c), BF16), blk((GDN_DK, c), BF16),
             blk((8, 128), F32)]
    return pl.pallas_call(
        functools.partial(_gdn_prep_kernel, n_lat_blocks=l // tm, ctx_len=lt - l),
        grid=(b, nblk),
        in_specs=[pl.BlockSpec((1, tm, width), lambda i, j: (i, j, 0)),
                  pl.BlockSpec((8, width), lambda i, j: (0, 0)),
                  pl.BlockSpec((1, tm, 128), lambda i, j: (i, j, 0)),
                  pl.BlockSpec((1, cpb, 16, c), lambda i, j: (i, j, 0, 0)),
                  pl.BlockSpec((2, 128), lambda i, j: (0, 0)),
                  pl.BlockSpec((8, 2), lambda i, j: (0, 0))],
        out_specs=[s for s, _ in specs],
        out_shape=[o for _, o in specs],
        compiler_params=_cparams(("parallel", "parallel")),
        name="gdn_prepare",
    )(qkv, conv_w, ba, bat_chunks, prow, pcol)


SCAN_GROUP = 2


def _gdn_scan_kernel(uf, wqf, qkf, kdtf, dlf, ub, wqb, qkb, kdtb, dlb, of_ref, ob_ref, s_ref):
    t = pl.program_id(0)
    nb = s_ref.shape[0]
    c = GDN_CHUNK

    @pl.when(t == 0)
    def _():
        s_ref[...] = jnp.zeros_like(s_ref)

    refs = ((uf, wqf, qkf, kdtf, dlf, of_ref), (ub, wqb, qkb, kdtb, dlb, ob_ref))
    for b0 in range(0, nb, SCAN_GROUP):
        chains = [(bi, d, h) for bi in range(b0, min(b0 + SCAN_GROUP, nb)) for d in range(2)
                  for h in range(GDN_HEADS)]
        s = {k: s_ref[k[0], k[1], k[2]] for k in chains}
        sb = {k: s[k].astype(BF16) for k in chains}
        r = {k: _dot(refs[k[1]][1][k[0], 0, k[2], 0], sb[k]) for k in chains}
        vb = {k: (refs[k[1]][0][k[0], 0, k[2], 0] - r[k][0:c]).astype(BF16) for k in chains}
        o = {k: r[k][c:2 * c] + _dot(refs[k[1]][2][k[0], 0, k[2], 0], vb[k]) for k in chains}
        sn = {k: s[k] * refs[k[1]][4][k[0], 0, k[2], 0][0:1, :] + _dot(refs[k[1]][3][k[0], 0, k[2], 0], vb[k])
              for k in chains}
        for k in chains:
            s_ref[k[0], k[1], k[2]] = sn[k]
            refs[k[1]][5][k[0], :, k[2] * 128:(k[2] + 1) * 128] = o[k]


def _gdn_scan(prep, l):
    u = prep[0]
    b, _, heads, nch, c, _ = u.shape
    n_lat = l // c
    n_ctx = nch - n_lat
    lt = nch * c

    def fwd_chunk(t):
        return jnp.where(t < n_ctx, n_lat + t, t - n_ctx)

    def bwd_chunk(t):
        return nch - 1 - t

    def spec(arr, chunk_of, d):
        tail = arr.shape[4:]
        return pl.BlockSpec((b, 1, heads, 1) + tail, lambda t: (0, d, 0, chunk_of(t), 0, 0))

    ins = [spec(a, fwd_chunk, 0) for a in prep] + [spec(a, bwd_chunk, 1) for a in prep]
    width = heads * 128
    return pl.pallas_call(
        _gdn_scan_kernel,
        grid=(nch,),
        in_specs=ins,
        out_specs=[pl.BlockSpec((b, c, width), lambda t: (0, fwd_chunk(t), 0)),
                   pl.BlockSpec((b, c, width), lambda t: (0, bwd_chunk(t), 0))],
        out_shape=[jax.ShapeDtypeStruct((b, lt, width), F32)] * 2,
        scratch_shapes=[pltpu.VMEM((b, 2, heads, GDN_DK, 128), F32)],
        compiler_params=_cparams(("arbitrary",)),
        name="gdn_scan",
    )(*prep, *prep)


def _merge_kernel(x_ref, z_ref, of_ref, ob_ref, gate_ref, br_ref, gt1_ref, sc2_ref, sh2_ref,
                  wglu_ref, bglu_ref, gnorm_ref, wa_ref, wb_ref, wo_ref, n2_ref, wr_ref, wrhi_ref, brt_ref,
                  x1_ref, h2_ref, lg_ref, zs_ref):
    nchunk = zs_ref.shape[1] // S5_Q
    for t in range(S5_Q):
        zt = z_ref[t, 0].astype(F32)
        for c in range(zs_ref.shape[0]):
            zs_ref[c, pl.ds(t, nchunk, stride=S5_Q), :] = zt[:, c * 128:(c + 1) * 128]
    zf = jnp.concatenate([zs_ref[c] for c in range(zs_ref.shape[0])], axis=1)
    z = zf.astype(BF16)
    ya = zf * _sigmoid(_dot(z, wglu_ref[...]) + bglu_ref[...])
    o = of_ref[0] + ob_ref[0]
    gate = gate_ref[0].astype(F32)
    parts = []
    for h in range(GDN_HEADS):
        oh = o[:, h * 128:(h + 1) * 128]
        parts.append(oh * lax.rsqrt(jnp.mean(oh * oh, axis=-1, keepdims=True) + EPS) * gnorm_ref[...])
    yb = jnp.concatenate(parts, axis=-1) * (gate * _sigmoid(gate))
    br = br_ref[0].astype(F32)
    d = x_ref.shape[2]
    ga = _sigmoid(br[:, 0:d])
    gb = _sigmoid(br[:, d:2 * d])
    m = ga * _dot(ya.astype(BF16), wa_ref[...]) + gb * _dot(yb.astype(BF16), wb_ref[...])
    mix = _dot(m.astype(BF16), wo_ref[...])
    x1 = x_ref[0] + gt1_ref[0] * mix
    x1_ref[0] = x1
    y2 = x1 * lax.rsqrt(jnp.mean(x1 * x1, axis=-1, keepdims=True) + EPS) * n2_ref[...]
    h2 = y2 * (1.0 + sc2_ref[0]) + sh2_ref[0]
    h2b = h2.astype(BF16)
    h2_ref[0, 0] = h2b
    h2_ref[1, 0] = h2b
    h2_lo = (h2 - h2b.astype(F32)).astype(BF16)
    rl = _dot(h2b, wr_ref[...])
    lg_ref[0] = rl[:, 0:128] + rl[:, 128:256] + _dot(h2_lo, wrhi_ref[...]) + brt_ref[...]


def _merge(x, z, o_f, o_b, gate, br, mod3, wglu, bglu, gnorm, wa, wb, wo, norm2, wr, wr_hi, brt):
    b, l, d = x.shape
    tm = min(MERGE_TILE, l)
    tok = lambda width: pl.BlockSpec((1, tm, width), lambda i, j: (i, j, 0))
    modspec = lambda k: pl.BlockSpec((1, 1, d), lambda i, j: (i * 6 + k, 0, 0))
    full = lambda arr: pl.BlockSpec(arr.shape, lambda i, j: (0,) * arr.ndim)
    consts = [wglu, bglu, gnorm, wa, wb, wo, norm2, wr, wr_hi, brt]
    return pl.pallas_call(
        _merge_kernel,
        grid=(b, l // tm),
        in_specs=[tok(d), pl.BlockSpec((S5_Q, 1, tm // S5_Q, 512), lambda i, j: (0, i, j, 0)),
                  tok(512), tok(512), tok(512), tok(2 * d),
                  modspec(2), modspec(4), modspec(3)] + [full(a) for a in consts],
        out_specs=[tok(d), pl.BlockSpec((2, 1, tm, d), lambda i, j: (0, i, j, 0)), tok(128)],
        out_shape=[jax.ShapeDtypeStruct((b, l, d), F32),
                   jax.ShapeDtypeStruct((2, b, l, d), BF16),
                   jax.ShapeDtypeStruct((b, l, 128), F32)],
        scratch_shapes=[pltpu.VMEM((4, tm, 128), F32)],
        compiler_params=_cparams(("parallel", "parallel")),
        name="branch_merge",
    )(x, z, o_f, o_b, gate, br, mod3, mod3, mod3, *consts)


def _moe_weight_copies(wgu_hbm, wd_hbm, wgu_buf, wd_buf, sem, expert, slot):
    return (pltpu.make_async_copy(wgu_hbm.at[expert], wgu_buf.at[slot], sem.at[0, slot]),
            pltpu.make_async_copy(wd_hbm.at[expert], wd_buf.at[slot], sem.at[1, slot]))


def _moe_kernel(te_ref, nu_ref, nxt_ref, slot_ref, x_ref, bgu_ref, bd_ref, wgu_hbm, wd_hbm, y_ref,
                wgu_buf, wd_buf, wgu_bf, wd_bf, sem):
    i = pl.program_id(0)
    used = i < nu_ref[0]
    copies = functools.partial(_moe_weight_copies, wgu_hbm, wd_hbm, wgu_buf, wd_buf, sem)

    @pl.when(jnp.logical_and(i == 0, used))
    def _():
        for cp in copies(te_ref[0], slot_ref[0]):
            cp.start()

    first = jnp.logical_or(i == 0, te_ref[i] != te_ref[jnp.maximum(i - 1, 0)])

    @pl.when(jnp.logical_and(first, used))
    def _():
        slot = slot_ref[i]
        for cp in copies(te_ref[i], slot):
            cp.wait()
        wgu_bf[...] = wgu_buf[slot].astype(BF16)
        wd_bf[...] = wd_buf[slot].astype(BF16)

        @pl.when(nxt_ref[i] >= 0)
        def _():
            for cp in copies(nxt_ref[i], 1 - slot):
                cp.start()

    @pl.when(used)
    def _():
        de = wd_bf.shape[0]
        gu = _dot(x_ref[...], wgu_bf[...]) + bgu_ref[0]
        gate = jnp.minimum(gu[:, 0:de], SWIGLU_LIMIT)
        up = jnp.clip(gu[:, de:2 * de], -SWIGLU_LIMIT, SWIGLU_LIMIT)
        act = (up + 1.0) * gate * _sigmoid(gate * SWIGLU_ALPHA)
        y_ref[...] = (_dot(act.astype(BF16), wd_bf[...]) + bd_ref[0]).astype(y_ref.dtype)

    @pl.when(jnp.logical_not(used))
    def _():
        y_ref[...] = jnp.zeros_like(y_ref)


def _moe_experts(xs, plan, w_gate_up, b_gate_up, w_down, b_down):
    tile_expert, n_used, next_expert, slot = plan
    p, d = xs.shape
    tm = MOE_TILE
    e, _, n2 = w_gate_up.shape
    de = w_down.shape[1]
    grid_spec = pltpu.PrefetchScalarGridSpec(
        num_scalar_prefetch=4,
        grid=(p // tm,),
        in_specs=[pl.BlockSpec((tm, d), lambda i, te, nu, nx, sl: (jnp.minimum(i, jnp.maximum(nu[0] - 1, 0)), 0)),
                  pl.BlockSpec((1, 1, n2), lambda i, te, nu, nx, sl: (te[i], 0, 0)),
                  pl.BlockSpec((1, 1, d), lambda i, te, nu, nx, sl: (te[i], 0, 0)),
                  pl.BlockSpec(memory_space=pl.ANY),
                  pl.BlockSpec(memory_space=pl.ANY)],
        out_specs=pl.BlockSpec((tm, d), lambda i, te, nu, nx, sl: (i, 0)),
        scratch_shapes=[pltpu.VMEM((2, d, n2), F32), pltpu.VMEM((2, de, d), F32),
                        pltpu.VMEM((d, n2), BF16), pltpu.VMEM((de, d), BF16),
                        pltpu.SemaphoreType.DMA((2, 2))],
    )
    return pl.pallas_call(
        _moe_kernel,
        grid_spec=grid_spec,
        out_shape=jax.ShapeDtypeStruct((p, d), BF16),
        compiler_params=_cparams(("arbitrary",)),
        name="moe_experts",
    )(tile_expert, n_used, next_expert, slot, xs, b_gate_up.reshape(e, 1, n2), b_down.reshape(e, 1, d),
      w_gate_up, w_down)


def _route(logits, n_tokens):
    tm = MOE_TILE
    top_val, top_idx = lax.top_k(logits, TOP_K)
    weights = jax.nn.softmax(top_val, axis=-1)
    flat_e = top_idx.reshape(-1).astype(jnp.int32)
    n_assign = flat_e.shape[0]
    iota = jnp.arange(n_assign, dtype=jnp.int32)
    sorted_e, order = lax.sort((flat_e, iota), num_keys=1, is_stable=True)
    counts = jnp.sum(jax.nn.one_hot(flat_e, N_EXPERTS, dtype=jnp.int32), axis=0)
    padded = ((counts + tm - 1) // tm) * tm
    pad_end = jnp.cumsum(padded)
    pad_start = pad_end - padded
    raw_start = jnp.cumsum(counts) - counts
    dest = pad_start[sorted_e] + (iota - raw_start[sorted_e])
    _, pos = lax.sort((order, dest), num_keys=1)
    pos = pos.reshape(n_tokens, TOP_K)
    n_rows = n_assign + N_EXPERTS * tm
    n_tiles = n_rows // tm
    n_used = (pad_end[-1] // tm).astype(jnp.int32)
    tile_start = jnp.arange(n_tiles, dtype=jnp.int32) * tm
    tile_expert = jnp.sum((tile_start[:, None] >= pad_end[None, :]).astype(jnp.int32), axis=1)
    last_e = jnp.sum((pad_end[-1] - 1 >= pad_end).astype(jnp.int32))
    tile_expert = jnp.minimum(tile_expert, last_e).astype(jnp.int32)
    lane = jnp.arange(tm, dtype=jnp.int32)[None, :]
    off = lane + (tile_start - pad_start[tile_expert])[:, None]
    valid = off < counts[tile_expert][:, None]
    sidx = jnp.clip(raw_start[tile_expert][:, None] + off, 0, n_assign - 1)
    filler = (tile_start[:, None] + lane) % n_tokens
    src_token = jnp.where(valid, (order // TOP_K)[sidx], filler).reshape(-1)
    nonempty = counts > 0
    idx = jnp.arange(N_EXPERTS, dtype=jnp.int32)
    later = jnp.where(jnp.logical_and(nonempty[None, :], idx[None, :] > idx[:, None]), idx[None, :], N_EXPERTS)
    next_nonempty = jnp.min(later, axis=1)
    next_nonempty = jnp.where(next_nonempty < N_EXPERTS, next_nonempty, -1).astype(jnp.int32)
    ordinal = jnp.cumsum(nonempty.astype(jnp.int32)) - 1
    plan = (tile_expert, n_used.reshape(1), next_nonempty[tile_expert], (ordinal[tile_expert] & 1).astype(jnp.int32))
    return weights, src_token, pos, plan


def _final_kernel(x1_ref, y0_ref, y1_ref, y2_ref, y3_ref, wt_ref, gt2_ref, nf_ref, o_ref):
    wt = wt_ref[0]
    moe = (wt[:, 0:1] * y0_ref[0, 0].astype(F32) + wt[:, 1:2] * y1_ref[0, 0].astype(F32)
           + wt[:, 2:3] * y2_ref[0, 0].astype(F32) + wt[:, 3:4] * y3_ref[0, 0].astype(F32))
    x2 = x1_ref[0] + gt2_ref[0] * moe
    o_ref[0] = x2 * lax.rsqrt(jnp.mean(x2 * x2, axis=-1, keepdims=True) + EPS) * nf_ref[...]


def _final(x1, yg, wt, mod3, norm_f):
    b, l, d = x1.shape
    tm = ROW_TILE
    tok = lambda width: pl.BlockSpec((1, tm, width), lambda i, j: (i, j, 0))
    ysel = lambda k: pl.BlockSpec((1, 1, tm, d), lambda i, j: (k, i, j, 0))
    return pl.pallas_call(
        _final_kernel,
        grid=(b, l // tm),
        in_specs=[tok(d)] + [ysel(k) for k in range(TOP_K)] + [tok(128), pl.BlockSpec((1, 1, d), lambda i, j: (i * 6 + 5, 0, 0)),
                                 pl.BlockSpec((1, d), lambda i, j: (0, 0))],
        out_specs=tok(d),
        out_shape=jax.ShapeDtypeStruct((b, l, d), F32),
        compiler_params=_cparams(("parallel", "parallel")),
        name="combine_final_norm",
    )(x1, *([yg] * TOP_K), wt, mod3, norm_f.reshape(1, d))


def _split_hi_lo(w):
    hi = w.astype(BF16)
    lo = (w - hi.astype(F32)).astype(BF16)
    return hi, lo


def kernel(x, c, ctx, c_ctx, w_mod, b_mod, norm1, w_in, s5_lam_re, s5_lam_im, s5_log_step, s5_b_re, s5_b_im, s5_c_re, s5_c_im, s5_d, s5_w_glu, s5_b_glu, gdn_conv, gdn_a_log, gdn_dt_bias, gdn_norm, w_branch_a, w_branch_b, w_out, norm2, w_router, b_router, w_gate_up, b_gate_up, w_down, b_down, norm_f):
    b, l, d = x.shape
    lc = ctx.shape[1]
    depth = w_mod.shape[0]
    assert depth == 1, "single-layer block: the context stream has no consumer after the token mixer"
    assert l % ROW_TILE == 0 and lc == ROW_TILE and lc & (lc - 1) == 0 and b <= 8
    ly = 0

    cc = jnp.zeros((16, d), F32).at[0:b].set(c).at[b].set(c_ctx)
    mod = _modulation(cc, w_mod[ly], b_mod[ly])
    mod3 = mod[0:b + 1].reshape((b + 1) * 6, 1, d)

    wi = w_in[ly]
    o_u, o_qkv, o_gate, o_ba, o_br = 0, 512, 2048, 2560, 2576
    w_ba = wi[:, o_ba:o_br]
    ba_hi, ba_lo = _split_hi_lo(w_ba)
    pad = lambda t: jnp.pad(t, ((0, 0), (0, 128 - t.shape[1])))
    w_main = jnp.concatenate([wi[:, o_u:o_ba].astype(BF16), wi[:, o_br:].astype(BF16), pad(ba_hi), pad(ba_lo)], axis=1)
    wt = jnp.concatenate([ba_hi.T, ba_lo.T], axis=0)
    ut, qkv, gate, br, ba, bat = _in_projection(x, ctx, mod3, norm1[ly], w_main, pad(ba_hi), wt, ba_hi.T)

    mats = _s5_matrices(s5_lam_re[ly], s5_lam_im[ly], s5_log_step[ly], s5_b_re[ly], s5_b_im[ly],
                        s5_c_re[ly], s5_c_im[ly], s5_d[ly])
    lt = l + lc
    z = _s5_mixer(ut, l, lt, mats)

    nch = lt // GDN_CHUNK
    bat_chunks = jnp.transpose(bat[:, :, 0:lt].reshape(b, 16, nch, GDN_CHUNK), (0, 2, 1, 3))
    ea = jnp.exp(gdn_a_log[ly].astype(F32)).reshape(-1)
    dtb = gdn_dt_bias[ly].astype(F32).reshape(-1)
    prow = jnp.zeros((2, 128), F32).at[0, 8:16].set(ea).at[1, 8:16].set(dtb)
    pcol = jnp.stack([ea, dtb], axis=1)
    conv_w = jnp.zeros((8, qkv.shape[2]), F32).at[0:CONV_K].set(gdn_conv[ly].astype(F32))
    prep = _gdn_prepare(qkv, conv_w, ba, bat_chunks, prow, pcol, l, lt)
    o_f, o_b = _gdn_scan(prep, l)

    wr_hi, wr_lo = _split_hi_lo(jnp.pad(w_router[ly].astype(F32), ((0, 0), (0, 128 - N_EXPERTS))))
    wr = jnp.concatenate([wr_hi, wr_lo], axis=1)
    brt = jnp.pad(b_router[ly].astype(F32), (0, 128 - N_EXPERTS)).reshape(1, 128)
    x1, h2, logits = _merge(x, z, o_f, o_b, gate, br, mod3,
                            s5_w_glu[ly].astype(BF16), s5_b_glu[ly].astype(F32).reshape(1, -1),
                            gdn_norm[ly].astype(F32).reshape(1, -1),
                            w_branch_a[ly].astype(BF16), w_branch_b[ly].astype(BF16), w_out[ly].astype(BF16),
                            norm2[ly].astype(F32).reshape(1, d), wr, wr_hi, brt)

    n_tok = b * l
    weights, src_token, pos, plan = _route(logits.reshape(n_tok, 128)[:, 0:N_EXPERTS], n_tok)
    h2f = h2.reshape(2 * n_tok, d)
    xs = h2f[src_token]
    ys = _moe_experts(xs, plan, w_gate_up[ly], b_gate_up[ly], w_down[ly], b_down[ly])
    yg = ys[pos.T.reshape(-1)].reshape(TOP_K, b, l, d)
    wt4 = jnp.pad(weights, ((0, 0), (0, 128 - TOP_K))).reshape(b, l, 128)
    return _final(x1, yg, wt4, mod3, norm_f)
```

```python
import functools

import numpy as np
import jax
import jax.numpy as jnp
from jax import lax
from jax.experimental import pallas as pl
from jax.experimental.pallas import tpu as pltpu

F32 = jnp.float32
BF16 = jnp.bfloat16
HIGHEST = lax.Precision.HIGHEST

EPS = 1e-6
GRID_W = 64

S5_GROUP = 16
S5_STATE = 64
S5_Q = 16

GDN_HEADS = 4
GDN_DK = 128
GDN_CHUNK = 64
CONV_K = 5

N_EXPERTS = 32
TOP_K = 4
SWIGLU_LIMIT = 7.0
SWIGLU_ALPHA = 1.702

ROW_TILE = 256
MERGE_TILE = 512
IN_TILE = 256
MOE_TILE = 512
VMEM_LIMIT = 56 * 1024 * 1024


def _cparams(sem):
    return pltpu.CompilerParams(dimension_semantics=sem, vmem_limit_bytes=VMEM_LIMIT)


def _sigmoid(v):
    return 1.0 / (1.0 + jnp.exp(-v))


def _softplus(v):
    return jnp.maximum(v, 0.0) + jnp.log(1.0 + jnp.exp(-jnp.abs(v)))


def _dot(a, b):
    return jnp.dot(a, b, preferred_element_type=F32)


def _dot_nt(a, b):
    return lax.dot_general(a, b, (((1,), (1,)), ((), ())), preferred_element_type=F32)


def _dot_tn(a, b):
    return lax.dot_general(a, b, (((0,), (0,)), ((), ())), preferred_element_type=F32)


def _mod_kernel(c_ref, w_ref, b_ref, o_ref):
    cc = c_ref[...]
    s = cc * _sigmoid(cc)
    o_ref[...] = jnp.dot(s, w_ref[...], precision=HIGHEST, preferred_element_type=F32) + b_ref[...]


def _modulation(cc, w_mod, b_mod):
    rows, d = cc.shape
    n = w_mod.shape[1]
    bn = d
    return pl.pallas_call(
        _mod_kernel,
        grid=(n // bn,),
        in_specs=[pl.BlockSpec((rows, d), lambda j: (0, 0)),
                  pl.BlockSpec((d, bn), lambda j: (0, j)),
                  pl.BlockSpec((1, bn), lambda j: (0, j))],
        out_specs=pl.BlockSpec((rows, bn), lambda j: (0, j)),
        out_shape=jax.ShapeDtypeStruct((rows, n), F32),
        compiler_params=_cparams(("arbitrary",)),
        name="adaln_mod",
    )(cc, w_mod, b_mod.reshape(1, n))


def _inproj_kernel(x_ref, c_ref, sc_ref, sh_ref, g_ref, w_ref, wlo_ref, wt_ref, wtlo_ref,
                   ut_ref, qkv_ref, gate_ref, br_ref, ba_ref, bat_ref, us_ref, *, n_lat_blocks):
    j = pl.program_id(1)
    is_ctx = j >= n_lat_blocks
    x = x_ref[0]
    lc = c_ref.shape[1]
    x = jnp.where(is_ctx, jnp.concatenate([c_ref[0], x[lc:]], axis=0) if lc < x.shape[0] else c_ref[0], x)
    ms = jnp.mean(x * x, axis=-1, keepdims=True)
    y = x * lax.rsqrt(ms + EPS) * g_ref[...]
    h = y * (1.0 + sc_ref[0]) + sh_ref[0]
    h_hi = h.astype(BF16)
    h_lo = (h - h_hi.astype(F32)).astype(BF16)
    z = _dot(h_hi, w_ref[...])
    nchunk = us_ref.shape[1] // S5_Q
    for c in range(us_ref.shape[0]):
        us_ref[c] = z[:, c * 128:(c + 1) * 128]
        for t in range(S5_Q):
            ut_ref[t, 0, :, c * 128:(c + 1) * 128] = us_ref[c, pl.ds(t, nchunk, stride=S5_Q), :].astype(BF16)
    qkv_ref[0] = z[:, 512:2048].astype(BF16)
    gate_ref[0] = z[:, 2048:2560].astype(BF16)
    br_ref[0] = z[:, 2560:4608].astype(BF16)
    ba_ref[0] = z[:, 4608:4736] + z[:, 4736:4864] + _dot(h_lo, wlo_ref[...])
    rt = _dot_nt(wt_ref[...], h_hi)
    bat_ref[0] = rt[0:16] + rt[16:32] + _dot_nt(wtlo_ref[...], h_lo)


def _in_projection(x, ctx, mod3, norm1, w_main, w_lo, wt, wt_lo):
    b, l, d = x.shape
    lc = ctx.shape[1]
    tm = min(IN_TILE, l)
    assert l % tm == 0 and lc <= tm
    nl = l // tm
    nb = mod3.shape[0] // 6 - 1
    lt = (nl + 1) * tm
    ncols = w_main.shape[1]

    def x_map(i, j):
        return (i, jnp.minimum(j, nl - 1), 0)

    def mod_map(k):
        return lambda i, j: (jnp.where(j >= nl, nb, i) * 6 + k, 0, 0)

    def tok(width):
        return pl.BlockSpec((1, tm, width), lambda i, j: (i, j, 0))

    const2 = lambda i, j: (0, 0)
    outs = pl.pallas_call(
        functools.partial(_inproj_kernel, n_lat_blocks=nl),
        grid=(b, nl + 1),
        in_specs=[pl.BlockSpec((1, tm, d), x_map),
                  pl.BlockSpec((1, lc, d), lambda i, j: (i, 0, 0)),
                  pl.BlockSpec((1, 1, d), mod_map(1)),
                  pl.BlockSpec((1, 1, d), mod_map(0)),
                  pl.BlockSpec((1, d), const2),
                  pl.BlockSpec((d, ncols), const2),
                  pl.BlockSpec((d, 128), const2),
                  pl.BlockSpec((32, d), const2),
                  pl.BlockSpec((16, d), const2)],
        out_specs=[pl.BlockSpec((S5_Q, 1, tm // S5_Q, 512), lambda i, j: (0, i, j, 0)),
                   tok(1536), tok(512), tok(2048), tok(128),
                   pl.BlockSpec((1, 16, tm), lambda i, j: (i, 0, j))],
        out_shape=[jax.ShapeDtypeStruct((S5_Q, b, lt // S5_Q, 512), BF16),
                   jax.ShapeDtypeStruct((b, lt, 1536), BF16),
                   jax.ShapeDtypeStruct((b, lt, 512), BF16),
                   jax.ShapeDtypeStruct((b, lt, 2048), BF16),
                   jax.ShapeDtypeStruct((b, lt, 128), F32),
                   jax.ShapeDtypeStruct((b, 16, lt), F32)],
        scratch_shapes=[pltpu.VMEM((4, tm, 128), F32)],
        compiler_params=_cparams(("parallel", "arbitrary")),
        name="in_projection",
    )(x, ctx, mod3, mod3, norm1.reshape(1, d), w_main, w_lo, wt, wt_lo)
    return outs


def _s5_matrices(lam_re, lam_im, log_step, b_re, b_im, c_re, c_im, d_skip):
    q, c16 = S5_Q, S5_GROUP
    qc = q * c16
    lr = jnp.minimum(lam_re.astype(F32), -1e-4)
    li = lam_im.astype(F32)
    dt = jnp.exp(log_step.astype(F32))[..., None]
    g, n = lr.shape[1], lr.shape[2]
    pw = jnp.arange(q + 1, dtype=F32)
    mag = jnp.exp(pw * (lr * dt)[..., None])
    ang = pw * (li * dt)[..., None]
    pr, pi = mag * jnp.cos(ang), mag * jnp.sin(ang)
    ar, ai = pr[..., 1], pi[..., 1]
    den = lr * lr + li * li
    fr = ((ar - 1.0) * lr + ai * li) / den
    fi = (ai * lr - (ar - 1.0) * li) / den
    br, bi = b_re.astype(F32), b_im.astype(F32)
    bbr = fr[..., None] * br - fi[..., None] * bi
    bbi = fr[..., None] * bi + fi[..., None] * br
    crt = jnp.swapaxes(c_re.astype(F32), -1, -2)
    cit = jnp.swapaxes(c_im.astype(F32), -1, -2)
    car = (crt[:, :, :, None, :] * pr[..., None] - cit[:, :, :, None, :] * pi[..., None]).reshape(2, g, n, (q + 1) * c16)
    cai = (crt[:, :, :, None, :] * pi[..., None] + cit[:, :, :, None, :] * pr[..., None]).reshape(2, g, n, (q + 1) * c16)
    kern = jnp.sum(bbr[..., None] * car[:, :, :, None, :] - bbi[..., None] * cai[:, :, :, None, :], axis=2)
    kf = kern[0, :, :, 0:qc]
    kb = kern[1].reshape(g, c16, q + 1, c16)[:, :, q - 1::-1].reshape(g, c16, qc)
    taps = (kf, kb, d_skip.astype(F32).reshape(g, c16, 1))

    bbr_t, bbi_t = jnp.swapaxes(bbr, -1, -2), jnp.swapaxes(bbi, -1, -2)
    prt, pit = jnp.swapaxes(pr, -1, -2), jnp.swapaxes(pi, -1, -2)

    def p_block(d, reverse):
        ppr = prt[d, :, q - 1::-1] if reverse else prt[d, :, 0:q]
        ppi = pit[d, :, q - 1::-1] if reverse else pit[d, :, 0:q]
        re = ppr[:, :, None, :] * bbr_t[d][:, None] - ppi[:, :, None, :] * bbi_t[d][:, None]
        im = ppr[:, :, None, :] * bbi_t[d][:, None] + ppi[:, :, None, :] * bbr_t[d][:, None]
        return re.reshape(g, qc, n), im.reshape(g, qc, n)

    pf_re, pf_im = p_block(0, True)
    pb_re, pb_im = p_block(1, False)
    p = jnp.concatenate([pf_re, pb_re, pf_im, pb_im], axis=-1)

    def r_block(d, reverse):
        if reverse:
            sel = lambda t: t[d].reshape(g, n, q + 1, c16)[:, :, q:0:-1].reshape(g, n, qc)
        else:
            sel = lambda t: t[d][:, :, c16:(q + 1) * c16]
        return sel(car), -sel(cai)

    rf_re, rf_im = r_block(0, False)
    rb_re, rb_im = r_block(1, True)
    r = jnp.concatenate([rf_re, rb_re, rf_im, rb_im], axis=1)
    a16 = jnp.stack([jnp.concatenate([pr[0, :, :, q], pr[1, :, :, q]], axis=-1),
                     jnp.concatenate([pi[0, :, :, q], pi[1, :, :, q]], axis=-1)], axis=1)
    return taps, p.astype(BF16), r.astype(BF16), a16


S5_LANE_GROUPS = 8
S5_BATCH = 4


def _s5_kernel(ut_ref, kf_ref, kb_ref, d_ref, p_ref, r_ref, a_ref, zt_ref,
               pi_ref, m_ref, x_ref, xp_ref, v_ref, sk_ref, srb_ref, y_ref, yp_ref, *, n_lat, n_ctx, bb):
    q, ng = S5_Q, S5_LANE_GROUPS
    qc = q * S5_GROUP
    width = ng * qc
    n_all = n_lat + n_ctx
    rows_lat = n_lat * bb

    @pl.when(jnp.logical_and(pl.program_id(0) == 0, pl.program_id(1) == 0))
    def _():
        src = lax.broadcasted_iota(jnp.int32, (width, qc), 0)
        dst = lax.broadcasted_iota(jnp.int32, (width, qc), 1)
        dest_in_group = ((src >> 7) << 4) + (src & 15)
        grp = (src >> 4) & (ng - 1)
        for g in range(ng):
            hit = jnp.logical_and(grp == g, dest_in_group == dst)
            pi_ref[:, g * qc:(g + 1) * qc] = jnp.where(hit, 1.0, 0.0).astype(BF16)

    @pl.when(pl.program_id(1) == 0)
    def _():
        lane = lax.broadcasted_iota(jnp.int32, (S5_GROUP, qc), 1)
        chan = lax.broadcasted_iota(jnp.int32, (S5_GROUP, qc), 0)
        for g in range(ng):
            kf, kb, dg = kf_ref[g], kb_ref[g], d_ref[g]
            for j in range(q):
                lo, hi = j * S5_GROUP, qc - (q - 1 - j) * S5_GROUP
                mf = jnp.where(lane >= lo, pltpu.roll(kf, lo, 1) if lo else kf, 0.0)
                mb = jnp.where(lane < hi, pltpu.roll(kb, hi % qc, 1) if hi % qc else kb, 0.0)
                skip = jnp.where(lane == lo + chan, dg, 0.0)
                m_ref[g, j * S5_GROUP:(j + 1) * S5_GROUP, :] = (mf + mb + skip).astype(BF16)

    for b in range(bb):
        for t in range(q):
            x_ref[t, pl.ds(b, n_all, stride=bb), :] = ut_ref[t, b].astype(F32)
    xb = jnp.concatenate([x_ref[t] for t in range(q)], axis=1).astype(BF16)
    for g in range(ng):
        cols = slice(g * qc, (g + 1) * qc)
        xg = _dot(xb, pi_ref[:, cols]).astype(BF16)
        xp_ref[:, cols] = xg
        v_ref[:, cols] = _dot(xg, p_ref[g])

    a = a_ref[...]
    a1 = jnp.concatenate([a[g, 0:1] for g in range(ng) for _ in range(2)], axis=1)
    a2 = jnp.concatenate([s * a[g, 1:2] for g in range(ng) for s in (-1.0, 1.0)], axis=1)
    fwd_lane = (lax.broadcasted_iota(jnp.int32, (bb, width), 1) & 127) < S5_STATE

    def partner(s):
        tiles = [s[:, i * 128:(i + 1) * 128] for i in range(2 * ng)]
        return jnp.concatenate([tiles[i ^ 1] for i in range(2 * ng)], axis=1)

    def advance(s, vf, vb):
        return a1 * s + a2 * partner(s) + jnp.where(fwd_lane, vf, vb)

    def scan(base, n, s0, store):
        def body(t, s):
            rf = pl.multiple_of((base + 2 * t) * bb, 2 * bb)
            rb = pl.multiple_of((base + n - 2 - 2 * t) * bb, 2 * bb)
            vf = v_ref[pl.ds(rf, 2 * bb), :]
            vb = v_ref[pl.ds(rb, 2 * bb), :]
            s1 = advance(s, vf[0:bb], vb[bb:2 * bb])
            s2 = advance(s1, vf[bb:2 * bb], vb[0:bb])
            if store:
                sk_ref[pl.ds(rf, 2 * bb), :] = jnp.concatenate([s, s1], axis=0)
                srb_ref[pl.ds(rb, 2 * bb), :] = jnp.concatenate([s1, s], axis=0)
            return s2
        return lax.fori_loop(0, n // 2, body, s0)

    s_ctx = scan(n_lat, n_ctx, jnp.zeros((bb, width), F32), False)
    scan(0, n_lat, s_ctx, True)
    lane = (lax.broadcasted_iota(jnp.int32, (rows_lat, width), 1) & 127) < S5_STATE
    st = jnp.where(lane, sk_ref[...], srb_ref[...]).astype(BF16)
    for g in range(ng):
        cols = slice(g * qc, (g + 1) * qc)
        y = _dot(xp_ref[0:rows_lat, cols], m_ref[g]) + _dot(st[:, cols], r_ref[g])
        y_ref[:, cols] = jax.nn.gelu(y).astype(BF16)
    yb = y_ref[...]
    for t in range(q):
        yp_ref[...] = _dot_nt(yb, pi_ref[t * 128:(t + 1) * 128, :])
        for b in range(bb):
            zt_ref[t, b] = yp_ref[pl.ds(b, n_lat, stride=bb), :].astype(BF16)


def _s5_mixer(ut, l, lt, mats):
    (kf, kb, dsk), p, r, a16 = mats
    q, b, _, width = ut.shape
    n_all, n_lat = lt // q, l // q
    ng = S5_LANE_GROUPS
    qc = q * S5_GROUP
    bb = min(S5_BATCH, b)
    nj = width // 128
    wspec = pl.BlockSpec((ng, qc, qc), lambda j, i: (j, 0, 0))
    tspec = pl.BlockSpec((ng, S5_GROUP, qc), lambda j, i: (j, 0, 0))
    return pl.pallas_call(
        functools.partial(_s5_kernel, n_lat=n_lat, n_ctx=n_all - n_lat, bb=bb),
        grid=(nj, b // bb),
        in_specs=[pl.BlockSpec((q, bb, n_all, 128), lambda j, i: (0, i, 0, j)),
                  tspec, tspec, pl.BlockSpec((ng, S5_GROUP, 1), lambda j, i: (j, 0, 0)), wspec, wspec,
                  pl.BlockSpec((ng, 2, 2 * S5_STATE), lambda j, i: (j, 0, 0))],
        out_specs=pl.BlockSpec((q, bb, n_lat, 128), lambda j, i: (0, i, 0, j)),
        out_shape=jax.ShapeDtypeStruct((q, b, n_lat, width), BF16),
        scratch_shapes=[pltpu.VMEM((ng * qc, ng * qc), BF16),
                        pltpu.VMEM((ng, qc, qc), BF16),
                        pltpu.VMEM((q, n_all * bb, 128), F32),
                        pltpu.VMEM((n_all * bb, ng * qc), BF16),
                        pltpu.VMEM((n_all * bb, ng * qc), F32),
                        pltpu.VMEM((n_lat * bb, ng * qc), F32),
                        pltpu.VMEM((n_lat * bb, ng * qc), F32),
                        pltpu.VMEM((n_lat * bb, ng * qc), BF16),
                        pltpu.VMEM((n_lat * bb, 128), F32)],
        compiler_params=_cparams(("arbitrary", "arbitrary")),
        name="s5_scan",
    )(ut, kf, kb, dsk, p, r, a16)


def _inv_unit_triangular_many(mats):
    c = mats[0].shape[0]
    eye = (lax.broadcasted_iota(jnp.int32, (c, c), 0) == lax.broadcasted_iota(jnp.int32, (c, c), 1)).astype(F32)
    prods = [eye - a for a in mats]
    pows = [a.astype(BF16) for a in mats]
    for _ in range(int(np.log2(c)) - 1):
        pows = [_dot(a, a).astype(BF16) for a in pows]
        prods = [p + _dot(p.astype(BF16), a) for p, a in zip(prods, pows)]
    return prods


def _gdn_prep_kernel(qkv_ref, cw_ref, ba_ref, bat_ref, prow_ref, pcol_ref,
                     u_ref, wq_ref, qk_ref, kdt_ref, dl_ref, *, n_lat_blocks, ctx_len):
    j = pl.program_id(1)
    rows = qkv_ref.shape[1]
    c = GDN_CHUNK
    nchunk = rows // c
    seg = jnp.where(j >= n_lat_blocks, ctx_len, GRID_W)
    x = qkv_ref[0]
    ri = lax.broadcasted_iota(jnp.int32, (rows, rows), 0)
    ci = lax.broadcasted_iota(jnp.int32, (rows, rows), 1)
    same_seg = (ci & -seg) == (ri & -seg)
    acc = x.astype(F32) * cw_ref[2:3, :]
    for s in (-2, -1, 1, 2):
        sel = jnp.where(jnp.logical_and(ci == ri + s, same_seg), 1.0, 0.0).astype(BF16)
        acc = acc + _dot(sel, x) * cw_ref[2 + s:3 + s, :]
    act = acc * _sigmoid(acc)

    ii = lax.broadcasted_iota(jnp.int32, (c, c), 0)
    jj = lax.broadcasted_iota(jnp.int32, (c, c), 1)
    incl = (ii >= jj, ii <= jj)
    strict = (ii > jj, ii < jj)
    tri_low = incl[0].astype(F32)
    tri_up = incl[1].astype(F32)
    eye_bf = (ii == jj).astype(BF16)
    ones = jnp.ones((c, c), F32)
    lane = lax.broadcasted_iota(jnp.int32, (c, 128), 1)
    prow = prow_ref[...]
    pcol = pcol_ref[...]
    hp = dict(precision=HIGHEST, preferred_element_type=F32)

    gates = []
    for ci in range(nchunk):
        ba = ba_ref[0, ci * c:(ci + 1) * c, :]
        g_all = -prow[0:1, :] * _softplus(ba + prow[1:2, :])
        g_all = jnp.where(jnp.logical_and(lane >= 8, lane < 16), g_all, 0.0)
        bat = bat_ref[0, ci]
        g_row = -pcol[:, 0:1] * _softplus(bat[8:16, :] + pcol[:, 1:2])
        gates.append(dict(beta=_sigmoid(ba),
                          gc=(jnp.dot(tri_low, g_all, **hp), jnp.dot(tri_up, g_all, **hp)),
                          gtot=jnp.dot(ones, g_all, **hp),
                          gr=(jnp.dot(g_row, tri_up, **hp), jnp.dot(g_row, tri_low, **hp))))

    qn, kn, vv = [], [], []
    for h in range(GDN_HEADS):
        qh = act[:, h * 128:(h + 1) * 128]
        kh = act[:, 512 + h * 128:512 + (h + 1) * 128]
        qn.append(qh * lax.rsqrt(jnp.sum(qh * qh, axis=-1, keepdims=True) + EPS) * (GDN_DK ** -0.5))
        kn.append(kh * lax.rsqrt(jnp.sum(kh * kh, axis=-1, keepdims=True) + EPS))
        vv.append(act[:, 1024 + h * 128:1024 + (h + 1) * 128])

    pairs = [(ci, h) for ci in range(nchunk) for h in range(GDN_HEADS)]
    sl = lambda t, ci: t[ci * c:(ci + 1) * c]
    kb = {p: sl(kn[p[1]], p[0]).astype(BF16) for p in pairs}
    kk = {p: _dot_nt(kb[p], kb[p]) for p in pairs}
    qkm = {p: _dot_nt(sl(qn[p[1]], p[0]).astype(BF16), kb[p]) for p in pairs}

    probs = [(ci, h, d) for ci in range(nchunk) for h in range(GDN_HEADS) for d in range(2)]
    beta, gc, gt, decay, a_mats = {}, {}, {}, {}, []
    for (ci, h, d) in probs:
        col = d * GDN_HEADS + h
        gi = gates[ci]
        beta[ci, h, d] = gi["beta"][:, col:col + 1]
        gc[ci, h, d] = gi["gc"][d][:, 8 + col:9 + col]
        gt[ci, h, d] = gi["gtot"][:, 8 + col:9 + col]
        grow = gi["gr"][d][col:col + 1, :]
        decay[ci, h, d] = jnp.exp(jnp.where(incl[d], gc[ci, h, d] - grow, -jnp.inf))
        a_mats.append(jnp.where(strict[d], beta[ci, h, d] * kk[ci, h] * decay[ci, h, d], 0.0))
    tinv = _inv_unit_triangular_many(a_mats)

    eg, sols, kdts = {}, {}, {}
    for n, (ci, h, d) in enumerate(probs):
        p = (ci, h, d)
        eg[p] = jnp.exp(gc[p])
        kh = sl(kn[h], ci)
        rhs = jnp.concatenate([sl(vv[h], ci) * beta[p], kh * (beta[p] * eg[p])], axis=-1).astype(BF16)
        sols[p] = _dot(tinv[n].astype(BF16), rhs)
        kdts[p] = _dot_tn((kh * jnp.exp(gt[p] - gc[p])).astype(BF16), eye_bf)
    for (ci, h, d) in probs:
        p = (ci, h, d)
        u_ref[0, d, h, ci] = sols[p][:, 0:128]
        wq_ref[0, d, h, ci, 0:c, :] = sols[p][:, 128:256].astype(BF16)
        wq_ref[0, d, h, ci, c:2 * c, :] = (sl(qn[h], ci) * eg[p]).astype(BF16)
        qk_ref[0, d, h, ci] = (qkm[ci, h] * decay[p]).astype(BF16)
        kdt_ref[0, d, h, ci] = kdts[p].astype(BF16)
        dl_ref[0, d, h, ci] = jnp.broadcast_to(jnp.exp(gt[p][0:8, :]), (8, 128))


def _gdn_prepare(qkv, conv_w, ba, bat_chunks, prow, pcol, l, lt):
    b, _, width = qkv.shape
    tm = ROW_TILE
    c = GDN_CHUNK
    cpb = tm // c
    nblk = lt // tm
    nch = lt // c
    hd = (b, 2, GDN_HEADS, nch)

    def blk(shape_tail, dtype):
        return (pl.BlockSpec((1, 2, GDN_HEADS, cpb) + shape_tail, lambda i, j: (i, 0, 0, j, 0, 0)),
                jax.ShapeDtypeStruct(hd + shape_tail, dtype))

    specs = [blk((c, 128), F32), blk((2 * c, 128), BF16), blk((c, c), BF16), blk((GDN_DK, c), BF16),
             blk((8, 128), F32)]
    return pl.pallas_call(
        functools.partial(_gdn_prep_kernel, n_lat_blocks=l // tm, ctx_len=lt - l),
        grid=(b, nblk),
        in_specs=[pl.BlockSpec((1, tm, width), lambda i, j: (i, j, 0)),
                  pl.BlockSpec((8, width), lambda i, j: (0, 0)),
                  pl.BlockSpec((1, tm, 128), lambda i, j: (i, j, 0)),
                  pl.BlockSpec((1, cpb, 16, c), lambda i, j: (i, j, 0, 0)),
                  pl.BlockSpec((2, 128), lambda i, j: (0, 0)),
                  pl.BlockSpec((8, 2), lambda i, j: (0, 0))],
        out_specs=[s for s, _ in specs],
        out_shape=[o for _, o in specs],
        compiler_params=_cparams(("parallel", "parallel")),
        name="gdn_prepare",
    )(qkv, conv_w, ba, bat_chunks, prow, pcol)


SCAN_GROUP = 2


def _gdn_scan_kernel(uf, wqf, qkf, kdtf, dlf, ub, wqb, qkb, kdtb, dlb, of_ref, ob_ref, s_ref):
    t = pl.program_id(0)
    nb = s_ref.shape[0]
    c = GDN_CHUNK

    @pl.when(t == 0)
    def _():
        s_ref[...] = jnp.zeros_like(s_ref)

    refs = ((uf, wqf, qkf, kdtf, dlf, of_ref), (ub, wqb, qkb, kdtb, dlb, ob_ref))
    for b0 in range(0, nb, SCAN_GROUP):
        chains = [(bi, d, h) for bi in range(b0, min(b0 + SCAN_GROUP, nb)) for d in range(2)
                  for h in range(GDN_HEADS)]
        s = {k: s_ref[k[0], k[1], k[2]] for k in chains}
        sb = {k: s[k].astype(BF16) for k in chains}
        r = {k: _dot(refs[k[1]][1][k[0], 0, k[2], 0], sb[k]) for k in chains}
        vb = {k: (refs[k[1]][0][k[0], 0, k[2], 0] - r[k][0:c]).astype(BF16) for k in chains}
        o = {k: r[k][c:2 * c] + _dot(refs[k[1]][2][k[0], 0, k[2], 0], vb[k]) for k in chains}
        sn = {k: s[k] * refs[k[1]][4][k[0], 0, k[2], 0][0:1, :] + _dot(refs[k[1]][3][k[0], 0, k[2], 0], vb[k])
              for k in chains}
        for k in chains:
            s_ref[k[0], k[1], k[2]] = sn[k]
            refs[k[1]][5][k[0], :, k[2] * 128:(k[2] + 1) * 128] = o[k]


def _gdn_scan(prep, l):
    u = prep[0]
    b, _, heads, nch, c, _ = u.shape
    n_lat = l // c
    n_ctx = nch - n_lat
    lt = nch * c

    def fwd_chunk(t):
        return jnp.where(t < n_ctx, n_lat + t, t - n_ctx)

    def bwd_chunk(t):
        return nch - 1 - t

    def spec(arr, chunk_of, d):
        tail = arr.shape[4:]
        return pl.BlockSpec((b, 1, heads, 1) + tail, lambda t: (0, d, 0, chunk_of(t), 0, 0))

    ins = [spec(a, fwd_chunk, 0) for a in prep] + [spec(a, bwd_chunk, 1) for a in prep]
    width = heads * 128
    return pl.pallas_call(
        _gdn_scan_kernel,
        grid=(nch,),
        in_specs=ins,
        out_specs=[pl.BlockSpec((b, c, width), lambda t: (0, fwd_chunk(t), 0)),
                   pl.BlockSpec((b, c, width), lambda t: (0, bwd_chunk(t), 0))],
        out_shape=[jax.ShapeDtypeStruct((b, lt, width), F32)] * 2,
        scratch_shapes=[pltpu.VMEM((b, 2, heads, GDN_DK, 128), F32)],
        compiler_params=_cparams(("arbitrary",)),
        name="gdn_scan",
    )(*prep, *prep)


def _merge_kernel(x_ref, z_ref, of_ref, ob_ref, gate_ref, br_ref, gt1_ref, sc2_ref, sh2_ref,
                  wglu_ref, bglu_ref, gnorm_ref, wa_ref, wb_ref, wo_ref, n2_ref, wr_ref, wrhi_ref, brt_ref,
                  x1_ref, h2_ref, lg_ref, zs_ref):
    nchunk = zs_ref.shape[1] // S5_Q
    for t in range(S5_Q):
        zt = z_ref[t, 0].astype(F32)
        for c in range(zs_ref.shape[0]):
            zs_ref[c, pl.ds(t, nchunk, stride=S5_Q), :] = zt[:, c * 128:(c + 1) * 128]
    zf = jnp.concatenate([zs_ref[c] for c in range(zs_ref.shape[0])], axis=1)
    z = zf.astype(BF16)
    ya = zf * _sigmoid(_dot(z, wglu_ref[...]) + bglu_ref[...])
    o = of_ref[0] + ob_ref[0]
    gate = gate_ref[0].astype(F32)
    parts = []
    for h in range(GDN_HEADS):
        oh = o[:, h * 128:(h + 1) * 128]
        parts.append(oh * lax.rsqrt(jnp.mean(oh * oh, axis=-1, keepdims=True) + EPS) * gnorm_ref[...])
    yb = jnp.concatenate(parts, axis=-1) * (gate * _sigmoid(gate))
    br = br_ref[0].astype(F32)
    d = x_ref.shape[2]
    ga = _sigmoid(br[:, 0:d])
    gb = _sigmoid(br[:, d:2 * d])
    m = ga * _dot(ya.astype(BF16), wa_ref[...]) + gb * _dot(yb.astype(BF16), wb_ref[...])
    mix = _dot(m.astype(BF16), wo_ref[...])
    x1 = x_ref[0] + gt1_ref[0] * mix
    x1_ref[0] = x1
    y2 = x1 * lax.rsqrt(jnp.mean(x1 * x1, axis=-1, keepdims=True) + EPS) * n2_ref[...]
    h2 = y2 * (1.0 + sc2_ref[0]) + sh2_ref[0]
    h2b = h2.astype(BF16)
    h2_ref[0, 0] = h2b
    h2_ref[1, 0] = h2b
    h2_lo = (h2 - h2b.astype(F32)).astype(BF16)
    rl = _dot(h2b, wr_ref[...])
    lg_ref[0] = rl[:, 0:128] + rl[:, 128:256] + _dot(h2_lo, wrhi_ref[...]) + brt_ref[...]


def _merge(x, z, o_f, o_b, gate, br, mod3, wglu, bglu, gnorm, wa, wb, wo, norm2, wr, wr_hi, brt):
    b, l, d = x.shape
    tm = min(MERGE_TILE, l)
    tok = lambda width: pl.BlockSpec((1, tm, width), lambda i, j: (i, j, 0))
    modspec = lambda k: pl.BlockSpec((1, 1, d), lambda i, j: (i * 6 + k, 0, 0))
    full = lambda arr: pl.BlockSpec(arr.shape, lambda i, j: (0,) * arr.ndim)
    consts = [wglu, bglu, gnorm, wa, wb, wo, norm2, wr, wr_hi, brt]
    return pl.pallas_call(
        _merge_kernel,
        grid=(b, l // tm),
        in_specs=[tok(d), pl.BlockSpec((S5_Q, 1, tm // S5_Q, 512), lambda i, j: (0, i, j, 0)),
                  tok(512), tok(512), tok(512), tok(2 * d),
                  modspec(2), modspec(4), modspec(3)] + [full(a) for a in consts],
        out_specs=[tok(d), pl.BlockSpec((2, 1, tm, d), lambda i, j: (0, i, j, 0)), tok(128)],
        out_shape=[jax.ShapeDtypeStruct((b, l, d), F32),
                   jax.ShapeDtypeStruct((2, b, l, d), BF16),
                   jax.ShapeDtypeStruct((b, l, 128), F32)],
        scratch_shapes=[pltpu.VMEM((4, tm, 128), F32)],
        compiler_params=_cparams(("parallel", "parallel")),
        name="branch_merge",
    )(x, z, o_f, o_b, gate, br, mod3, mod3, mod3, *consts)


def _moe_weight_copies(wgu_hbm, wd_hbm, wgu_buf, wd_buf, sem, expert, slot):
    return (pltpu.make_async_copy(wgu_hbm.at[expert], wgu_buf.at[slot], sem.at[0, slot]),
            pltpu.make_async_copy(wd_hbm.at[expert], wd_buf.at[slot], sem.at[1, slot]))


def _moe_kernel(te_ref, nu_ref, nxt_ref, slot_ref, x_ref, bgu_ref, bd_ref, wgu_hbm, wd_hbm, y_ref,
                wgu_buf, wd_buf, wgu_bf, wd_bf, sem):
    i = pl.program_id(0)
    used = i < nu_ref[0]
    copies = functools.partial(_moe_weight_copies, wgu_hbm, wd_hbm, wgu_buf, wd_buf, sem)

    @pl.when(jnp.logical_and(i == 0, used))
    def _():
        for cp in copies(te_ref[0], slot_ref[0]):
            cp.start()

    first = jnp.logical_or(i == 0, te_ref[i] != te_ref[jnp.maximum(i - 1, 0)])

    @pl.when(jnp.logical_and(first, used))
    def _():
        slot = slot_ref[i]
        for cp in copies(te_ref[i], slot):
            cp.wait()
        wgu_bf[...] = wgu_buf[slot].astype(BF16)
        wd_bf[...] = wd_buf[slot].astype(BF16)

        @pl.when(nxt_ref[i] >= 0)
        def _():
            for cp in copies(nxt_ref[i], 1 - slot):
                cp.start()

    @pl.when(used)
    def _():
        de = wd_bf.shape[0]
        gu = _dot(x_ref[...], wgu_bf[...]) + bgu_ref[0]
        gate = jnp.minimum(gu[:, 0:de], SWIGLU_LIMIT)
        up = jnp.clip(gu[:, de:2 * de], -SWIGLU_LIMIT, SWIGLU_LIMIT)
        act = (up + 1.0) * gate * _sigmoid(gate * SWIGLU_ALPHA)
        y_ref[...] = (_dot(act.astype(BF16), wd_bf[...]) + bd_ref[0]).astype(y_ref.dtype)

    @pl.when(jnp.logical_not(used))
    def _():
        y_ref[...] = jnp.zeros_like(y_ref)


def _moe_experts(xs, plan, w_gate_up, b_gate_up, w_down, b_down):
    tile_expert, n_used, next_expert, slot = plan
    p, d = xs.shape
    tm = MOE_TILE
    e, _, n2 = w_gate_up.shape
    de = w_down.shape[1]
    grid_spec = pltpu.PrefetchScalarGridSpec(
        num_scalar_prefetch=4,
        grid=(p // tm,),
        in_specs=[pl.BlockSpec((tm, d), lambda i, te, nu, nx, sl: (jnp.minimum(i, jnp.maximum(nu[0] - 1, 0)), 0)),
                  pl.BlockSpec((1, 1, n2), lambda i, te, nu, nx, sl: (te[i], 0, 0)),
                  pl.BlockSpec((1, 1, d), lambda i, te, nu, nx, sl: (te[i], 0, 0)),
                  pl.BlockSpec(memory_space=pl.ANY),
                  pl.BlockSpec(memory_space=pl.ANY)],
        out_specs=pl.BlockSpec((tm, d), lambda i, te, nu, nx, sl: (i, 0)),
        scratch_shapes=[pltpu.VMEM((2, d, n2), F32), pltpu.VMEM((2, de, d), F32),
                        pltpu.VMEM((d, n2), BF16), pltpu.VMEM((de, d), BF16),
                        pltpu.SemaphoreType.DMA((2, 2))],
    )
    return pl.pallas_call(
        _moe_kernel,
        grid_spec=grid_spec,
        out_shape=jax.ShapeDtypeStruct((p, d), BF16),
        compiler_params=_cparams(("arbitrary",)),
        name="moe_experts",
    )(tile_expert, n_used, next_expert, slot, xs, b_gate_up.reshape(e, 1, n2), b_down.reshape(e, 1, d),
      w_gate_up, w_down)


def _route(logits, n_tokens):
    tm = MOE_TILE
    top_val, top_idx = lax.top_k(logits, TOP_K)
    weights = jax.nn.softmax(top_val, axis=-1)
    flat_e = top_idx.reshape(-1).astype(jnp.int32)
    n_assign = flat_e.shape[0]
    n_fill = N_EXPERTS * tm
    n_rows = n_assign + n_fill
    counts = jnp.sum(jax.nn.one_hot(flat_e, N_EXPERTS, dtype=jnp.int32), axis=0)
    padded = ((counts + tm - 1) // tm) * tm
    pad_end = jnp.cumsum(padded)
    fill = jnp.arange(n_fill, dtype=jnp.int32)
    fill_e = jnp.sum((fill[:, None] >= jnp.cumsum(padded - counts)[None, :]).astype(jnp.int32), axis=1)
    assign = jnp.arange(n_assign, dtype=jnp.int32)
    keys = jnp.concatenate([flat_e, fill_e])
    toks = jnp.concatenate([assign // TOP_K, fill % n_tokens])
    origin = jnp.concatenate([assign, n_assign + fill])
    _, src_token, s_origin = lax.sort((keys, toks, origin), num_keys=1, is_stable=True)
    _, row_of = lax.sort((s_origin, jnp.arange(n_rows, dtype=jnp.int32)), num_keys=1)
    pos = row_of[0:n_assign].reshape(n_tokens, TOP_K)
    n_tiles = n_rows // tm
    n_used = (pad_end[-1] // tm).astype(jnp.int32)
    tile_start = jnp.arange(n_tiles, dtype=jnp.int32) * tm
    tile_expert = jnp.sum((tile_start[:, None] >= pad_end[None, :]).astype(jnp.int32), axis=1)
    last_e = jnp.sum((pad_end[-1] - 1 >= pad_end).astype(jnp.int32))
    tile_expert = jnp.minimum(tile_expert, last_e).astype(jnp.int32)
    nonempty = counts > 0
    idx = jnp.arange(N_EXPERTS, dtype=jnp.int32)
    later = jnp.where(jnp.logical_and(nonempty[None, :], idx[None, :] > idx[:, None]), idx[None, :], N_EXPERTS)
    next_nonempty = jnp.min(later, axis=1)
    next_nonempty = jnp.where(next_nonempty < N_EXPERTS, next_nonempty, -1).astype(jnp.int32)
    ordinal = jnp.cumsum(nonempty.astype(jnp.int32)) - 1
    plan = (tile_expert, n_used.reshape(1), next_nonempty[tile_expert], (ordinal[tile_expert] & 1).astype(jnp.int32))
    return weights, src_token, pos, plan


def _final_kernel(x1_ref, y0_ref, y1_ref, y2_ref, y3_ref, wt_ref, gt2_ref, nf_ref, o_ref):
    wt = wt_ref[0]
    moe = (wt[:, 0:1] * y0_ref[0, 0].astype(F32) + wt[:, 1:2] * y1_ref[0, 0].astype(F32)
           + wt[:, 2:3] * y2_ref[0, 0].astype(F32) + wt[:, 3:4] * y3_ref[0, 0].astype(F32))
    x2 = x1_ref[0] + gt2_ref[0] * moe
    o_ref[0] = x2 * lax.rsqrt(jnp.mean(x2 * x2, axis=-1, keepdims=True) + EPS) * nf_ref[...]


def _final(x1, yg, wt, mod3, norm_f):
    b, l, d = x1.shape
    tm = ROW_TILE
    tok = lambda width: pl.BlockSpec((1, tm, width), lambda i, j: (i, j, 0))
    ysel = lambda k: pl.BlockSpec((1, 1, tm, d), lambda i, j: (k, i, j, 0))
    return pl.pallas_call(
        _final_kernel,
        grid=(b, l // tm),
        in_specs=[tok(d)] + [ysel(k) for k in range(TOP_K)] + [tok(128), pl.BlockSpec((1, 1, d), lambda i, j: (i * 6 + 5, 0, 0)),
                                 pl.BlockSpec((1, d), lambda i, j: (0, 0))],
        out_specs=tok(d),
        out_shape=jax.ShapeDtypeStruct((b, l, d), F32),
        compiler_params=_cparams(("parallel", "parallel")),
        name="combine_final_norm",
    )(x1, *([yg] * TOP_K), wt, mod3, norm_f.reshape(1, d))


def _split_hi_lo(w):
    hi = w.astype(BF16)
    lo = (w - hi.astype(F32)).astype(BF16)
    return hi, lo


def kernel(x, c, ctx, c_ctx, w_mod, b_mod, norm1, w_in, s5_lam_re, s5_lam_im, s5_log_step, s5_b_re, s5_b_im, s5_c_re, s5_c_im, s5_d, s5_w_glu, s5_b_glu, gdn_conv, gdn_a_log, gdn_dt_bias, gdn_norm, w_branch_a, w_branch_b, w_out, norm2, w_router, b_router, w_gate_up, b_gate_up, w_down, b_down, norm_f):
    b, l, d = x.shape
    lc = ctx.shape[1]
    depth = w_mod.shape[0]
    assert depth == 1, "single-layer block: the context stream has no consumer after the token mixer"
    assert l % ROW_TILE == 0 and lc == ROW_TILE and lc & (lc - 1) == 0 and b <= 8
    ly = 0

    cc = jnp.zeros((16, d), F32).at[0:b].set(c).at[b].set(c_ctx)
    mod = _modulation(cc, w_mod[ly], b_mod[ly])
    mod3 = mod[0:b + 1].reshape((b + 1) * 6, 1, d)

    wi = w_in[ly]
    o_u, o_qkv, o_gate, o_ba, o_br = 0, 512, 2048, 2560, 2576
    w_ba = wi[:, o_ba:o_br]
    ba_hi, ba_lo = _split_hi_lo(w_ba)
    pad = lambda t: jnp.pad(t, ((0, 0), (0, 128 - t.shape[1])))
    w_main = jnp.concatenate([wi[:, o_u:o_ba].astype(BF16), wi[:, o_br:].astype(BF16), pad(ba_hi), pad(ba_lo)], axis=1)
    wt = jnp.concatenate([ba_hi.T, ba_lo.T], axis=0)
    ut, qkv, gate, br, ba, bat = _in_projection(x, ctx, mod3, norm1[ly], w_main, pad(ba_hi), wt, ba_hi.T)

    mats = _s5_matrices(s5_lam_re[ly], s5_lam_im[ly], s5_log_step[ly], s5_b_re[ly], s5_b_im[ly],
                        s5_c_re[ly], s5_c_im[ly], s5_d[ly])
    lt = l + lc
    z = _s5_mixer(ut, l, lt, mats)

    nch = lt // GDN_CHUNK
    bat_chunks = jnp.transpose(bat[:, :, 0:lt].reshape(b, 16, nch, GDN_CHUNK), (0, 2, 1, 3))
    ea = jnp.exp(gdn_a_log[ly].astype(F32)).reshape(-1)
    dtb = gdn_dt_bias[ly].astype(F32).reshape(-1)
    prow = jnp.zeros((2, 128), F32).at[0, 8:16].set(ea).at[1, 8:16].set(dtb)
    pcol = jnp.stack([ea, dtb], axis=1)
    conv_w = jnp.zeros((8, qkv.shape[2]), F32).at[0:CONV_K].set(gdn_conv[ly].astype(F32))
    prep = _gdn_prepare(qkv, conv_w, ba, bat_chunks, prow, pcol, l, lt)
    o_f, o_b = _gdn_scan(prep, l)

    wr_hi, wr_lo = _split_hi_lo(jnp.pad(w_router[ly].astype(F32), ((0, 0), (0, 128 - N_EXPERTS))))
    wr = jnp.concatenate([wr_hi, wr_lo], axis=1)
    brt = jnp.pad(b_router[ly].astype(F32), (0, 128 - N_EXPERTS)).reshape(1, 128)
    x1, h2, logits = _merge(x, z, o_f, o_b, gate, br, mod3,
                            s5_w_glu[ly].astype(BF16), s5_b_glu[ly].astype(F32).reshape(1, -1),
                            gdn_norm[ly].astype(F32).reshape(1, -1),
                            w_branch_a[ly].astype(BF16), w_branch_b[ly].astype(BF16), w_out[ly].astype(BF16),
                            norm2[ly].astype(F32).reshape(1, d), wr, wr_hi, brt)

    n_tok = b * l
    weights, src_token, pos, plan = _route(logits.reshape(n_tok, 128)[:, 0:N_EXPERTS], n_tok)
    h2f = h2.reshape(2 * n_tok, d)
    xs = h2f[src_token]
    ys = _moe_experts(xs, plan, w_gate_up[ly], b_gate_up[ly], w_down[ly], b_down[ly])
    yg = ys[pos.T.reshape(-1)].reshape(TOP_K, b, l, d)
    wt4 = jnp.pad(weights, ((0, 0), (0, 128 - TOP_K))).reshape(b, l, 128)
    return _final(x1, yg, wt4, mod3, norm_f)
```

```python
import functools

import numpy as np
import jax
import jax.numpy as jnp
from jax import lax
from jax.experimental import pallas as pl
from jax.experimental.pallas import tpu as pltpu

F32 = jnp.float32
BF16 = jnp.bfloat16
HIGHEST = lax.Precision.HIGHEST

EPS = 1e-6
GRID_W = 64

S5_GROUP = 16
S5_STATE = 64
S5_Q = 16

GDN_HEADS = 4
GDN_DK = 128
GDN_CHUNK = 64
CONV_K = 5

N_EXPERTS = 32
TOP_K = 4
SWIGLU_LIMIT = 7.0
SWIGLU_ALPHA = 1.702

ROW_TILE = 256
MERGE_TILE = 512
IN_TILE = 256
MOE_TILE = 512
VMEM_LIMIT = 56 * 1024 * 1024


def _cparams(sem):
    return pltpu.CompilerParams(dimension_semantics=sem, vmem_limit_bytes=VMEM_LIMIT)


def _sigmoid(v):
    return 1.0 / (1.0 + jnp.exp(-v))


def _softplus(v):
    return jnp.maximum(v, 0.0) + jnp.log(1.0 + jnp.exp(-jnp.abs(v)))


def _dot(a, b):
    return jnp.dot(a, b, preferred_element_type=F32)


def _dot_nt(a, b):
    return lax.dot_general(a, b, (((1,), (1,)), ((), ())), preferred_element_type=F32)


def _dot_tn(a, b):
    return lax.dot_general(a, b, (((0,), (0,)), ((), ())), preferred_element_type=F32)


def _mod_kernel(c_ref, w_ref, b_ref, o_ref):
    cc = c_ref[...]
    s = cc * _sigmoid(cc)
    o_ref[...] = jnp.dot(s, w_ref[...], precision=HIGHEST, preferred_element_type=F32) + b_ref[...]


def _modulation(cc, w_mod, b_mod):
    rows, d = cc.shape
    n = w_mod.shape[1]
    bn = d
    return pl.pallas_call(
        _mod_kernel,
        grid=(n // bn,),
        in_specs=[pl.BlockSpec((rows, d), lambda j: (0, 0)),
                  pl.BlockSpec((d, bn), lambda j: (0, j)),
                  pl.BlockSpec((1, bn), lambda j: (0, j))],
        out_specs=pl.BlockSpec((rows, bn), lambda j: (0, j)),
        out_shape=jax.ShapeDtypeStruct((rows, n), F32),
        compiler_params=_cparams(("arbitrary",)),
        name="adaln_mod",
    )(cc, w_mod, b_mod.reshape(1, n))


def _inproj_kernel(x_ref, c_ref, sc_ref, sh_ref, g_ref, w_ref, wlo_ref, wt_ref, wtlo_ref,
                   ut_ref, qkv_ref, gate_ref, br_ref, ba_ref, bat_ref, us_ref, *, n_lat_blocks):
    j = pl.program_id(1)
    is_ctx = j >= n_lat_blocks
    x = x_ref[0]
    lc = c_ref.shape[1]
    x = jnp.where(is_ctx, jnp.concatenate([c_ref[0], x[lc:]], axis=0) if lc < x.shape[0] else c_ref[0], x)
    ms = jnp.mean(x * x, axis=-1, keepdims=True)
    y = x * lax.rsqrt(ms + EPS) * g_ref[...]
    h = y * (1.0 + sc_ref[0]) + sh_ref[0]
    h_hi = h.astype(BF16)
    h_lo = (h - h_hi.astype(F32)).astype(BF16)
    z = _dot(h_hi, w_ref[...])
    nchunk = us_ref.shape[1] // S5_Q
    for c in range(us_ref.shape[0]):
        us_ref[c] = z[:, c * 128:(c + 1) * 128]
        for t in range(S5_Q):
            ut_ref[t, 0, :, c * 128:(c + 1) * 128] = us_ref[c, pl.ds(t, nchunk, stride=S5_Q), :].astype(BF16)
    qkv_ref[0] = z[:, 512:2048].astype(BF16)
    gate_ref[0] = z[:, 2048:2560].astype(BF16)
    br_ref[0] = z[:, 2560:4608].astype(BF16)
    ba_ref[0] = z[:, 4608:4736] + z[:, 4736:4864] + _dot(h_lo, wlo_ref[...])
    rt = _dot_nt(wt_ref[...], h_hi)
    bat_ref[0] = rt[0:16] + rt[16:32] + _dot_nt(wtlo_ref[...], h_lo)


def _in_projection(x, ctx, mod3, norm1, w_main, w_lo, wt, wt_lo):
    b, l, d = x.shape
    lc = ctx.shape[1]
    tm = min(IN_TILE, l)
    assert l % tm == 0 and lc <= tm
    nl = l // tm
    nb = mod3.shape[0] // 6 - 1
    lt = (nl + 1) * tm
    ncols = w_main.shape[1]

    def x_map(i, j):
        return (i, jnp.minimum(j, nl - 1), 0)

    def mod_map(k):
        return lambda i, j: (jnp.where(j >= nl, nb, i) * 6 + k, 0, 0)

    def tok(width):
        return pl.BlockSpec((1, tm, width), lambda i, j: (i, j, 0))

    const2 = lambda i, j: (0, 0)
    outs = pl.pallas_call(
        functools.partial(_inproj_kernel, n_lat_blocks=nl),
        grid=(b, nl + 1),
        in_specs=[pl.BlockSpec((1, tm, d), x_map),
                  pl.BlockSpec((1, lc, d), lambda i, j: (i, 0, 0)),
                  pl.BlockSpec((1, 1, d), mod_map(1)),
                  pl.BlockSpec((1, 1, d), mod_map(0)),
                  pl.BlockSpec((1, d), const2),
                  pl.BlockSpec((d, ncols), const2),
                  pl.BlockSpec((d, 128), const2),
                  pl.BlockSpec((32, d), const2),
                  pl.BlockSpec((16, d), const2)],
        out_specs=[pl.BlockSpec((S5_Q, 1, tm // S5_Q, 512), lambda i, j: (0, i, j, 0)),
                   tok(1536), tok(512), tok(2048), tok(128),
                   pl.BlockSpec((1, 16, tm), lambda i, j: (i, 0, j))],
        out_shape=[jax.ShapeDtypeStruct((S5_Q, b, lt // S5_Q, 512), BF16),
                   jax.ShapeDtypeStruct((b, lt, 1536), BF16),
                   jax.ShapeDtypeStruct((b, lt, 512), BF16),
                   jax.ShapeDtypeStruct((b, lt, 2048), BF16),
                   jax.ShapeDtypeStruct((b, lt, 128), F32),
                   jax.ShapeDtypeStruct((b, 16, lt), F32)],
        scratch_shapes=[pltpu.VMEM((4, tm, 128), F32)],
        compiler_params=_cparams(("parallel", "arbitrary")),
        name="in_projection",
    )(x, ctx, mod3, mod3, norm1.reshape(1, d), w_main, w_lo, wt, wt_lo)
    return outs


def _s5_matrices(lam_re, lam_im, log_step, b_re, b_im, c_re, c_im, d_skip):
    q, c16 = S5_Q, S5_GROUP
    qc = q * c16
    lr = jnp.minimum(lam_re.astype(F32), -1e-4)
    li = lam_im.astype(F32)
    dt = jnp.exp(log_step.astype(F32))[..., None]
    g, n = lr.shape[1], lr.shape[2]
    pw = jnp.arange(q + 1, dtype=F32)
    mag = jnp.exp(pw * (lr * dt)[..., None])
    ang = pw * (li * dt)[..., None]
    pr, pi = mag * jnp.cos(ang), mag * jnp.sin(ang)
    ar, ai = pr[..., 1], pi[..., 1]
    den = lr * lr + li * li
    fr = ((ar - 1.0) * lr + ai * li) / den
    fi = (ai * lr - (ar - 1.0) * li) / den
    br, bi = b_re.astype(F32), b_im.astype(F32)
    bbr = fr[..., None] * br - fi[..., None] * bi
    bbi = fr[..., None] * bi + fi[..., None] * br
    crt = jnp.swapaxes(c_re.astype(F32), -1, -2)
    cit = jnp.swapaxes(c_im.astype(F32), -1, -2)
    car = (crt[:, :, :, None, :] * pr[..., None] - cit[:, :, :, None, :] * pi[..., None]).reshape(2, g, n, (q + 1) * c16)
    cai = (crt[:, :, :, None, :] * pi[..., None] + cit[:, :, :, None, :] * pr[..., None]).reshape(2, g, n, (q + 1) * c16)
    kern = jnp.sum(bbr[..., None] * car[:, :, :, None, :] - bbi[..., None] * cai[:, :, :, None, :], axis=2)
    kf = kern[0, :, :, 0:qc]
    kb = kern[1].reshape(g, c16, q + 1, c16)[:, :, q - 1::-1].reshape(g, c16, qc)
    taps = (kf, kb, d_skip.astype(F32).reshape(g, c16, 1))

    bbr_t, bbi_t = jnp.swapaxes(bbr, -1, -2), jnp.swapaxes(bbi, -1, -2)
    prt, pit = jnp.swapaxes(pr, -1, -2), jnp.swapaxes(pi, -1, -2)

    def p_block(d, reverse):
        ppr = prt[d, :, q - 1::-1] if reverse else prt[d, :, 0:q]
        ppi = pit[d, :, q - 1::-1] if reverse else pit[d, :, 0:q]
        re = ppr[:, :, None, :] * bbr_t[d][:, None] - ppi[:, :, None, :] * bbi_t[d][:, None]
        im = ppr[:, :, None, :] * bbi_t[d][:, None] + ppi[:, :, None, :] * bbr_t[d][:, None]
        return re.reshape(g, qc, n), im.reshape(g, qc, n)

    pf_re, pf_im = p_block(0, True)
    pb_re, pb_im = p_block(1, False)
    p = jnp.concatenate([pf_re, pb_re, pf_im, pb_im], axis=-1)

    def r_block(d, reverse):
        if reverse:
            sel = lambda t: t[d].reshape(g, n, q + 1, c16)[:, :, q:0:-1].reshape(g, n, qc)
        else:
            sel = lambda t: t[d][:, :, c16:(q + 1) * c16]
        return sel(car), -sel(cai)

    rf_re, rf_im = r_block(0, False)
    rb_re, rb_im = r_block(1, True)
    r = jnp.concatenate([rf_re, rb_re, rf_im, rb_im], axis=1)
    a16 = jnp.stack([jnp.concatenate([pr[0, :, :, q], pr[1, :, :, q]], axis=-1),
                     jnp.concatenate([pi[0, :, :, q], pi[1, :, :, q]], axis=-1)], axis=1)
    return taps, p.astype(BF16), r.astype(BF16), a16


S5_LANE_GROUPS = 8
S5_BATCH = 4


def _s5_kernel(ut_ref, kf_ref, kb_ref, d_ref, p_ref, r_ref, a_ref, zt_ref,
               pi_ref, m_ref, x_ref, xp_ref, v_ref, sk_ref, srb_ref, y_ref, yp_ref, *, n_lat, n_ctx, bb):
    q, ng = S5_Q, S5_LANE_GROUPS
    qc = q * S5_GROUP
    width = ng * qc
    n_all = n_lat + n_ctx
    rows_lat = n_lat * bb

    @pl.when(jnp.logical_and(pl.program_id(0) == 0, pl.program_id(1) == 0))
    def _():
        src = lax.broadcasted_iota(jnp.int32, (width, qc), 0)
        dst = lax.broadcasted_iota(jnp.int32, (width, qc), 1)
        dest_in_group = ((src >> 7) << 4) + (src & 15)
        grp = (src >> 4) & (ng - 1)
        for g in range(ng):
            hit = jnp.logical_and(grp == g, dest_in_group == dst)
            pi_ref[:, g * qc:(g + 1) * qc] = jnp.where(hit, 1.0, 0.0).astype(BF16)

    @pl.when(pl.program_id(1) == 0)
    def _():
        lane = lax.broadcasted_iota(jnp.int32, (S5_GROUP, qc), 1)
        chan = lax.broadcasted_iota(jnp.int32, (S5_GROUP, qc), 0)
        for g in range(ng):
            kf, kb, dg = kf_ref[g], kb_ref[g], d_ref[g]
            for j in range(q):
                lo, hi = j * S5_GROUP, qc - (q - 1 - j) * S5_GROUP
                mf = jnp.where(lane >= lo, pltpu.roll(kf, lo, 1) if lo else kf, 0.0)
                mb = jnp.where(lane < hi, pltpu.roll(kb, hi % qc, 1) if hi % qc else kb, 0.0)
                skip = jnp.where(lane == lo + chan, dg, 0.0)
                m_ref[g, j * S5_GROUP:(j + 1) * S5_GROUP, :] = (mf + mb + skip).astype(BF16)

    for b in range(bb):
        for t in range(q):
            x_ref[t, pl.ds(b, n_all, stride=bb), :] = ut_ref[t, b].astype(F32)
    xb = jnp.concatenate([x_ref[t] for t in range(q)], axis=1).astype(BF16)
    for g in range(ng):
        cols = slice(g * qc, (g + 1) * qc)
        xg = _dot(xb, pi_ref[:, cols]).astype(BF16)
        xp_ref[:, cols] = xg
        v_ref[:, cols] = _dot(xg, p_ref[g])

    a = a_ref[...]
    a1 = jnp.concatenate([a[g, 0:1] for g in range(ng) for _ in range(2)], axis=1)
    a2 = jnp.concatenate([s * a[g, 1:2] for g in range(ng) for s in (-1.0, 1.0)], axis=1)
    fwd_lane = (lax.broadcasted_iota(jnp.int32, (bb, width), 1) & 127) < S5_STATE

    def partner(s):
        tiles = [s[:, i * 128:(i + 1) * 128] for i in range(2 * ng)]
        return jnp.concatenate([tiles[i ^ 1] for i in range(2 * ng)], axis=1)

    def advance(s, vf, vb):
        return a1 * s + a2 * partner(s) + jnp.where(fwd_lane, vf, vb)

    def scan(base, n, s0, store):
        def body(t, s):
            rf = pl.multiple_of((base + 2 * t) * bb, 2 * bb)
            rb = pl.multiple_of((base + n - 2 - 2 * t) * bb, 2 * bb)
            vf = v_ref[pl.ds(rf, 2 * bb), :]
            vb = v_ref[pl.ds(rb, 2 * bb), :]
            s1 = advance(s, vf[0:bb], vb[bb:2 * bb])
            s2 = advance(s1, vf[bb:2 * bb], vb[0:bb])
            if store:
                sk_ref[pl.ds(rf, 2 * bb), :] = jnp.concatenate([s, s1], axis=0)
                srb_ref[pl.ds(rb, 2 * bb), :] = jnp.concatenate([s1, s], axis=0)
            return s2
        return lax.fori_loop(0, n // 2, body, s0)

    s_ctx = scan(n_lat, n_ctx, jnp.zeros((bb, width), F32), False)
    scan(0, n_lat, s_ctx, True)
    lane = (lax.broadcasted_iota(jnp.int32, (rows_lat, width), 1) & 127) < S5_STATE
    st = jnp.where(lane, sk_ref[...], srb_ref[...]).astype(BF16)
    for g in range(ng):
        cols = slice(g * qc, (g + 1) * qc)
        y = _dot(xp_ref[0:rows_lat, cols], m_ref[g]) + _dot(st[:, cols], r_ref[g])
        y_ref[:, cols] = jax.nn.gelu(y).astype(BF16)
    yb = y_ref[...]
    for t in range(q):
        yp_ref[...] = _dot_nt(yb, pi_ref[t * 128:(t + 1) * 128, :])
        for b in range(bb):
            zt_ref[t, b] = yp_ref[pl.ds(b, n_lat, stride=bb), :].astype(BF16)


def _s5_mixer(ut, l, lt, mats):
    (kf, kb, dsk), p, r, a16 = mats
    q, b, _, width = ut.shape
    n_all, n_lat = lt // q, l // q
    ng = S5_LANE_GROUPS
    qc = q * S5_GROUP
    bb = min(S5_BATCH, b)
    nj = width // 128
    wspec = pl.BlockSpec((ng, qc, qc), lambda j, i: (j, 0, 0))
    tspec = pl.BlockSpec((ng, S5_GROUP, qc), lambda j, i: (j, 0, 0))
    return pl.pallas_call(
        functools.partial(_s5_kernel, n_lat=n_lat, n_ctx=n_all - n_lat, bb=bb),
        grid=(nj, b // bb),
        in_specs=[pl.BlockSpec((q, bb, n_all, 128), lambda j, i: (0, i, 0, j)),
                  tspec, tspec, pl.BlockSpec((ng, S5_GROUP, 1), lambda j, i: (j, 0, 0)), wspec, wspec,
                  pl.BlockSpec((ng, 2, 2 * S5_STATE), lambda j, i: (j, 0, 0))],
        out_specs=pl.BlockSpec((q, bb, n_lat, 128), lambda j, i: (0, i, 0, j)),
        out_shape=jax.ShapeDtypeStruct((q, b, n_lat, width), BF16),
        scratch_shapes=[pltpu.VMEM((ng * qc, ng * qc), BF16),
                        pltpu.VMEM((ng, qc, qc), BF16),
                        pltpu.VMEM((q, n_all * bb, 128), F32),
                        pltpu.VMEM((n_all * bb, ng * qc), BF16),
                        pltpu.VMEM((n_all * bb, ng * qc), F32),
                        pltpu.VMEM((n_lat * bb, ng * qc), F32),
                        pltpu.VMEM((n_lat * bb, ng * qc), F32),
                        pltpu.VMEM((n_lat * bb, ng * qc), BF16),
                        pltpu.VMEM((n_lat * bb, 128), F32)],
        compiler_params=_cparams(("arbitrary", "arbitrary")),
        name="s5_scan",
    )(ut, kf, kb, dsk, p, r, a16)


def _inv_unit_triangular_many(mats):
    c = mats[0].shape[0]
    eye = (lax.broadcasted_iota(jnp.int32, (c, c), 0) == lax.broadcasted_iota(jnp.int32, (c, c), 1)).astype(F32)
    prods = [eye - a for a in mats]
    pows = [a.astype(BF16) for a in mats]
    for _ in range(int(np.log2(c)) - 1):
        pows = [_dot(a, a).astype(BF16) for a in pows]
        prods = [p + _dot(p.astype(BF16), a) for p, a in zip(prods, pows)]
    return prods


def _gdn_prep_kernel(qkv_ref, cw_ref, ba_ref, bat_ref, prow_ref, pcol_ref,
                     u_ref, wq_ref, qk_ref, kdt_ref, dl_ref, *, n_lat_blocks, ctx_len):
    j = pl.program_id(1)
    rows = qkv_ref.shape[1]
    c = GDN_CHUNK
    nchunk = rows // c
    seg = jnp.where(j >= n_lat_blocks, ctx_len, GRID_W)
    x = qkv_ref[0]
    ri = lax.broadcasted_iota(jnp.int32, (rows, rows), 0)
    ci = lax.broadcasted_iota(jnp.int32, (rows, rows), 1)
    same_seg = (ci & -seg) == (ri & -seg)
    acc = x.astype(F32) * cw_ref[2:3, :]
    for s in (-2, -1, 1, 2):
        sel = jnp.where(jnp.logical_and(ci == ri + s, same_seg), 1.0, 0.0).astype(BF16)
        acc = acc + _dot(sel, x) * cw_ref[2 + s:3 + s, :]
    act = acc * _sigmoid(acc)

    ii = lax.broadcasted_iota(jnp.int32, (c, c), 0)
    jj = lax.broadcasted_iota(jnp.int32, (c, c), 1)
    incl = (ii >= jj, ii <= jj)
    strict = (ii > jj, ii < jj)
    tri_low = incl[0].astype(F32)
    tri_up = incl[1].astype(F32)
    eye_bf = (ii == jj).astype(BF16)
    ones = jnp.ones((c, c), F32)
    lane = lax.broadcasted_iota(jnp.int32, (c, 128), 1)
    prow = prow_ref[...]
    pcol = pcol_ref[...]
    hp = dict(precision=HIGHEST, preferred_element_type=F32)

    gates = []
    for ci in range(nchunk):
        ba = ba_ref[0, ci * c:(ci + 1) * c, :]
        g_all = -prow[0:1, :] * _softplus(ba + prow[1:2, :])
        g_all = jnp.where(jnp.logical_and(lane >= 8, lane < 16), g_all, 0.0)
        bat = bat_ref[0, ci]
        g_row = -pcol[:, 0:1] * _softplus(bat[8:16, :] + pcol[:, 1:2])
        gates.append(dict(beta=_sigmoid(ba),
                          gc=(jnp.dot(tri_low, g_all, **hp), jnp.dot(tri_up, g_all, **hp)),
                          gtot=jnp.dot(ones, g_all, **hp),
                          gr=(jnp.dot(g_row, tri_up, **hp), jnp.dot(g_row, tri_low, **hp))))

    qn, kn, vv = [], [], []
    for h in range(GDN_HEADS):
        qh = act[:, h * 128:(h + 1) * 128]
        kh = act[:, 512 + h * 128:512 + (h + 1) * 128]
        qn.append(qh * lax.rsqrt(jnp.sum(qh * qh, axis=-1, keepdims=True) + EPS) * (GDN_DK ** -0.5))
        kn.append(kh * lax.rsqrt(jnp.sum(kh * kh, axis=-1, keepdims=True) + EPS))
        vv.append(act[:, 1024 + h * 128:1024 + (h + 1) * 128])

    pairs = [(ci, h) for ci in range(nchunk) for h in range(GDN_HEADS)]
    sl = lambda t, ci: t[ci * c:(ci + 1) * c]
    kb = {p: sl(kn[p[1]], p[0]).astype(BF16) for p in pairs}
    kk = {p: _dot_nt(kb[p], kb[p]) for p in pairs}
    qkm = {p: _dot_nt(sl(qn[p[1]], p[0]).astype(BF16), kb[p]) for p in pairs}

    probs = [(ci, h, d) for ci in range(nchunk) for h in range(GDN_HEADS) for d in range(2)]
    beta, gc, gt, decay, a_mats = {}, {}, {}, {}, []
    for (ci, h, d) in probs:
        col = d * GDN_HEADS + h
        gi = gates[ci]
        beta[ci, h, d] = gi["beta"][:, col:col + 1]
        gc[ci, h, d] = gi["gc"][d][:, 8 + col:9 + col]
        gt[ci, h, d] = gi["gtot"][:, 8 + col:9 + col]
        grow = gi["gr"][d][col:col + 1, :]
        decay[ci, h, d] = jnp.exp(jnp.where(incl[d], gc[ci, h, d] - grow, -jnp.inf))
        a_mats.append(jnp.where(strict[d], beta[ci, h, d] * kk[ci, h] * decay[ci, h, d], 0.0))
    tinv = _inv_unit_triangular_many(a_mats)

    eg, sols, kdts = {}, {}, {}
    for n, (ci, h, d) in enumerate(probs):
        p = (ci, h, d)
        eg[p] = jnp.exp(gc[p])
        kh = sl(kn[h], ci)
        rhs = jnp.concatenate([sl(vv[h], ci) * beta[p], kh * (beta[p] * eg[p])], axis=-1).astype(BF16)
        sols[p] = _dot(tinv[n].astype(BF16), rhs)
        kdts[p] = _dot_tn((kh * jnp.exp(gt[p] - gc[p])).astype(BF16), eye_bf)
    for (ci, h, d) in probs:
        p = (ci, h, d)
        u_ref[0, d, h, ci] = sols[p][:, 0:128]
        wq_ref[0, d, h, ci, 0:c, :] = sols[p][:, 128:256].astype(BF16)
        wq_ref[0, d, h, ci, c:2 * c, :] = (sl(qn[h], ci) * eg[p]).astype(BF16)
        qk_ref[0, d, h, ci] = (qkm[ci, h] * decay[p]).astype(BF16)
        kdt_ref[0, d, h, ci] = kdts[p].astype(BF16)
        dl_ref[0, d, h, ci] = jnp.broadcast_to(jnp.exp(gt[p][0:8, :]), (8, 128))


def _gdn_prepare(qkv, conv_w, ba, bat_chunks, prow, pcol, l, lt):
    b, _, width = qkv.shape
    tm = ROW_TILE
    c = GDN_CHUNK
    cpb = tm // c
    nblk = lt // tm
    nch = lt // c
    hd = (b, 2, GDN_HEADS, nch)

    def blk(shape_tail, dtype):
        return (pl.BlockSpec((1, 2, GDN_HEADS, cpb) + shape_tail, lambda i, j: (i, 0, 0, j, 0, 0)),
                jax.ShapeDtypeStruct(hd + shape_tail, dtype))

    specs = [blk((c, 128), F32), blk((2 * c, 128), BF16), blk((c, c), BF16), blk((GDN_DK, c), BF16),
             blk((8, 128), F32)]
    return pl.pallas_call(
        functools.partial(_gdn_prep_kernel, n_lat_blocks=l // tm, ctx_len=lt - l),
        grid=(b, nblk),
        in_specs=[pl.BlockSpec((1, tm, width), lambda i, j: (i, j, 0)),
                  pl.BlockSpec((8, width), lambda i, j: (0, 0)),
                  pl.BlockSpec((1, tm, 128), lambda i, j: (i, j, 0)),
                  pl.BlockSpec((1, cpb, 16, c), lambda i, j: (i, j, 0, 0)),
                  pl.BlockSpec((2, 128), lambda i, j: (0, 0)),
                  pl.BlockSpec((8, 2), lambda i, j: (0, 0))],
        out_specs=[s for s, _ in specs],
        out_shape=[o for _, o in specs],
        compiler_params=_cparams(("parallel", "parallel")),
        name="gdn_prepare",
    )(qkv, conv_w, ba, bat_chunks, prow, pcol)


SCAN_GROUP = 2


def _gdn_scan_kernel(uf, wqf, qkf, kdtf, dlf, ub, wqb, qkb, kdtb, dlb, of_ref, ob_ref, s_ref):
    t = pl.program_id(0)
    nb = s_ref.shape[0]
    c = GDN_CHUNK

    @pl.when(t == 0)
    def _():
        s_ref[...] = jnp.zeros_like(s_ref)

    refs = ((uf, wqf, qkf, kdtf, dlf, of_ref), (ub, wqb, qkb, kdtb, dlb, ob_ref))
    for b0 in range(0, nb, SCAN_GROUP):
        chains = [(bi, d, h) for bi in range(b0, min(b0 + SCAN_GROUP, nb)) for d in range(2)
                  for h in range(GDN_HEADS)]
        s = {k: s_ref[k[0], k[1], k[2]] for k in chains}
        sb = {k: s[k].astype(BF16) for k in chains}
        r = {k: _dot(refs[k[1]][1][k[0], 0, k[2], 0], sb[k]) for k in chains}
        vb = {k: (refs[k[1]][0][k[0], 0, k[2], 0] - r[k][0:c]).astype(BF16) for k in chains}
        o = {k: r[k][c:2 * c] + _dot(refs[k[1]][2][k[0], 0, k[2], 0], vb[k]) for k in chains}
        sn = {k: s[k] * refs[k[1]][4][k[0], 0, k[2], 0][0:1, :] + _dot(refs[k[1]][3][k[0], 0, k[2], 0], vb[k])
              for k in chains}
        for k in chains:
            s_ref[k[0], k[1], k[2]] = sn[k]
            refs[k[1]][5][k[0], :, k[2] * 128:(k[2] + 1) * 128] = o[k]


def _gdn_scan(prep, l):
    u = prep[0]
    b, _, heads, nch, c, _ = u.shape
    n_lat = l // c
    n_ctx = nch - n_lat
    lt = nch * c

    def fwd_chunk(t):
        return jnp.where(t < n_ctx, n_lat + t, t - n_ctx)

    def bwd_chunk(t):
        return nch - 1 - t

    def spec(arr, chunk_of, d):
        tail = arr.shape[4:]
        return pl.BlockSpec((b, 1, heads, 1) + tail, lambda t: (0, d, 0, chunk_of(t), 0, 0))

    ins = [spec(a, fwd_chunk, 0) for a in prep] + [spec(a, bwd_chunk, 1) for a in prep]
    width = heads * 128
    return pl.pallas_call(
        _gdn_scan_kernel,
        grid=(nch,),
        in_specs=ins,
        out_specs=[pl.BlockSpec((b, c, width), lambda t: (0, fwd_chunk(t), 0)),
                   pl.BlockSpec((b, c, width), lambda t: (0, bwd_chunk(t), 0))],
        out_shape=[jax.ShapeDtypeStruct((b, lt, width), F32)] * 2,
        scratch_shapes=[pltpu.VMEM((b, 2, heads, GDN_DK, 128), F32)],
        compiler_params=_cparams(("arbitrary",)),
        name="gdn_scan",
    )(*prep, *prep)


def _merge_kernel(x_ref, z_ref, of_ref, ob_ref, gate_ref, br_ref, gt1_ref, sc2_ref, sh2_ref,
                  wglu_ref, bglu_ref, gnorm_ref, wa_ref, wb_ref, wo_ref, n2_ref, wr_ref, wrhi_ref, brt_ref,
                  x1_ref, h2_ref, lg_ref, zs_ref):
    nchunk = zs_ref.shape[1] // S5_Q
    for t in range(S5_Q):
        zt = z_ref[t, 0].astype(F32)
        for c in range(zs_ref.shape[0]):
            zs_ref[c, pl.ds(t, nchunk, stride=S5_Q), :] = zt[:, c * 128:(c + 1) * 128]
    zf = jnp.concatenate([zs_ref[c] for c in range(zs_ref.shape[0])], axis=1)
    z = zf.astype(BF16)
    ya = zf * _sigmoid(_dot(z, wglu_ref[...]) + bglu_ref[...])
    o = of_ref[0] + ob_ref[0]
    gate = gate_ref[0].astype(F32)
    parts = []
    for h in range(GDN_HEADS):
        oh = o[:, h * 128:(h + 1) * 128]
        parts.append(oh * lax.rsqrt(jnp.mean(oh * oh, axis=-1, keepdims=True) + EPS) * gnorm_ref[...])
    yb = jnp.concatenate(parts, axis=-1) * (gate * _sigmoid(gate))
    br = br_ref[0].astype(F32)
    d = x_ref.shape[2]
    ga = _sigmoid(br[:, 0:d])
    gb = _sigmoid(br[:, d:2 * d])
    m = ga * _dot(ya.astype(BF16), wa_ref[...]) + gb * _dot(yb.astype(BF16), wb_ref[...])
    mix = _dot(m.astype(BF16), wo_ref[...])
    x1 = x_ref[0] + gt1_ref[0] * mix
    x1_ref[0] = x1
    y2 = x1 * lax.rsqrt(jnp.mean(x1 * x1, axis=-1, keepdims=True) + EPS) * n2_ref[...]
    h2 = y2 * (1.0 + sc2_ref[0]) + sh2_ref[0]
    h2b = h2.astype(BF16)
    h2_ref[0, 0] = h2b
    h2_ref[1, 0] = h2b
    h2_lo = (h2 - h2b.astype(F32)).astype(BF16)
    rl = _dot(h2b, wr_ref[...])
    lg_ref[0] = rl[:, 0:128] + rl[:, 128:256] + _dot(h2_lo, wrhi_ref[...]) + brt_ref[...]


def _merge(x, z, o_f, o_b, gate, br, mod3, wglu, bglu, gnorm, wa, wb, wo, norm2, wr, wr_hi, brt):
    b, l, d = x.shape
    tm = min(MERGE_TILE, l)
    tok = lambda width: pl.BlockSpec((1, tm, width), lambda i, j: (i, j, 0))
    modspec = lambda k: pl.BlockSpec((1, 1, d), lambda i, j: (i * 6 + k, 0, 0))
    full = lambda arr: pl.BlockSpec(arr.shape, lambda i, j: (0,) * arr.ndim)
    consts = [wglu, bglu, gnorm, wa, wb, wo, norm2, wr, wr_hi, brt]
    return pl.pallas_call(
        _merge_kernel,
        grid=(b, l // tm),
        in_specs=[tok(d), pl.BlockSpec((S5_Q, 1, tm // S5_Q, 512), lambda i, j: (0, i, j, 0)),
                  tok(512), tok(512), tok(512), tok(2 * d),
                  modspec(2), modspec(4), modspec(3)] + [full(a) for a in consts],
        out_specs=[tok(d), pl.BlockSpec((2, 1, tm, d), lambda i, j: (0, i, j, 0)), tok(128)],
        out_shape=[jax.ShapeDtypeStruct((b, l, d), F32),
                   jax.ShapeDtypeStruct((2, b, l, d), BF16),
                   jax.ShapeDtypeStruct((b, l, 128), F32)],
        scratch_shapes=[pltpu.VMEM((4, tm, 128), F32)],
        compiler_params=_cparams(("parallel", "parallel")),
        name="branch_merge",
    )(x, z, o_f, o_b, gate, br, mod3, mod3, mod3, *consts)


def _moe_weight_copies(wgu_hbm, wd_hbm, wgu_buf, wd_buf, sem, expert, slot):
    return (pltpu.make_async_copy(wgu_hbm.at[expert], wgu_buf.at[slot], sem.at[0, slot]),
            pltpu.make_async_copy(wd_hbm.at[expert], wd_buf.at[slot], sem.at[1, slot]))


def _moe_kernel(te_ref, nu_ref, nxt_ref, slot_ref, x_ref, bgu_ref, bd_ref, wgu_hbm, wd_hbm, y_ref,
                wgu_buf, wd_buf, wgu_bf, wd_bf, sem):
    i = pl.program_id(0)
    used = i < nu_ref[0]
    copies = functools.partial(_moe_weight_copies, wgu_hbm, wd_hbm, wgu_buf, wd_buf, sem)

    @pl.when(jnp.logical_and(i == 0, used))
    def _():
        for cp in copies(te_ref[0], slot_ref[0]):
            cp.start()

    first = jnp.logical_or(i == 0, te_ref[i] != te_ref[jnp.maximum(i - 1, 0)])

    @pl.when(jnp.logical_and(first, used))
    def _():
        slot = slot_ref[i]
        for cp in copies(te_ref[i], slot):
            cp.wait()
        wgu_bf[...] = wgu_buf[slot].astype(BF16)
        wd_bf[...] = wd_buf[slot].astype(BF16)

        @pl.when(nxt_ref[i] >= 0)
        def _():
            for cp in copies(nxt_ref[i], 1 - slot):
                cp.start()

    @pl.when(used)
    def _():
        de = wd_bf.shape[0]
        gu = _dot(x_ref[...], wgu_bf[...]) + bgu_ref[0]
        gate = jnp.minimum(gu[:, 0:de], SWIGLU_LIMIT)
        up = jnp.clip(gu[:, de:2 * de], -SWIGLU_LIMIT, SWIGLU_LIMIT)
        act = (up + 1.0) * gate * _sigmoid(gate * SWIGLU_ALPHA)
        y_ref[...] = (_dot(act.astype(BF16), wd_bf[...]) + bd_ref[0]).astype(y_ref.dtype)

    @pl.when(jnp.logical_not(used))
    def _():
        y_ref[...] = jnp.zeros_like(y_ref)


def _moe_experts(xs, plan, w_gate_up, b_gate_up, w_down, b_down):
    tile_expert, n_used, next_expert, slot = plan
    p, d = xs.shape
    tm = MOE_TILE
    e, _, n2 = w_gate_up.shape
    de = w_down.shape[1]
    grid_spec = pltpu.PrefetchScalarGridSpec(
        num_scalar_prefetch=4,
        grid=(p // tm,),
        in_specs=[pl.BlockSpec((tm, d), lambda i, te, nu, nx, sl: (jnp.minimum(i, jnp.maximum(nu[0] - 1, 0)), 0)),
                  pl.BlockSpec((1, 1, n2), lambda i, te, nu, nx, sl: (te[i], 0, 0)),
                  pl.BlockSpec((1, 1, d), lambda i, te, nu, nx, sl: (te[i], 0, 0)),
                  pl.BlockSpec(memory_space=pl.ANY),
                  pl.BlockSpec(memory_space=pl.ANY)],
        out_specs=pl.BlockSpec((tm, d), lambda i, te, nu, nx, sl: (i, 0)),
        scratch_shapes=[pltpu.VMEM((2, d, n2), F32), pltpu.VMEM((2, de, d), F32),
                        pltpu.VMEM((d, n2), BF16), pltpu.VMEM((de, d), BF16),
                        pltpu.SemaphoreType.DMA((2, 2))],
    )
    return pl.pallas_call(
        _moe_kernel,
        grid_spec=grid_spec,
        out_shape=jax.ShapeDtypeStruct((p, d), BF16),
        compiler_params=_cparams(("arbitrary",)),
        name="moe_experts",
    )(tile_expert, n_used, next_expert, slot, xs, b_gate_up.reshape(e, 1, n2), b_down.reshape(e, 1, d),
      w_gate_up, w_down)


def _route(logits, n_tokens):
    tm = MOE_TILE
    top_val, top_idx = lax.top_k(logits, TOP_K)
    weights = jax.nn.softmax(top_val, axis=-1)
    flat_e = top_idx.reshape(-1).astype(jnp.int32)
    n_assign = flat_e.shape[0]
    n_fill = N_EXPERTS * tm
    n_rows = n_assign + n_fill
    counts = jnp.sum(jax.nn.one_hot(flat_e, N_EXPERTS, dtype=jnp.int32), axis=0)
    padded = ((counts + tm - 1) // tm) * tm
    pad_end = jnp.cumsum(padded)
    fill = jnp.arange(n_fill, dtype=jnp.int32)
    fill_e = jnp.sum((fill[:, None] >= jnp.cumsum(padded - counts)[None, :]).astype(jnp.int32), axis=1)
    bits = int(n_rows - 1).bit_length()
    assert (N_EXPERTS + 1) << bits < 2 ** 31
    keys = jnp.concatenate([flat_e, fill_e])
    packed = (keys << bits) | jnp.arange(n_rows, dtype=jnp.int32)
    s_origin = lax.sort(packed) & ((1 << bits) - 1)
    src_token = jnp.where(s_origin < n_assign, s_origin // TOP_K, (s_origin - n_assign) % n_tokens)
    _, row_of = lax.sort((s_origin, jnp.arange(n_rows, dtype=jnp.int32)), num_keys=1)
    pos = row_of[0:n_assign].reshape(n_tokens, TOP_K)
    n_tiles = n_rows // tm
    n_used = (pad_end[-1] // tm).astype(jnp.int32)
    tile_start = jnp.arange(n_tiles, dtype=jnp.int32) * tm
    tile_expert = jnp.sum((tile_start[:, None] >= pad_end[None, :]).astype(jnp.int32), axis=1)
    last_e = jnp.sum((pad_end[-1] - 1 >= pad_end).astype(jnp.int32))
    tile_expert = jnp.minimum(tile_expert, last_e).astype(jnp.int32)
    nonempty = counts > 0
    idx = jnp.arange(N_EXPERTS, dtype=jnp.int32)
    later = jnp.where(jnp.logical_and(nonempty[None, :], idx[None, :] > idx[:, None]), idx[None, :], N_EXPERTS)
    next_nonempty = jnp.min(later, axis=1)
    next_nonempty = jnp.where(next_nonempty < N_EXPERTS, next_nonempty, -1).astype(jnp.int32)
    ordinal = jnp.cumsum(nonempty.astype(jnp.int32)) - 1
    plan = (tile_expert, n_used.reshape(1), next_nonempty[tile_expert], (ordinal[tile_expert] & 1).astype(jnp.int32))
    return weights, src_token, pos, plan


def _final_kernel(x1_ref, y0_ref, y1_ref, y2_ref, y3_ref, wt_ref, gt2_ref, nf_ref, o_ref):
    wt = wt_ref[0]
    moe = (wt[:, 0:1] * y0_ref[0, 0].astype(F32) + wt[:, 1:2] * y1_ref[0, 0].astype(F32)
           + wt[:, 2:3] * y2_ref[0, 0].astype(F32) + wt[:, 3:4] * y3_ref[0, 0].astype(F32))
    x2 = x1_ref[0] + gt2_ref[0] * moe
    o_ref[0] = x2 * lax.rsqrt(jnp.mean(x2 * x2, axis=-1, keepdims=True) + EPS) * nf_ref[...]


def _final(x1, yg, wt, mod3, norm_f):
    b, l, d = x1.shape
    tm = ROW_TILE
    tok = lambda width: pl.BlockSpec((1, tm, width), lambda i, j: (i, j, 0))
    ysel = lambda k: pl.BlockSpec((1, 1, tm, d), lambda i, j: (k, i, j, 0))
    return pl.pallas_call(
        _final_kernel,
        grid=(b, l // tm),
        in_specs=[tok(d)] + [ysel(k) for k in range(TOP_K)] + [tok(128), pl.BlockSpec((1, 1, d), lambda i, j: (i * 6 + 5, 0, 0)),
                                 pl.BlockSpec((1, d), lambda i, j: (0, 0))],
        out_specs=tok(d),
        out_shape=jax.ShapeDtypeStruct((b, l, d), F32),
        compiler_params=_cparams(("parallel", "parallel")),
        name="combine_final_norm",
    )(x1, *([yg] * TOP_K), wt, mod3, norm_f.reshape(1, d))


def _split_hi_lo(w):
    hi = w.astype(BF16)
    lo = (w - hi.astype(F32)).astype(BF16)
    return hi, lo


def kernel(x, c, ctx, c_ctx, w_mod, b_mod, norm1, w_in, s5_lam_re, s5_lam_im, s5_log_step, s5_b_re, s5_b_im, s5_c_re, s5_c_im, s5_d, s5_w_glu, s5_b_glu, gdn_conv, gdn_a_log, gdn_dt_bias, gdn_norm, w_branch_a, w_branch_b, w_out, norm2, w_router, b_router, w_gate_up, b_gate_up, w_down, b_down, norm_f):
    b, l, d = x.shape
    lc = ctx.shape[1]
    depth = w_mod.shape[0]
    assert depth == 1, "single-layer block: the context stream has no consumer after the token mixer"
    assert l % ROW_TILE == 0 and lc == ROW_TILE and lc & (lc - 1) == 0 and b <= 8
    ly = 0

    cc = jnp.zeros((16, d), F32).at[0:b].set(c).at[b].set(c_ctx)
    mod = _modulation(cc, w_mod[ly], b_mod[ly])
    mod3 = mod[0:b + 1].reshape((b + 1) * 6, 1, d)

    wi = w_in[ly]
    o_u, o_qkv, o_gate, o_ba, o_br = 0, 512, 2048, 2560, 2576
    w_ba = wi[:, o_ba:o_br]
    ba_hi, ba_lo = _split_hi_lo(w_ba)
    pad = lambda t: jnp.pad(t, ((0, 0), (0, 128 - t.shape[1])))
    w_main = jnp.concatenate([wi[:, o_u:o_ba].astype(BF16), wi[:, o_br:].astype(BF16), pad(ba_hi), pad(ba_lo)], axis=1)
    wt = jnp.concatenate([ba_hi.T, ba_lo.T], axis=0)
    ut, qkv, gate, br, ba, bat = _in_projection(x, ctx, mod3, norm1[ly], w_main, pad(ba_hi), wt, ba_hi.T)

    mats = _s5_matrices(s5_lam_re[ly], s5_lam_im[ly], s5_log_step[ly], s5_b_re[ly], s5_b_im[ly],
                        s5_c_re[ly], s5_c_im[ly], s5_d[ly])
    lt = l + lc
    z = _s5_mixer(ut, l, lt, mats)

    nch = lt // GDN_CHUNK
    bat_chunks = jnp.transpose(bat[:, :, 0:lt].reshape(b, 16, nch, GDN_CHUNK), (0, 2, 1, 3))
    ea = jnp.exp(gdn_a_log[ly].astype(F32)).reshape(-1)
    dtb = gdn_dt_bias[ly].astype(F32).reshape(-1)
    prow = jnp.zeros((2, 128), F32).at[0, 8:16].set(ea).at[1, 8:16].set(dtb)
    pcol = jnp.stack([ea, dtb], axis=1)
    conv_w = jnp.zeros((8, qkv.shape[2]), F32).at[0:CONV_K].set(gdn_conv[ly].astype(F32))
    prep = _gdn_prepare(qkv, conv_w, ba, bat_chunks, prow, pcol, l, lt)
    o_f, o_b = _gdn_scan(prep, l)

    wr_hi, wr_lo = _split_hi_lo(jnp.pad(w_router[ly].astype(F32), ((0, 0), (0, 128 - N_EXPERTS))))
    wr = jnp.concatenate([wr_hi, wr_lo], axis=1)
    brt = jnp.pad(b_router[ly].astype(F32), (0, 128 - N_EXPERTS)).reshape(1, 128)
    x1, h2, logits = _merge(x, z, o_f, o_b, gate, br, mod3,
                            s5_w_glu[ly].astype(BF16), s5_b_glu[ly].astype(F32).reshape(1, -1),
                            gdn_norm[ly].astype(F32).reshape(1, -1),
                            w_branch_a[ly].astype(BF16), w_branch_b[ly].astype(BF16), w_out[ly].astype(BF16),
                            norm2[ly].astype(F32).reshape(1, d), wr, wr_hi, brt)

    n_tok = b * l
    weights, src_token, pos, plan = _route(logits.reshape(n_tok, 128)[:, 0:N_EXPERTS], n_tok)
    h2f = h2.reshape(2 * n_tok, d)
    xs = h2f[src_token]
    ys = _moe_experts(xs, plan, w_gate_up[ly], b_gate_up[ly], w_down[ly], b_down[ly])
    yg = ys[pos.T.reshape(-1)].reshape(TOP_K, b, l, d)
    wt4 = jnp.pad(weights, ((0, 0), (0, 128 - TOP_K))).reshape(b, l, 128)
    return _final(x1, yg, wt4, mod3, norm_f)
```

```python
import functools

import numpy as np
import jax
import jax.numpy as jnp
from jax import lax
from jax.experimental import pallas as pl
from jax.experimental.pallas import tpu as pltpu

F32 = jnp.float32
BF16 = jnp.bfloat16
HIGHEST = lax.Precision.HIGHEST

EPS = 1e-6
GRID_W = 64

S5_GROUP = 16
S5_STATE = 64
S5_Q = 16

GDN_HEADS = 4
GDN_DK = 128
GDN_CHUNK = 64
CONV_K = 5

N_EXPERTS = 32
TOP_K = 4
SWIGLU_LIMIT = 7.0
SWIGLU_ALPHA = 1.702

ROW_TILE = 256
MERGE_TILE = 512
IN_TILE = 256
MOE_TILE = 512
MOE_HIDDEN_CHUNK = 256
VMEM_LIMIT = 56 * 1024 * 1024


def _cparams(sem):
    return pltpu.CompilerParams(dimension_semantics=sem, vmem_limit_bytes=VMEM_LIMIT)


def _sigmoid(v):
    return 1.0 / (1.0 + jnp.exp(-v))


def _softplus(v):
    return jnp.maximum(v, 0.0) + jnp.log(1.0 + jnp.exp(-jnp.abs(v)))


def _dot(a, b):
    return jnp.dot(a, b, preferred_element_type=F32)


def _dot_nt(a, b):
    return lax.dot_general(a, b, (((1,), (1,)), ((), ())), preferred_element_type=F32)


def _dot_tn(a, b):
    return lax.dot_general(a, b, (((0,), (0,)), ((), ())), preferred_element_type=F32)


def _mod_kernel(c_ref, w_ref, b_ref, o_ref):
    cc = c_ref[...]
    s = cc * _sigmoid(cc)
    o_ref[...] = jnp.dot(s, w_ref[...], precision=HIGHEST, preferred_element_type=F32) + b_ref[...]


def _modulation(cc, w_mod, b_mod):
    rows, d = cc.shape
    n = w_mod.shape[1]
    bn = d
    return pl.pallas_call(
        _mod_kernel,
        grid=(n // bn,),
        in_specs=[pl.BlockSpec((rows, d), lambda j: (0, 0)),
                  pl.BlockSpec((d, bn), lambda j: (0, j)),
                  pl.BlockSpec((1, bn), lambda j: (0, j))],
        out_specs=pl.BlockSpec((rows, bn), lambda j: (0, j)),
        out_shape=jax.ShapeDtypeStruct((rows, n), F32),
        compiler_params=_cparams(("arbitrary",)),
        name="adaln_mod",
    )(cc, w_mod, b_mod.reshape(1, n))


def _inproj_kernel(x_ref, c_ref, sc_ref, sh_ref, g_ref, w_ref, wlo_ref, wt_ref, wtlo_ref,
                   ut_ref, qkv_ref, gate_ref, br_ref, ba_ref, bat_ref, us_ref, *, n_lat_blocks):
    j = pl.program_id(1)
    is_ctx = j >= n_lat_blocks
    x = x_ref[0]
    lc = c_ref.shape[1]
    x = jnp.where(is_ctx, jnp.concatenate([c_ref[0], x[lc:]], axis=0) if lc < x.shape[0] else c_ref[0], x)
    ms = jnp.mean(x * x, axis=-1, keepdims=True)
    y = x * lax.rsqrt(ms + EPS) * g_ref[...]
    h = y * (1.0 + sc_ref[0]) + sh_ref[0]
    h_hi = h.astype(BF16)
    h_lo = (h - h_hi.astype(F32)).astype(BF16)
    z = _dot(h_hi, w_ref[...])
    nchunk = us_ref.shape[1] // S5_Q
    for c in range(us_ref.shape[0]):
        us_ref[c] = z[:, c * 128:(c + 1) * 128]
        for t in range(S5_Q):
            ut_ref[t, 0, :, c * 128:(c + 1) * 128] = us_ref[c, pl.ds(t, nchunk, stride=S5_Q), :].astype(BF16)
    qkv_ref[0] = z[:, 512:2048].astype(BF16)
    gate_ref[0] = z[:, 2048:2560].astype(BF16)
    br_ref[0] = z[:, 2560:4608].astype(BF16)
    ba_ref[0] = z[:, 4608:4736] + z[:, 4736:4864] + _dot(h_lo, wlo_ref[...])
    rt = _dot_nt(wt_ref[...], h_hi)
    bat_ref[0] = rt[0:16] + rt[16:32] + _dot_nt(wtlo_ref[...], h_lo)


def _in_projection(x, ctx, mod3, norm1, w_main, w_lo, wt, wt_lo):
    b, l, d = x.shape
    lc = ctx.shape[1]
    tm = min(IN_TILE, l)
    assert l % tm == 0 and lc <= tm
    nl = l // tm
    nb = mod3.shape[0] // 6 - 1
    lt = (nl + 1) * tm
    ncols = w_main.shape[1]

    def x_map(i, j):
        return (i, jnp.minimum(j, nl - 1), 0)

    def mod_map(k):
        return lambda i, j: (jnp.where(j >= nl, nb, i) * 6 + k, 0, 0)

    def tok(width):
        return pl.BlockSpec((1, tm, width), lambda i, j: (i, j, 0))

    const2 = lambda i, j: (0, 0)
    outs = pl.pallas_call(
        functools.partial(_inproj_kernel, n_lat_blocks=nl),
        grid=(b, nl + 1),
        in_specs=[pl.BlockSpec((1, tm, d), x_map),
                  pl.BlockSpec((1, lc, d), lambda i, j: (i, 0, 0)),
                  pl.BlockSpec((1, 1, d), mod_map(1)),
                  pl.BlockSpec((1, 1, d), mod_map(0)),
                  pl.BlockSpec((1, d), const2),
                  pl.BlockSpec((d, ncols), const2),
                  pl.BlockSpec((d, 128), const2),
                  pl.BlockSpec((32, d), const2),
                  pl.BlockSpec((16, d), const2)],
        out_specs=[pl.BlockSpec((S5_Q, 1, tm // S5_Q, 512), lambda i, j: (0, i, j, 0)),
                   tok(1536), tok(512), tok(2048), tok(128),
                   pl.BlockSpec((1, 16, tm), lambda i, j: (i, 0, j))],
        out_shape=[jax.ShapeDtypeStruct((S5_Q, b, lt // S5_Q, 512), BF16),
                   jax.ShapeDtypeStruct((b, lt, 1536), BF16),
                   jax.ShapeDtypeStruct((b, lt, 512), BF16),
                   jax.ShapeDtypeStruct((b, lt, 2048), BF16),
                   jax.ShapeDtypeStruct((b, lt, 128), F32),
                   jax.ShapeDtypeStruct((b, 16, lt), F32)],
        scratch_shapes=[pltpu.VMEM((4, tm, 128), F32)],
        compiler_params=_cparams(("parallel", "arbitrary")),
        name="in_projection",
    )(x, ctx, mod3, mod3, norm1.reshape(1, d), w_main, w_lo, wt, wt_lo)
    return outs


def _s5_matrices(lam_re, lam_im, log_step, b_re, b_im, c_re, c_im, d_skip):
    q, c16 = S5_Q, S5_GROUP
    qc = q * c16
    lr = jnp.minimum(lam_re.astype(F32), -1e-4)
    li = lam_im.astype(F32)
    dt = jnp.exp(log_step.astype(F32))[..., None]
    g, n = lr.shape[1], lr.shape[2]
    pw = jnp.arange(q + 1, dtype=F32)
    mag = jnp.exp(pw * (lr * dt)[..., None])
    ang = pw * (li * dt)[..., None]
    pr, pi = mag * jnp.cos(ang), mag * jnp.sin(ang)
    ar, ai = pr[..., 1], pi[..., 1]
    den = lr * lr + li * li
    fr = ((ar - 1.0) * lr + ai * li) / den
    fi = (ai * lr - (ar - 1.0) * li) / den
    br, bi = b_re.astype(F32), b_im.astype(F32)
    bbr = fr[..., None] * br - fi[..., None] * bi
    bbi = fr[..., None] * bi + fi[..., None] * br
    crt = jnp.swapaxes(c_re.astype(F32), -1, -2)
    cit = jnp.swapaxes(c_im.astype(F32), -1, -2)
    car = (crt[:, :, :, None, :] * pr[..., None] - cit[:, :, :, None, :] * pi[..., None]).reshape(2, g, n, (q + 1) * c16)
    cai = (crt[:, :, :, None, :] * pi[..., None] + cit[:, :, :, None, :] * pr[..., None]).reshape(2, g, n, (q + 1) * c16)
    kern = jnp.sum(bbr[..., None] * car[:, :, :, None, :] - bbi[..., None] * cai[:, :, :, None, :], axis=2)
    kf = kern[0, :, :, 0:qc]
    kb = kern[1].reshape(g, c16, q + 1, c16)[:, :, q - 1::-1].reshape(g, c16, qc)
    taps = (kf, kb, d_skip.astype(F32).reshape(g, c16, 1))

    bbr_t, bbi_t = jnp.swapaxes(bbr, -1, -2), jnp.swapaxes(bbi, -1, -2)
    prt, pit = jnp.swapaxes(pr, -1, -2), jnp.swapaxes(pi, -1, -2)

    def p_block(d, reverse):
        ppr = prt[d, :, q - 1::-1] if reverse else prt[d, :, 0:q]
        ppi = pit[d, :, q - 1::-1] if reverse else pit[d, :, 0:q]
        re = ppr[:, :, None, :] * bbr_t[d][:, None] - ppi[:, :, None, :] * bbi_t[d][:, None]
        im = ppr[:, :, None, :] * bbi_t[d][:, None] + ppi[:, :, None, :] * bbr_t[d][:, None]
        return re.reshape(g, qc, n), im.reshape(g, qc, n)

    pf_re, pf_im = p_block(0, True)
    pb_re, pb_im = p_block(1, False)
    p = jnp.concatenate([pf_re, pb_re, pf_im, pb_im], axis=-1)

    def r_block(d, reverse):
        if reverse:
            sel = lambda t: t[d].reshape(g, n, q + 1, c16)[:, :, q:0:-1].reshape(g, n, qc)
        else:
            sel = lambda t: t[d][:, :, c16:(q + 1) * c16]
        return sel(car), -sel(cai)

    rf_re, rf_im = r_block(0, False)
    rb_re, rb_im = r_block(1, True)
    r = jnp.concatenate([rf_re, rb_re, rf_im, rb_im], axis=1)
    a16 = jnp.stack([jnp.concatenate([pr[0, :, :, q], pr[1, :, :, q]], axis=-1),
                     jnp.concatenate([pi[0, :, :, q], pi[1, :, :, q]], axis=-1)], axis=1)
    return taps, p.astype(BF16), r.astype(BF16), a16


S5_LANE_GROUPS = 8
S5_BATCH = 4


def _s5_kernel(ut_ref, kf_ref, kb_ref, d_ref, p_ref, r_ref, a_ref, zt_ref,
               pi_ref, m_ref, x_ref, xp_ref, v_ref, sk_ref, srb_ref, y_ref, yp_ref, *, n_lat, n_ctx, bb):
    q, ng = S5_Q, S5_LANE_GROUPS
    qc = q * S5_GROUP
    width = ng * qc
    n_all = n_lat + n_ctx
    rows_lat = n_lat * bb

    @pl.when(jnp.logical_and(pl.program_id(0) == 0, pl.program_id(1) == 0))
    def _():
        src = lax.broadcasted_iota(jnp.int32, (width, qc), 0)
        dst = lax.broadcasted_iota(jnp.int32, (width, qc), 1)
        dest_in_group = ((src >> 7) << 4) + (src & 15)
        grp = (src >> 4) & (ng - 1)
        for g in range(ng):
            hit = jnp.logical_and(grp == g, dest_in_group == dst)
            pi_ref[:, g * qc:(g + 1) * qc] = jnp.where(hit, 1.0, 0.0).astype(BF16)

    @pl.when(pl.program_id(1) == 0)
    def _():
        lane = lax.broadcasted_iota(jnp.int32, (S5_GROUP, qc), 1)
        chan = lax.broadcasted_iota(jnp.int32, (S5_GROUP, qc), 0)
        for g in range(ng):
            kf, kb, dg = kf_ref[g], kb_ref[g], d_ref[g]
            for j in range(q):
                lo, hi = j * S5_GROUP, qc - (q - 1 - j) * S5_GROUP
                mf = jnp.where(lane >= lo, pltpu.roll(kf, lo, 1) if lo else kf, 0.0)
                mb = jnp.where(lane < hi, pltpu.roll(kb, hi % qc, 1) if hi % qc else kb, 0.0)
                skip = jnp.where(lane == lo + chan, dg, 0.0)
                m_ref[g, j * S5_GROUP:(j + 1) * S5_GROUP, :] = (mf + mb + skip).astype(BF16)

    for b in range(bb):
        for t in range(q):
            x_ref[t, pl.ds(b, n_all, stride=bb), :] = ut_ref[t, b].astype(F32)
    xb = jnp.concatenate([x_ref[t] for t in range(q)], axis=1).astype(BF16)
    for g in range(ng):
        cols = slice(g * qc, (g + 1) * qc)
        xg = _dot(xb, pi_ref[:, cols]).astype(BF16)
        xp_ref[:, cols] = xg
        v_ref[:, cols] = _dot(xg, p_ref[g])

    a = a_ref[...]
    a1 = jnp.concatenate([a[g, 0:1] for g in range(ng) for _ in range(2)], axis=1)
    a2 = jnp.concatenate([s * a[g, 1:2] for g in range(ng) for s in (-1.0, 1.0)], axis=1)
    fwd_lane = (lax.broadcasted_iota(jnp.int32, (bb, width), 1) & 127) < S5_STATE

    def partner(s):
        tiles = [s[:, i * 128:(i + 1) * 128] for i in range(2 * ng)]
        return jnp.concatenate([tiles[i ^ 1] for i in range(2 * ng)], axis=1)

    def advance(s, vf, vb):
        return a1 * s + a2 * partner(s) + jnp.where(fwd_lane, vf, vb)

    def scan(base, n, s0, store):
        def body(t, s):
            rf = pl.multiple_of((base + 2 * t) * bb, 2 * bb)
            rb = pl.multiple_of((base + n - 2 - 2 * t) * bb, 2 * bb)
            vf = v_ref[pl.ds(rf, 2 * bb), :]
            vb = v_ref[pl.ds(rb, 2 * bb), :]
            s1 = advance(s, vf[0:bb], vb[bb:2 * bb])
            s2 = advance(s1, vf[bb:2 * bb], vb[0:bb])
            if store:
                sk_ref[pl.ds(rf, 2 * bb), :] = jnp.concatenate([s, s1], axis=0)
                srb_ref[pl.ds(rb, 2 * bb), :] = jnp.concatenate([s1, s], axis=0)
            return s2
        return lax.fori_loop(0, n // 2, body, s0)

    s_ctx = scan(n_lat, n_ctx, jnp.zeros((bb, width), F32), False)
    scan(0, n_lat, s_ctx, True)
    lane = (lax.broadcasted_iota(jnp.int32, (rows_lat, width), 1) & 127) < S5_STATE
    st = jnp.where(lane, sk_ref[...], srb_ref[...]).astype(BF16)
    for g in range(ng):
        cols = slice(g * qc, (g + 1) * qc)
        y = _dot(xp_ref[0:rows_lat, cols], m_ref[g]) + _dot(st[:, cols], r_ref[g])
        y_ref[:, cols] = jax.nn.gelu(y).astype(BF16)
    yb = y_ref[...]
    for t in range(q):
        yp_ref[...] = _dot_nt(yb, pi_ref[t * 128:(t + 1) * 128, :])
        for b in range(bb):
            zt_ref[t, b] = yp_ref[pl.ds(b, n_lat, stride=bb), :].astype(BF16)


def _s5_mixer(ut, l, lt, mats):
    (kf, kb, dsk), p, r, a16 = mats
    q, b, _, width = ut.shape
    n_all, n_lat = lt // q, l // q
    ng = S5_LANE_GROUPS
    qc = q * S5_GROUP
    bb = min(S5_BATCH, b)
    nj = width // 128
    wspec = pl.BlockSpec((ng, qc, qc), lambda j, i: (j, 0, 0))
    tspec = pl.BlockSpec((ng, S5_GROUP, qc), lambda j, i: (j, 0, 0))
    return pl.pallas_call(
        functools.partial(_s5_kernel, n_lat=n_lat, n_ctx=n_all - n_lat, bb=bb),
        grid=(nj, b // bb),
        in_specs=[pl.BlockSpec((q, bb, n_all, 128), lambda j, i: (0, i, 0, j)),
                  tspec, tspec, pl.BlockSpec((ng, S5_GROUP, 1), lambda j, i: (j, 0, 0)), wspec, wspec,
                  pl.BlockSpec((ng, 2, 2 * S5_STATE), lambda j, i: (j, 0, 0))],
        out_specs=pl.BlockSpec((q, bb, n_lat, 128), lambda j, i: (0, i, 0, j)),
        out_shape=jax.ShapeDtypeStruct((q, b, n_lat, width), BF16),
        scratch_shapes=[pltpu.VMEM((ng * qc, ng * qc), BF16),
                        pltpu.VMEM((ng, qc, qc), BF16),
                        pltpu.VMEM((q, n_all * bb, 128), F32),
                        pltpu.VMEM((n_all * bb, ng * qc), BF16),
                        pltpu.VMEM((n_all * bb, ng * qc), F32),
                        pltpu.VMEM((n_lat * bb, ng * qc), F32),
                        pltpu.VMEM((n_lat * bb, ng * qc), F32),
                        pltpu.VMEM((n_lat * bb, ng * qc), BF16),
                        pltpu.VMEM((n_lat * bb, 128), F32)],
        compiler_params=_cparams(("arbitrary", "arbitrary")),
        name="s5_scan",
    )(ut, kf, kb, dsk, p, r, a16)


def _inv_unit_triangular_many(mats):
    c = mats[0].shape[0]
    eye = (lax.broadcasted_iota(jnp.int32, (c, c), 0) == lax.broadcasted_iota(jnp.int32, (c, c), 1)).astype(F32)
    prods = [eye - a for a in mats]
    pows = [a.astype(BF16) for a in mats]
    for _ in range(int(np.log2(c)) - 1):
        pows = [_dot(a, a).astype(BF16) for a in pows]
        prods = [p + _dot(p.astype(BF16), a) for p, a in zip(prods, pows)]
    return prods


def _gdn_prep_kernel(qkv_ref, cw_ref, ba_ref, bat_ref, prow_ref, pcol_ref,
                     u_ref, wq_ref, qk_ref, kdt_ref, dl_ref, *, n_lat_blocks, ctx_len):
    j = pl.program_id(1)
    rows = qkv_ref.shape[1]
    c = GDN_CHUNK
    nchunk = rows // c
    seg = jnp.where(j >= n_lat_blocks, ctx_len, GRID_W)
    x = qkv_ref[0]
    ri = lax.broadcasted_iota(jnp.int32, (rows, rows), 0)
    ci = lax.broadcasted_iota(jnp.int32, (rows, rows), 1)
    same_seg = (ci & -seg) == (ri & -seg)
    acc = x.astype(F32) * cw_ref[2:3, :]
    for s in (-2, -1, 1, 2):
        sel = jnp.where(jnp.logical_and(ci == ri + s, same_seg), 1.0, 0.0).astype(BF16)
        acc = acc + _dot(sel, x) * cw_ref[2 + s:3 + s, :]
    act = acc * _sigmoid(acc)

    ii = lax.broadcasted_iota(jnp.int32, (c, c), 0)
    jj = lax.broadcasted_iota(jnp.int32, (c, c), 1)
    incl = (ii >= jj, ii <= jj)
    strict = (ii > jj, ii < jj)
    tri_low = incl[0].astype(F32)
    tri_up = incl[1].astype(F32)
    eye_bf = (ii == jj).astype(BF16)
    ones = jnp.ones((c, c), F32)
    lane = lax.broadcasted_iota(jnp.int32, (c, 128), 1)
    prow = prow_ref[...]
    pcol = pcol_ref[...]
    hp = dict(precision=HIGHEST, preferred_element_type=F32)

    gates = []
    for ci in range(nchunk):
        ba = ba_ref[0, ci * c:(ci + 1) * c, :]
        g_all = -prow[0:1, :] * _softplus(ba + prow[1:2, :])
        g_all = jnp.where(jnp.logical_and(lane >= 8, lane < 16), g_all, 0.0)
        bat = bat_ref[0, ci]
        g_row = -pcol[:, 0:1] * _softplus(bat[8:16, :] + pcol[:, 1:2])
        gates.append(dict(beta=_sigmoid(ba),
                          gc=(jnp.dot(tri_low, g_all, **hp), jnp.dot(tri_up, g_all, **hp)),
                          gtot=jnp.dot(ones, g_all, **hp),
                          gr=(jnp.dot(g_row, tri_up, **hp), jnp.dot(g_row, tri_low, **hp))))

    qn, kn, vv = [], [], []
    for h in range(GDN_HEADS):
        qh = act[:, h * 128:(h + 1) * 128]
        kh = act[:, 512 + h * 128:512 + (h + 1) * 128]
        qn.append(qh * lax.rsqrt(jnp.sum(qh * qh, axis=-1, keepdims=True) + EPS) * (GDN_DK ** -0.5))
        kn.append(kh * lax.rsqrt(jnp.sum(kh * kh, axis=-1, keepdims=True) + EPS))
        vv.append(act[:, 1024 + h * 128:1024 + (h + 1) * 128])

    pairs = [(ci, h) for ci in range(nchunk) for h in range(GDN_HEADS)]
    sl = lambda t, ci: t[ci * c:(ci + 1) * c]
    kb = {p: sl(kn[p[1]], p[0]).astype(BF16) for p in pairs}
    kk = {p: _dot_nt(kb[p], kb[p]) for p in pairs}
    qkm = {p: _dot_nt(sl(qn[p[1]], p[0]).astype(BF16), kb[p]) for p in pairs}

    probs = [(ci, h, d) for ci in range(nchunk) for h in range(GDN_HEADS) for d in range(2)]
    beta, gc, gt, decay, a_mats = {}, {}, {}, {}, []
    for (ci, h, d) in probs:
        col = d * GDN_HEADS + h
        gi = gates[ci]
        beta[ci, h, d] = gi["beta"][:, col:col + 1]
        gc[ci, h, d] = gi["gc"][d][:, 8 + col:9 + col]
        gt[ci, h, d] = gi["gtot"][:, 8 + col:9 + col]
        grow = gi["gr"][d][col:col + 1, :]
        decay[ci, h, d] = jnp.exp(jnp.where(incl[d], gc[ci, h, d] - grow, -jnp.inf))
        a_mats.append(jnp.where(strict[d], beta[ci, h, d] * kk[ci, h] * decay[ci, h, d], 0.0))
    tinv = _inv_unit_triangular_many(a_mats)

    eg, sols, kdts = {}, {}, {}
    for n, (ci, h, d) in enumerate(probs):
        p = (ci, h, d)
        eg[p] = jnp.exp(gc[p])
        kh = sl(kn[h], ci)
        rhs = jnp.concatenate([sl(vv[h], ci) * beta[p], kh * (beta[p] * eg[p])], axis=-1).astype(BF16)
        sols[p] = _dot(tinv[n].astype(BF16), rhs)
        kdts[p] = _dot_tn((kh * jnp.exp(gt[p] - gc[p])).astype(BF16), eye_bf)
    for (ci, h, d) in probs:
        p = (ci, h, d)
        u_ref[0, d, h, ci] = sols[p][:, 0:128]
        wq_ref[0, d, h, ci, 0:c, :] = sols[p][:, 128:256].astype(BF16)
        wq_ref[0, d, h, ci, c:2 * c, :] = (sl(qn[h], ci) * eg[p]).astype(BF16)
        qk_ref[0, d, h, ci] = (qkm[ci, h] * decay[p]).astype(BF16)
        kdt_ref[0, d, h, ci] = kdts[p].astype(BF16)
        dl_ref[0, d, h, ci] = jnp.broadcast_to(jnp.exp(gt[p][0:8, :]), (8, 128))


def _gdn_prepare(qkv, conv_w, ba, bat_chunks, prow, pcol, l, lt):
    b, _, width = qkv.shape
    tm = ROW_TILE
    c = GDN_CHUNK
    cpb = tm // c
    nblk = lt // tm
    nch = lt // c
    hd = (b, 2, GDN_HEADS, nch)

    def blk(shape_tail, dtype):
        return (pl.BlockSpec((1, 2, GDN_HEADS, cpb) + shape_tail, lambda i, j: (i, 0, 0, j, 0, 0)),
                jax.ShapeDtypeStruct(hd + shape_tail, dtype))

    specs = [blk((c, 128), F32), blk((2 * c, 128), BF16), blk((c, c), BF16), blk((GDN_DK, c), BF16),
             blk((8, 128), F32)]
    return pl.pallas_call(
        functools.partial(_gdn_prep_kernel, n_lat_blocks=l // tm, ctx_len=lt - l),
        grid=(b, nblk),
        in_specs=[pl.BlockSpec((1, tm, width), lambda i, j: (i, j, 0)),
                  pl.BlockSpec((8, width), lambda i, j: (0, 0)),
                  pl.BlockSpec((1, tm, 128), lambda i, j: (i, j, 0)),
                  pl.BlockSpec((1, cpb, 16, c), lambda i, j: (i, j, 0, 0)),
                  pl.BlockSpec((2, 128), lambda i, j: (0, 0)),
                  pl.BlockSpec((8, 2), lambda i, j: (0, 0))],
        out_specs=[s for s, _ in specs],
        out_shape=[o for _, o in specs],
        compiler_params=_cparams(("parallel", "parallel")),
        name="gdn_prepare",
    )(qkv, conv_w, ba, bat_chunks, prow, pcol)


SCAN_GROUP = 2


def _gdn_scan_kernel(uf, wqf, qkf, kdtf, dlf, ub, wqb, qkb, kdtb, dlb, of_ref, ob_ref, s_ref):
    t = pl.program_id(0)
    nb = s_ref.shape[0]
    c = GDN_CHUNK

    @pl.when(t == 0)
    def _():
        s_ref[...] = jnp.zeros_like(s_ref)

    refs = ((uf, wqf, qkf, kdtf, dlf, of_ref), (ub, wqb, qkb, kdtb, dlb, ob_ref))
    for b0 in range(0, nb, SCAN_GROUP):
        chains = [(bi, d, h) for bi in range(b0, min(b0 + SCAN_GROUP, nb)) for d in range(2)
                  for h in range(GDN_HEADS)]
        s = {k: s_ref[k[0], k[1], k[2]] for k in chains}
        sb = {k: s[k].astype(BF16) for k in chains}
        r = {k: _dot(refs[k[1]][1][k[0], 0, k[2], 0], sb[k]) for k in chains}
        vb = {k: (refs[k[1]][0][k[0], 0, k[2], 0] - r[k][0:c]).astype(BF16) for k in chains}
        o = {k: r[k][c:2 * c] + _dot(refs[k[1]][2][k[0], 0, k[2], 0], vb[k]) for k in chains}
        sn = {k: s[k] * refs[k[1]][4][k[0], 0, k[2], 0][0:1, :] + _dot(refs[k[1]][3][k[0], 0, k[2], 0], vb[k])
              for k in chains}
        for k in chains:
            s_ref[k[0], k[1], k[2]] = sn[k]
            refs[k[1]][5][k[0], :, k[2] * 128:(k[2] + 1) * 128] = o[k]


def _gdn_scan(prep, l):
    u = prep[0]
    b, _, heads, nch, c, _ = u.shape
    n_lat = l // c
    n_ctx = nch - n_lat
    lt = nch * c

    def fwd_chunk(t):
        return jnp.where(t < n_ctx, n_lat + t, t - n_ctx)

    def bwd_chunk(t):
        return nch - 1 - t

    def spec(arr, chunk_of, d):
        tail = arr.shape[4:]
        return pl.BlockSpec((b, 1, heads, 1) + tail, lambda t: (0, d, 0, chunk_of(t), 0, 0))

    ins = [spec(a, fwd_chunk, 0) for a in prep] + [spec(a, bwd_chunk, 1) for a in prep]
    width = heads * 128
    return pl.pallas_call(
        _gdn_scan_kernel,
        grid=(nch,),
        in_specs=ins,
        out_specs=[pl.BlockSpec((b, c, width), lambda t: (0, fwd_chunk(t), 0)),
                   pl.BlockSpec((b, c, width), lambda t: (0, bwd_chunk(t), 0))],
        out_shape=[jax.ShapeDtypeStruct((b, lt, width), F32)] * 2,
        scratch_shapes=[pltpu.VMEM((b, 2, heads, GDN_DK, 128), F32)],
        compiler_params=_cparams(("arbitrary",)),
        name="gdn_scan",
    )(*prep, *prep)


def _merge_kernel(x_ref, z_ref, of_ref, ob_ref, gate_ref, br_ref, gt1_ref, sc2_ref, sh2_ref,
                  wglu_ref, bglu_ref, gnorm_ref, wa_ref, wb_ref, wo_ref, n2_ref, wr_ref, wrhi_ref, brt_ref,
                  x1_ref, h2_ref, lg_ref, zs_ref):
    nchunk = zs_ref.shape[1] // S5_Q
    for t in range(S5_Q):
        zt = z_ref[t, 0].astype(F32)
        for c in range(zs_ref.shape[0]):
            zs_ref[c, pl.ds(t, nchunk, stride=S5_Q), :] = zt[:, c * 128:(c + 1) * 128]
    zf = jnp.concatenate([zs_ref[c] for c in range(zs_ref.shape[0])], axis=1)
    z = zf.astype(BF16)
    ya = zf * _sigmoid(_dot(z, wglu_ref[...]) + bglu_ref[...])
    o = of_ref[0] + ob_ref[0]
    gate = gate_ref[0].astype(F32)
    parts = []
    for h in range(GDN_HEADS):
        oh = o[:, h * 128:(h + 1) * 128]
        parts.append(oh * lax.rsqrt(jnp.mean(oh * oh, axis=-1, keepdims=True) + EPS) * gnorm_ref[...])
    yb = jnp.concatenate(parts, axis=-1) * (gate * _sigmoid(gate))
    br = br_ref[0].astype(F32)
    d = x_ref.shape[2]
    ga = _sigmoid(br[:, 0:d])
    gb = _sigmoid(br[:, d:2 * d])
    m = ga * _dot(ya.astype(BF16), wa_ref[...]) + gb * _dot(yb.astype(BF16), wb_ref[...])
    mix = _dot(m.astype(BF16), wo_ref[...])
    x1 = x_ref[0] + gt1_ref[0] * mix
    x1_ref[0] = x1
    y2 = x1 * lax.rsqrt(jnp.mean(x1 * x1, axis=-1, keepdims=True) + EPS) * n2_ref[...]
    h2 = y2 * (1.0 + sc2_ref[0]) + sh2_ref[0]
    h2b = h2.astype(BF16)
    h2_ref[0, 0] = h2b
    h2_ref[1, 0] = h2b
    h2_lo = (h2 - h2b.astype(F32)).astype(BF16)
    rl = _dot(h2b, wr_ref[...])
    lg_ref[0] = rl[:, 0:128] + rl[:, 128:256] + _dot(h2_lo, wrhi_ref[...]) + brt_ref[...]


def _merge(x, z, o_f, o_b, gate, br, mod3, wglu, bglu, gnorm, wa, wb, wo, norm2, wr, wr_hi, brt):
    b, l, d = x.shape
    tm = min(MERGE_TILE, l)
    tok = lambda width: pl.BlockSpec((1, tm, width), lambda i, j: (i, j, 0))
    modspec = lambda k: pl.BlockSpec((1, 1, d), lambda i, j: (i * 6 + k, 0, 0))
    full = lambda arr: pl.BlockSpec(arr.shape, lambda i, j: (0,) * arr.ndim)
    consts = [wglu, bglu, gnorm, wa, wb, wo, norm2, wr, wr_hi, brt]
    return pl.pallas_call(
        _merge_kernel,
        grid=(b, l // tm),
        in_specs=[tok(d), pl.BlockSpec((S5_Q, 1, tm // S5_Q, 512), lambda i, j: (0, i, j, 0)),
                  tok(512), tok(512), tok(512), tok(2 * d),
                  modspec(2), modspec(4), modspec(3)] + [full(a) for a in consts],
        out_specs=[tok(d), pl.BlockSpec((2, 1, tm, d), lambda i, j: (0, i, j, 0)), tok(128)],
        out_shape=[jax.ShapeDtypeStruct((b, l, d), F32),
                   jax.ShapeDtypeStruct((2, b, l, d), BF16),
                   jax.ShapeDtypeStruct((b, l, 128), F32)],
        scratch_shapes=[pltpu.VMEM((4, tm, 128), F32)],
        compiler_params=_cparams(("parallel", "parallel")),
        name="branch_merge",
    )(x, z, o_f, o_b, gate, br, mod3, mod3, mod3, *consts)


def _moe_weight_copies(wgu_hbm, wd_hbm, wgu_buf, wd_buf, sem, expert, slot):
    return (pltpu.make_async_copy(wgu_hbm.at[expert], wgu_buf.at[slot], sem.at[0, slot]),
            pltpu.make_async_copy(wd_hbm.at[expert], wd_buf.at[slot], sem.at[1, slot]))


def _moe_kernel(te_ref, nu_ref, nxt_ref, slot_ref, x_ref, bgu_ref, bd_ref, wgu_hbm, wd_hbm, y_ref,
                wgu_buf, wd_buf, wgu_bf, wd_bf, sem):
    i = pl.program_id(0)
    used = i < nu_ref[0]
    copies = functools.partial(_moe_weight_copies, wgu_hbm, wd_hbm, wgu_buf, wd_buf, sem)

    @pl.when(jnp.logical_and(i == 0, used))
    def _():
        for cp in copies(te_ref[0], slot_ref[0]):
            cp.start()

    first = jnp.logical_or(i == 0, te_ref[i] != te_ref[jnp.maximum(i - 1, 0)])

    @pl.when(jnp.logical_and(first, used))
    def _():
        slot = slot_ref[i]
        for cp in copies(te_ref[i], slot):
            cp.wait()
        wgu_bf[...] = wgu_buf[slot].astype(BF16)
        wd_bf[...] = wd_buf[slot].astype(BF16)

        @pl.when(nxt_ref[i] >= 0)
        def _():
            for cp in copies(nxt_ref[i], 1 - slot):
                cp.start()

    @pl.when(used)
    def _():
        de = wd_bf.shape[0]
        x = x_ref[...]
        acc = None
        for c0 in range(0, de, MOE_HIDDEN_CHUNK):
            c1 = c0 + MOE_HIDDEN_CHUNK
            gate = jnp.minimum(_dot(x, wgu_bf[:, c0:c1]) + bgu_ref[0, :, c0:c1], SWIGLU_LIMIT)
            up = jnp.clip(_dot(x, wgu_bf[:, de + c0:de + c1]) + bgu_ref[0, :, de + c0:de + c1],
                          -SWIGLU_LIMIT, SWIGLU_LIMIT)
            act = ((up + 1.0) * gate * _sigmoid(gate * SWIGLU_ALPHA)).astype(BF16)
            part = _dot(act, wd_bf[c0:c1, :])
            acc = part if acc is None else acc + part
        y_ref[...] = (acc + bd_ref[0]).astype(y_ref.dtype)

    @pl.when(jnp.logical_not(used))
    def _():
        y_ref[...] = jnp.zeros_like(y_ref)


def _moe_experts(xs, plan, w_gate_up, b_gate_up, w_down, b_down):
    tile_expert, n_used, next_expert, slot = plan
    p, d = xs.shape
    tm = MOE_TILE
    e, _, n2 = w_gate_up.shape
    de = w_down.shape[1]
    grid_spec = pltpu.PrefetchScalarGridSpec(
        num_scalar_prefetch=4,
        grid=(p // tm,),
        in_specs=[pl.BlockSpec((tm, d), lambda i, te, nu, nx, sl: (jnp.minimum(i, jnp.maximum(nu[0] - 1, 0)), 0)),
                  pl.BlockSpec((1, 1, n2), lambda i, te, nu, nx, sl: (te[i], 0, 0)),
                  pl.BlockSpec((1, 1, d), lambda i, te, nu, nx, sl: (te[i], 0, 0)),
                  pl.BlockSpec(memory_space=pl.ANY),
                  pl.BlockSpec(memory_space=pl.ANY)],
        out_specs=pl.BlockSpec((tm, d), lambda i, te, nu, nx, sl: (i, 0)),
        scratch_shapes=[pltpu.VMEM((2, d, n2), F32), pltpu.VMEM((2, de, d), F32),
                        pltpu.VMEM((d, n2), BF16), pltpu.VMEM((de, d), BF16),
                        pltpu.SemaphoreType.DMA((2, 2))],
    )
    return pl.pallas_call(
        _moe_kernel,
        grid_spec=grid_spec,
        out_shape=jax.ShapeDtypeStruct((p, d), BF16),
        compiler_params=_cparams(("arbitrary",)),
        name="moe_experts",
    )(tile_expert, n_used, next_expert, slot, xs, b_gate_up.reshape(e, 1, n2), b_down.reshape(e, 1, d),
      w_gate_up, w_down)


def _route(logits, n_tokens):
    tm = MOE_TILE
    top_val, top_idx = lax.top_k(logits, TOP_K)
    weights = jax.nn.softmax(top_val, axis=-1)
    flat_e = top_idx.reshape(-1).astype(jnp.int32)
    n_assign = flat_e.shape[0]
    n_fill = N_EXPERTS * tm
    n_rows = n_assign + n_fill
    counts = jnp.sum(jax.nn.one_hot(flat_e, N_EXPERTS, dtype=jnp.int32), axis=0)
    padded = ((counts + tm - 1) // tm) * tm
    pad_end = jnp.cumsum(padded)
    fill = jnp.arange(n_fill, dtype=jnp.int32)
    fill_e = jnp.sum((fill[:, None] >= jnp.cumsum(padded - counts)[None, :]).astype(jnp.int32), axis=1)
    bits = int(n_rows - 1).bit_length()
    assert (N_EXPERTS + 1) << bits < 2 ** 31
    keys = jnp.concatenate([flat_e, fill_e])
    packed = (keys << bits) | jnp.arange(n_rows, dtype=jnp.int32)
    s_origin = lax.sort(packed) & ((1 << bits) - 1)
    src_token = jnp.where(s_origin < n_assign, s_origin // TOP_K, (s_origin - n_assign) % n_tokens)
    _, row_of = lax.sort((s_origin, jnp.arange(n_rows, dtype=jnp.int32)), num_keys=1)
    pos = row_of[0:n_assign].reshape(n_tokens, TOP_K)
    n_tiles = n_rows // tm
    n_used = (pad_end[-1] // tm).astype(jnp.int32)
    tile_start = jnp.arange(n_tiles, dtype=jnp.int32) * tm
    tile_expert = jnp.sum((tile_start[:, None] >= pad_end[None, :]).astype(jnp.int32), axis=1)
    last_e = jnp.sum((pad_end[-1] - 1 >= pad_end).astype(jnp.int32))
    tile_expert = jnp.minimum(tile_expert, last_e).astype(jnp.int32)
    nonempty = counts > 0
    idx = jnp.arange(N_EXPERTS, dtype=jnp.int32)
    later = jnp.where(jnp.logical_and(nonempty[None, :], idx[None, :] > idx[:, None]), idx[None, :], N_EXPERTS)
    next_nonempty = jnp.min(later, axis=1)
    next_nonempty = jnp.where(next_nonempty < N_EXPERTS, next_nonempty, -1).astype(jnp.int32)
    ordinal = jnp.cumsum(nonempty.astype(jnp.int32)) - 1
    plan = (tile_expert, n_used.reshape(1), next_nonempty[tile_expert], (ordinal[tile_expert] & 1).astype(jnp.int32))
    return weights, src_token, pos, plan


def _final_kernel(x1_ref, y0_ref, y1_ref, y2_ref, y3_ref, wt_ref, gt2_ref, nf_ref, o_ref):
    wt = wt_ref[0]
    moe = (wt[:, 0:1] * y0_ref[0, 0].astype(F32) + wt[:, 1:2] * y1_ref[0, 0].astype(F32)
           + wt[:, 2:3] * y2_ref[0, 0].astype(F32) + wt[:, 3:4] * y3_ref[0, 0].astype(F32))
    x2 = x1_ref[0] + gt2_ref[0] * moe
    o_ref[0] = x2 * lax.rsqrt(jnp.mean(x2 * x2, axis=-1, keepdims=True) + EPS) * nf_ref[...]


def _final(x1, yg, wt, mod3, norm_f):
    b, l, d = x1.shape
    tm = ROW_TILE
    tok = lambda width: pl.BlockSpec((1, tm, width), lambda i, j: (i, j, 0))
    ysel = lambda k: pl.BlockSpec((1, 1, tm, d), lambda i, j: (k, i, j, 0))
    return pl.pallas_call(
        _final_kernel,
        grid=(b, l // tm),
        in_specs=[tok(d)] + [ysel(k) for k in range(TOP_K)] + [tok(128), pl.BlockSpec((1, 1, d), lambda i, j: (i * 6 + 5, 0, 0)),
                                 pl.BlockSpec((1, d), lambda i, j: (0, 0))],
        out_specs=tok(d),
        out_shape=jax.ShapeDtypeStruct((b, l, d), F32),
        compiler_params=_cparams(("parallel", "parallel")),
        name="combine_final_norm",
    )(x1, *([yg] * TOP_K), wt, mod3, norm_f.reshape(1, d))


def _split_hi_lo(w):
    hi = w.astype(BF16)
    lo = (w - hi.astype(F32)).astype(BF16)
    return hi, lo


def kernel(x, c, ctx, c_ctx, w_mod, b_mod, norm1, w_in, s5_lam_re, s5_lam_im, s5_log_step, s5_b_re, s5_b_im, s5_c_re, s5_c_im, s5_d, s5_w_glu, s5_b_glu, gdn_conv, gdn_a_log, gdn_dt_bias, gdn_norm, w_branch_a, w_branch_b, w_out, norm2, w_router, b_router, w_gate_up, b_gate_up, w_down, b_down, norm_f):
    b, l, d = x.shape
    lc = ctx.shape[1]
    depth = w_mod.shape[0]
    assert depth == 1, "single-layer block: the context stream has no consumer after the token mixer"
    assert l % ROW_TILE == 0 and lc == ROW_TILE and lc & (lc - 1) == 0 and b <= 8
    ly = 0

    cc = jnp.zeros((16, d), F32).at[0:b].set(c).at[b].set(c_ctx)
    mod = _modulation(cc, w_mod[ly], b_mod[ly])
    mod3 = mod[0:b + 1].reshape((b + 1) * 6, 1, d)

    wi = w_in[ly]
    o_u, o_qkv, o_gate, o_ba, o_br = 0, 512, 2048, 2560, 2576
    w_ba = wi[:, o_ba:o_br]
    ba_hi, ba_lo = _split_hi_lo(w_ba)
    pad = lambda t: jnp.pad(t, ((0, 0), (0, 128 - t.shape[1])))
    w_main = jnp.concatenate([wi[:, o_u:o_ba].astype(BF16), wi[:, o_br:].astype(BF16), pad(ba_hi), pad(ba_lo)], axis=1)
    wt = jnp.concatenate([ba_hi.T, ba_lo.T], axis=0)
    ut, qkv, gate, br, ba, bat = _in_projection(x, ctx, mod3, norm1[ly], w_main, pad(ba_hi), wt, ba_hi.T)

    mats = _s5_matrices(s5_lam_re[ly], s5_lam_im[ly], s5_log_step[ly], s5_b_re[ly], s5_b_im[ly],
                        s5_c_re[ly], s5_c_im[ly], s5_d[ly])
    lt = l + lc
    z = _s5_mixer(ut, l, lt, mats)

    nch = lt // GDN_CHUNK
    bat_chunks = jnp.transpose(bat[:, :, 0:lt].reshape(b, 16, nch, GDN_CHUNK), (0, 2, 1, 3))
    ea = jnp.exp(gdn_a_log[ly].astype(F32)).reshape(-1)
    dtb = gdn_dt_bias[ly].astype(F32).reshape(-1)
    prow = jnp.zeros((2, 128), F32).at[0, 8:16].set(ea).at[1, 8:16].set(dtb)
    pcol = jnp.stack([ea, dtb], axis=1)
    conv_w = jnp.zeros((8, qkv.shape[2]), F32).at[0:CONV_K].set(gdn_conv[ly].astype(F32))
    prep = _gdn_prepare(qkv, conv_w, ba, bat_chunks, prow, pcol, l, lt)
    o_f, o_b = _gdn_scan(prep, l)

    wr_hi, wr_lo = _split_hi_lo(jnp.pad(w_router[ly].astype(F32), ((0, 0), (0, 128 - N_EXPERTS))))
    wr = jnp.concatenate([wr_hi, wr_lo], axis=1)
    brt = jnp.pad(b_router[ly].astype(F32), (0, 128 - N_EXPERTS)).reshape(1, 128)
    x1, h2, logits = _merge(x, z, o_f, o_b, gate, br, mod3,
                            s5_w_glu[ly].astype(BF16), s5_b_glu[ly].astype(F32).reshape(1, -1),
                            gdn_norm[ly].astype(F32).reshape(1, -1),
                            w_branch_a[ly].astype(BF16), w_branch_b[ly].astype(BF16), w_out[ly].astype(BF16),
                            norm2[ly].astype(F32).reshape(1, d), wr, wr_hi, brt)

    n_tok = b * l
    weights, src_token, pos, plan = _route(logits.reshape(n_tok, 128)[:, 0:N_EXPERTS], n_tok)
    h2f = h2.reshape(2 * n_tok, d)
    xs = h2f[src_token]
    ys = _moe_experts(xs, plan, w_gate_up[ly], b_gate_up[ly], w_down[ly], b_down[ly])
    yg = ys[pos.T.reshape(-1)].reshape(TOP_K, b, l, d)
    wt4 = jnp.pad(weights, ((0, 0), (0, 128 - TOP_K))).reshape(b, l, 128)
    return _final(x1, yg, wt4, mod3, norm_f)
```
